```python
import math
import jax, jax.numpy as jnp
from jax import lax
import numpy as np


D_MODEL = 1024
BATCH = 2
SEQ = 16384
DEPTH = 4

HEAD_DIM = 64
N_RWKV_HEADS = 8
D_RWKV = N_RWKV_HEADS * HEAD_DIM
N_DIFF_HEADS = 4
D_DIFF = N_DIFF_HEADS * 2 * HEAD_DIM
DECAY_LORA = 64
AAA_LORA = 64
GATE_LORA = 128
RWKV_COLS = 3 * D_RWKV + DECAY_LORA + AAA_LORA + GATE_LORA
AB_COLS = RWKV_COLS + 3 * D_DIFF
RWKV_GN_EPS = 64e-5
ROPE_THETA = 500000.0
ROPE_DIM = HEAD_DIM // 4
Q_BLOCK = 128
CHUNK = 128
GMLP_GROUPS = 8
D_GMLP = D_MODEL
D_FF = 2816
CONV_W = 3
N_AB = (DEPTH + 1) // 2
N_C = DEPTH // 2
DEEPNORM_ALPHA = (2 * DEPTH) ** 0.25
DEEPNORM_BETA = (8 * DEPTH) ** -0.25

kernel_name = 'hybrid_rwkv7_diffattn_gmlp_convffn'


def layer_norm(x, g, b, eps=1e-5):
    xf = x.astype(jnp.float32)
    mu = jnp.mean(xf, axis=-1, keepdims=True)
    var = jnp.mean(jnp.square(xf - mu), axis=-1, keepdims=True)
    y = (xf - mu) * lax.rsqrt(var + eps)
    return (y * g.astype(jnp.float32) + b.astype(jnp.float32)).astype(x.dtype)


def rms_norm(x, g, eps=1e-5):
    xf = x.astype(jnp.float32)
    y = xf * lax.rsqrt(jnp.mean(jnp.square(xf), axis=-1, keepdims=True) + eps)
    return (y * g.astype(jnp.float32)).astype(x.dtype)


def shift_prev(t):
    return jnp.pad(t[:, :-1], ((0, 0), (1, 0), (0, 0)))


def rope_tables(seq):
    inv_freq = ROPE_THETA ** (-jnp.arange(0, ROPE_DIM, 2, dtype=jnp.float32) / ROPE_DIM)
    ang = jnp.arange(seq, dtype=jnp.float32)[:, None] * inv_freq[None, :]
    return jnp.cos(ang), jnp.sin(ang)


def apply_partial_rope(t, cos, sin):
    half = ROPE_DIM // 2
    c = cos[None, :, None, None, :].astype(t.dtype)
    s = sin[None, :, None, None, :].astype(t.dtype)
    t1, t2, rest = t[..., :half], t[..., half:ROPE_DIM], t[..., ROPE_DIM:]
    return jnp.concatenate([t1 * c - t2 * s, t1 * s + t2 * c, rest], axis=-1)


def wkv7_scan(r, decay, k, v, a_vec, b_vec):
    B, S, H, N = r.shape
    xs = tuple(jnp.moveaxis(t.astype(jnp.float32), 1, 0) for t in (r, decay, k, v, a_vec, b_vec))

    def step(state, inp):
        r_t, w_t, k_t, v_t, a_t, b_t = inp
        sa = jnp.einsum('bhvk,bhk->bhv', state, a_t)
        state = (state * w_t[:, :, None, :] + sa[..., None] * b_t[:, :, None, :]
                 + v_t[..., None] * k_t[:, :, None, :])
        y_t = jnp.einsum('bhvk,bhk->bhv', state, r_t)
        return state, y_t

    s0 = jnp.zeros((B, H, N, N), jnp.float32)
    _, y = lax.scan(step, s0, xs)
    return jnp.moveaxis(y, 0, 1)


def rwkv7_mix(pr, mu, w0, w2, a0, a2, g2, k_k, k_a, r_k, lnx_g, lnx_b):
    B, S, _ = pr.shape

    def heads(t):
        return t.reshape(B, S, N_RWKV_HEADS, HEAD_DIM)

    xs = pr + mu * (shift_prev(pr) - pr)
    r, k, v, xw, xa, xg = jnp.split(
        xs, [D_RWKV, 2 * D_RWKV, 3 * D_RWKV, 3 * D_RWKV + DECAY_LORA,
             3 * D_RWKV + DECAY_LORA + AAA_LORA], axis=-1)
    w = -jax.nn.softplus(-(w0 + jnp.tanh(xw) @ w2)) - 0.5
    decay = jnp.exp(-jnp.exp(w.astype(jnp.float32)))
    a = jax.nn.sigmoid(a0 + xa @ a2)
    g = jax.nn.sigmoid(xg) @ g2
    kk = heads(k * k_k).astype(jnp.float32)
    kk = kk / jnp.maximum(jnp.sqrt(jnp.sum(jnp.square(kk), axis=-1, keepdims=True)), 1e-12)
    k = k * (1.0 + (a - 1.0) * k_a)
    r_h, k_h, v_h, a_h = heads(r), heads(k), heads(v), heads(a)
    y = wkv7_scan(r_h, heads(decay), k_h, v_h, -kk, kk * a_h.astype(jnp.float32))
    mu_y = jnp.mean(y, axis=-1, keepdims=True)
    var_y = jnp.mean(jnp.square(y - mu_y), axis=-1, keepdims=True)
    yn = ((y - mu_y) * lax.rsqrt(var_y + RWKV_GN_EPS)).reshape(B, S, D_RWKV)
    yn = (yn * lnx_g.astype(jnp.float32) + lnx_b.astype(jnp.float32)).astype(pr.dtype)
    bonus = jnp.sum(r_h * k_h * r_k, axis=-1, keepdims=True) * v_h
    return (yn + bonus.reshape(B, S, D_RWKV)) * g


def diff_attention(pd, lam_q1, lam_k1, lam_q2, lam_k2, subln_g, lam_init, cos, sin):
    B, S, _ = pd.shape
    q, k, v = jnp.split(pd, 3, axis=-1)
    q = q.reshape(B, S, N_DIFF_HEADS, 2, HEAD_DIM)
    k = k.reshape(B, S, N_DIFF_HEADS, 2, HEAD_DIM)
    v = v.reshape(B, S, N_DIFF_HEADS, 2 * HEAD_DIM)
    q = apply_partial_rope(q, cos, sin) * (HEAD_DIM ** -0.5)
    k = apply_partial_rope(k, cos, sin)
    lam = (jnp.exp(jnp.sum(lam_q1.astype(jnp.float32) * lam_k1.astype(jnp.float32)))
           - jnp.exp(jnp.sum(lam_q2.astype(jnp.float32) * lam_k2.astype(jnp.float32)))
           + lam_init)
    n_blocks = S // Q_BLOCK
    qb = jnp.moveaxis(q.reshape(B, n_blocks, Q_BLOCK, N_DIFF_HEADS, 2, HEAD_DIM), 1, 0)
    k_pos = jnp.arange(S)

    def block(args):
        q_blk, i = args
        q_pos = i * Q_BLOCK + jnp.arange(Q_BLOCK)
        s = jnp.einsum('bqhcd,bkhcd->bhcqk', q_blk, k).astype(jnp.float32)
        mask = k_pos[None, :] <= q_pos[:, None]
        s = jnp.where(mask[None, None, None], s, -jnp.inf)
        p = jax.nn.softmax(s, axis=-1)
        attn = p[:, :, 0] - lam * p[:, :, 1]
        return jnp.einsum('bhqk,bkhd->bqhd', attn.astype(v.dtype), v)

    o = lax.map(block, (qb, jnp.arange(n_blocks)))
    o = jnp.moveaxis(o, 0, 1).reshape(B, S, N_DIFF_HEADS, 2 * HEAD_DIM)
    o = rms_norm(o, subln_g) * (1.0 - lam_init)
    return o.reshape(B, S, D_DIFF)


def chunked_spatial_gating(x, w_in, b_in, ln_g, ln_b, w_s, b_s):
    B, S, _ = x.shape
    h = jax.nn.gelu(x @ w_in + b_in, approximate=False)
    u, v = jnp.split(h, 2, axis=-1)
    v = layer_norm(v, ln_g, ln_b)
    vc = v.reshape(B, S // CHUNK, CHUNK, GMLP_GROUPS, D_GMLP // GMLP_GROUPS)
    ws = w_s * jnp.tril(jnp.ones((CHUNK, CHUNK), w_s.dtype))
    mixed = jnp.einsum('gts,bcsgd->bctgd', ws, vc) + b_s.T[None, None, :, :, None]
    return u * mixed.reshape(B, S, D_GMLP)


def conv_ffn(x, w_up, conv_w, conv_b, w_down):
    S = x.shape[1]
    gate, val = jnp.split(x @ w_up, 2, axis=-1)
    gp = jnp.pad(gate, ((0, 0), (CONV_W - 1, 0), (0, 0)))
    conv = conv_b
    for j in range(CONV_W):
        conv = conv + conv_w[j] * gp[:, j:j + S]
    return (jax.nn.silu(conv) * val) @ w_down


def setup_inputs(seed: int = 0) -> dict:
    key = jax.random.key(seed)
    keys = jax.random.split(key, 48)
    counter = [0]
    f32 = jnp.float32

    def nxt():
        kk = keys[counter[0]]
        counter[0] += 1
        return kk

    def nrm(shape, scale):
        return jax.random.normal(nxt(), shape, f32) * scale

    def unif(shape, lo, hi):
        return jax.random.uniform(nxt(), shape, f32, lo, hi)

    D = D_MODEL
    return {
        'x': nrm((BATCH, SEQ, D), 1.0),
        'ab_w_in': nrm((N_AB, D, AB_COLS), D ** -0.5),
        'ab_shift_mu': unif((N_AB, RWKV_COLS), 0.0, 1.0),
        'ab_w0': unif((N_AB, D_RWKV), -6.0, -1.0),
        'ab_w2': nrm((N_AB, DECAY_LORA, D_RWKV), 0.1),
        'ab_a0': nrm((N_AB, D_RWKV), 0.5),
        'ab_a2': nrm((N_AB, AAA_LORA, D_RWKV), AAA_LORA ** -0.5),
        'ab_g2': nrm((N_AB, GATE_LORA, D_RWKV), GATE_LORA ** -0.5),
        'ab_k_k': 0.85 + nrm((N_AB, D_RWKV), 0.05),
        'ab_k_a': 1.0 + nrm((N_AB, D_RWKV), 0.05),
        'ab_r_k': nrm((N_AB, N_RWKV_HEADS, HEAD_DIM), 0.1),
        'ab_lnx_g': 1.0 + nrm((N_AB, D_RWKV), 0.05),
        'ab_lnx_b': nrm((N_AB, D_RWKV), 0.01),
        'ab_lam_q1': nrm((N_AB, HEAD_DIM), 0.1),
        'ab_lam_k1': nrm((N_AB, HEAD_DIM), 0.1),
        'ab_lam_q2': nrm((N_AB, HEAD_DIM), 0.1),
        'ab_lam_k2': nrm((N_AB, HEAD_DIM), 0.1),
        'ab_subln_g': 1.0 + nrm((N_AB, 2 * HEAD_DIM), 0.05),
        'ab_w_out': nrm((N_AB, D_RWKV + D_DIFF, D), (D_RWKV + D_DIFF) ** -0.5 * DEEPNORM_BETA),
        'c_w_in': nrm((N_C, D, 2 * D_GMLP), D ** -0.5),
        'c_b_in': nrm((N_C, 2 * D_GMLP), 0.01),
        'c_ln_g': 1.0 + nrm((N_C, D_GMLP), 0.05),
        'c_ln_b': nrm((N_C, D_GMLP), 0.01),
        'c_w_s': nrm((N_C, GMLP_GROUPS, CHUNK, CHUNK), CHUNK ** -0.5),
        'c_b_s': 1.0 + nrm((N_C, GMLP_GROUPS, CHUNK), 0.05),
        'c_w_out': nrm((N_C, D_GMLP, D), D_GMLP ** -0.5 * DEEPNORM_BETA),
        'ln1_g': 1.0 + nrm((DEPTH, D), 0.05),
        'ln1_b': nrm((DEPTH, D), 0.01),
        'ffn_w_up': nrm((DEPTH, D, 2 * D_FF), D ** -0.5),
        'ffn_conv_w': nrm((DEPTH, CONV_W, D_FF), CONV_W ** -0.5),
        'ffn_conv_b': nrm((DEPTH, D_FF), 0.01),
        'ffn_w_down': nrm((DEPTH, D_FF, D), D_FF ** -0.5 * DEEPNORM_BETA),
        'ln2_g': 1.0 + nrm((DEPTH, D), 0.05),
        'ln2_b': nrm((DEPTH, D), 0.01),
    }


def reference(x, ab_w_in, ab_shift_mu, ab_w0, ab_w2, ab_a0, ab_a2, ab_g2, ab_k_k, ab_k_a,
              ab_r_k, ab_lnx_g, ab_lnx_b, ab_lam_q1, ab_lam_k1, ab_lam_q2, ab_lam_k2,
              ab_subln_g, ab_w_out, c_w_in, c_b_in, c_ln_g, c_ln_b, c_w_s, c_b_s, c_w_out,
              ln1_g, ln1_b, ffn_w_up, ffn_conv_w, ffn_conv_b, ffn_w_down, ln2_g, ln2_b):
    S = x.shape[1]
    cos, sin = rope_tables(S)
    for i in range(DEPTH):
        j = i // 2
        if i % 2 == 0:
            p = x @ ab_w_in[j]
            y_r = rwkv7_mix(p[..., :RWKV_COLS], ab_shift_mu[j], ab_w0[j], ab_w2[j], ab_a0[j],
                            ab_a2[j], ab_g2[j], ab_k_k[j], ab_k_a[j], ab_r_k[j],
                            ab_lnx_g[j], ab_lnx_b[j])
            lam_init = 0.8 - 0.6 * math.exp(-0.3 * i)
            y_d = diff_attention(p[..., RWKV_COLS:], ab_lam_q1[j], ab_lam_k1[j], ab_lam_q2[j],
                                 ab_lam_k2[j], ab_subln_g[j], lam_init, cos, sin)
            mix = jnp.concatenate([y_r, y_d], axis=-1) @ ab_w_out[j]
        else:
            mix = chunked_spatial_gating(x, c_w_in[j], c_b_in[j], c_ln_g[j], c_ln_b[j],
                                         c_w_s[j], c_b_s[j]) @ c_w_out[j]
        x = layer_norm(DEEPNORM_ALPHA * x + mix, ln1_g[i], ln1_b[i])
        ffn = conv_ffn(x, ffn_w_up[i], ffn_conv_w[i], ffn_conv_b[i], ffn_w_down[i])
        x = layer_norm(DEEPNORM_ALPHA * x + ffn, ln2_g[i], ln2_b[i])
    return x
```

```python
import functools
import math

import jax
import jax.numpy as jnp
from jax import lax
from jax.experimental import pallas as pl
from jax.experimental.pallas import tpu as pltpu

F32 = jnp.float32
BF16 = jnp.bfloat16
HIGHEST = lax.Precision.HIGHEST

D_MODEL = 1024
HEAD_DIM = 64
N_RWKV_HEADS = 8
D_RWKV = N_RWKV_HEADS * HEAD_DIM
N_DIFF_HEADS = 4
D_DIFF = N_DIFF_HEADS * 2 * HEAD_DIM
DECAY_LORA = 64
AAA_LORA = 64
GATE_LORA = 128
RWKV_COLS = 3 * D_RWKV + DECAY_LORA + AAA_LORA + GATE_LORA
AB_COLS = RWKV_COLS + 3 * D_DIFF
RWKV_GN_EPS = 64e-5
ROPE_THETA = 500000.0
ROPE_DIM = HEAD_DIM // 4
CHUNK = 128
GMLP_GROUPS = 8
D_GMLP = D_MODEL
D_FF = 2816
DEPTH = 4
DEEPNORM_ALPHA = (2 * DEPTH) ** 0.25

LANES = 128
HALO_ROWS = 8
WKV_CHUNK = 64
VMEM_LIMIT = 56 * 1024 * 1024


def _cparams(sem):
    return pltpu.CompilerParams(dimension_semantics=sem, vmem_limit_bytes=VMEM_LIMIT)


def _dot(a, b):
    return jnp.dot(a.astype(BF16), b.astype(BF16), preferred_element_type=F32)


def _dot_nt(a, b):
    return lax.dot_general(a.astype(BF16), b.astype(BF16), (((1,), (1,)), ((), ())),
                           preferred_element_type=F32)


def _layer_norm(z, g, b, eps=1e-5):
    mu = jnp.mean(z, axis=-1, keepdims=True)
    d = z - mu
    var = jnp.mean(d * d, axis=-1, keepdims=True)
    return d * lax.rsqrt(var + eps) * g + b


def _shift_rows(t, k, halo):
    rolled = pltpu.roll(t, k, 0)
    row = lax.broadcasted_iota(jnp.int32, t.shape, 0)
    out = rolled
    for j in range(k):
        out = jnp.where(row == j, halo[HALO_ROWS - k + j:HALO_ROWS - k + j + 1, :], out)
    return out


def _ab_in_kernel(x_ref, xh_ref, w_ref, mu_ref, w0_ref, w2_ref, a0_ref, a2_ref, g2_ref, kk_ref,
                  ka_ref, rc_ref, rs1_ref, rs2_ref,
                  r_out, k_out, v_out, lw_out, kn_out, a_out, g_out, q_out, kd_out, vd_out,
                  *, blocks_per_seq):
    i = pl.program_id(0)
    xb = x_ref[...].astype(BF16)
    p = jnp.dot(xb, w_ref[...], preferred_element_type=F32)
    ph = jnp.dot(xh_ref[...].astype(BF16), w_ref[:, :RWKV_COLS], preferred_element_type=F32)
    ph = jnp.where(i % blocks_per_seq == 0, 0.0, ph)
    pr = p[:, :RWKV_COLS]
    xs = pr + mu_ref[...] * (_shift_rows(pr, 1, ph) - pr)

    r = xs[:, :D_RWKV]
    k = xs[:, D_RWKV:2 * D_RWKV]
    v = xs[:, 2 * D_RWKV:3 * D_RWKV]
    o = 3 * D_RWKV
    xw = xs[:, o:o + DECAY_LORA]
    xa = xs[:, o + DECAY_LORA:o + DECAY_LORA + AAA_LORA]
    xg = xs[:, o + DECAY_LORA + AAA_LORA:]

    z = w0_ref[...] + jnp.dot(jnp.tanh(xw), w2_ref[...], precision=HIGHEST,
                              preferred_element_type=F32)
    softplus_neg = jnp.maximum(-z, 0.0) + jnp.log1p(jnp.exp(-jnp.abs(z)))
    w = -softplus_neg - 0.5
    lw_out[...] = -jnp.exp(w)
    a = jax.nn.sigmoid(a0_ref[...] + _dot(xa, a2_ref[...]))
    a_out[...] = a
    g_out[...] = _dot(jax.nn.sigmoid(xg), g2_ref[...])

    kx = k * kk_ref[...]
    sq = kx * kx
    lane = lax.broadcasted_iota(jnp.int32, (sq.shape[0], LANES), 1)
    lo = lane < HEAD_DIM
    for c in range(D_RWKV // LANES):
        blk = sq[:, c * LANES:(c + 1) * LANES]
        n_lo = jnp.sqrt(jnp.sum(jnp.where(lo, blk, 0.0), axis=-1, keepdims=True))
        n_hi = jnp.sqrt(jnp.sum(jnp.where(lo, 0.0, blk), axis=-1, keepdims=True))
        norm = jnp.maximum(jnp.where(lo, n_lo, n_hi), 1e-12)
        kn_out[:, c * LANES:(c + 1) * LANES] = kx[:, c * LANES:(c + 1) * LANES] / norm
    r_out[...] = r
    k_out[...] = k * (1.0 + (a - 1.0) * ka_ref[...])
    v_out[...] = v

    pd = p[:, RWKV_COLS:]
    rc = jnp.concatenate([rc_ref[...]] * (D_DIFF // LANES), axis=1)
    rs1 = jnp.concatenate([rs1_ref[...]] * (D_DIFF // LANES), axis=1)
    rs2 = jnp.concatenate([rs2_ref[...]] * (D_DIFF // LANES), axis=1)
    half = ROPE_DIM // 2

    def rope(t):
        return t * rc + pltpu.roll(t, half, 1) * rs1 + pltpu.roll(t, D_DIFF - half, 1) * rs2

    q_out[...] = (rope(pd[:, :D_DIFF]) * (HEAD_DIM ** -0.5)).astype(BF16)
    kd_out[...] = rope(pd[:, D_DIFF:2 * D_DIFF]).astype(BF16)
    vd_out[...] = pd[:, 2 * D_DIFF:].astype(BF16)


def _ab_in(x2, w_in, mu, w0, w2, a0, a2, g2, k_k, k_a, rc, rs1, rs2, *, seq, tm):
    T = x2.shape[0]
    n = T // tm
    bps = seq // tm
    row = lambda i: (i, 0)
    full = lambda i: (0, 0)
    halo = lambda i: (jnp.maximum(i * (tm // HALO_ROWS) - 1, 0), 0)
    rope_map = lambda i: (i % bps, 0)
    vec = lambda a: a.reshape(1, -1)
    f32_out = jax.ShapeDtypeStruct((T, D_RWKV), F32)
    bf_out = jax.ShapeDtypeStruct((T, D_DIFF), BF16)
    out_spec = pl.BlockSpec((tm, D_RWKV), row)
    return pl.pallas_call(
        functools.partial(_ab_in_kernel, blocks_per_seq=bps),
        grid=(n,),
        in_specs=[
            pl.BlockSpec((tm, D_MODEL), row),
            pl.BlockSpec((HALO_ROWS, D_MODEL), halo),
            pl.BlockSpec((D_MODEL, AB_COLS), full),
            pl.BlockSpec((1, RWKV_COLS), full),
            pl.BlockSpec((1, D_RWKV), full),
            pl.BlockSpec((DECAY_LORA, D_RWKV), full),
            pl.BlockSpec((1, D_RWKV), full),
            pl.BlockSpec((AAA_LORA, D_RWKV), full),
            pl.BlockSpec((GATE_LORA, D_RWKV), full),
            pl.BlockSpec((1, D_RWKV), full),
            pl.BlockSpec((1, D_RWKV), full),
            pl.BlockSpec((tm, LANES), rope_map),
            pl.BlockSpec((tm, LANES), rope_map),
            pl.BlockSpec((tm, LANES), rope_map),
        ],
        out_specs=[out_spec] * 10,
        out_shape=[f32_out] * 7 + [bf_out] * 3,
        compiler_params=_cparams(("parallel",)),
        name="ab_in",
    )(x2, x2, w_in, vec(mu), vec(w0), w2, vec(a0), a2, g2, vec(k_k), vec(k_a), rc, rs1, rs2)


def _unit_lower_inverse(n_mat, row, col):
    c = n_mat.shape[0]
    eye = jnp.where(row == col, 1.0, 0.0)
    same16 = (row // 16) == (col // 16)
    nd = jnp.where(same16, n_mat, 0.0)
    t = eye + nd
    pw = nd
    for _ in range(3):
        pw = _dot(pw, pw)
        t = t + _dot(t, pw)
    blk = 16
    while blk < c:
        same_lo = (row // blk) == (col // blk)
        same_hi = (row // (2 * blk)) == (col // (2 * blk))
        off = jnp.where(jnp.logical_and(same_hi, jnp.logical_not(same_lo)), n_mat, 0.0)
        t = t + _dot(_dot(t, off), t)
        blk *= 2
    return t


def _wkv_chunk(r, k, v, lw, kn, a, states):
    c = r.shape[0]
    row = lax.broadcasted_iota(jnp.int32, (c, c), 0)
    col = lax.broadcasted_iota(jnp.int32, (c, c), 1)
    incl = row >= col
    strict = row > col
    cum = jnp.dot(jnp.where(incl, 1.0, 0.0), lw, precision=HIGHEST, preferred_element_type=F32)
    mid = cum[c // 2 - 1:c // 2, :]
    end = cum[c - 1:c, :]
    e_abs = jnp.exp(cum)
    e_abs_prev = jnp.exp(cum - lw)
    e_mid = jnp.exp(-mid)
    e_neg = jnp.exp(mid - cum)
    e_end = jnp.exp(end - cum)
    g_end = jnp.exp(end)
    bvec = kn * a
    r0 = r * e_abs
    a0 = -kn * e_abs_prev
    rt = r0 * e_mid
    at = a0 * e_mid
    kt = k * e_neg
    bt = bvec * e_neg
    kh = k * e_end
    bh = bvec * e_end
    ys, new_states = [], []
    for h in range(N_RWKV_HEADS):
        sl = slice(h * HEAD_DIM, (h + 1) * HEAD_DIM)
        s0 = states[h]
        vh = v[:, sl]
        lhs = jnp.concatenate([at[:, sl], rt[:, sl]], axis=0)
        rhs = jnp.concatenate([bt[:, sl], kt[:, sl]], axis=0)
        am = _dot_nt(lhs, rhs)
        a_ab = jnp.where(strict, am[:c, :c], 0.0)
        a_ak = jnp.where(strict, am[:c, c:], 0.0)
        a_rb = jnp.where(incl, am[c:, :c], 0.0)
        a_rk = jnp.where(incl, am[c:, c:], 0.0)
        t = _unit_lower_inverse(a_ab, row, col)
        x = _dot(t, jnp.concatenate([a0[:, sl], _dot(a_ak, vh)], axis=1))
        u = _dot_nt(x[:, :HEAD_DIM], s0) + x[:, HEAD_DIM:]
        y = _dot_nt(r0[:, sl], s0) + _dot(a_rb, u) + _dot(a_rk, vh)
        uv = jnp.concatenate([u, vh], axis=0)
        bk = jnp.concatenate([bh[:, sl], kh[:, sl]], axis=0)
        upd = lax.dot_general(uv.astype(BF16), bk.astype(BF16), (((0,), (0,)), ((), ())),
                              preferred_element_type=F32)
        new_states.append(s0 * g_end[:, sl] + upd)
        ys.append(y)
    return ys, new_states


def _wkv_kernel(r_ref, k_ref, v_ref, lw_ref, kn_ref, a_ref, g_ref, rk_ref, lg_ref, lb_ref,
                o_ref, s_scr, *, n_chunks):
    @pl.when(pl.program_id(1) == 0)
    def _():
        s_scr[...] = jnp.zeros_like(s_scr)

    states = [s_scr[h] for h in range(N_RWKV_HEADS)]
    for ci in range(n_chunks):
        rows = slice(ci * WKV_CHUNK, (ci + 1) * WKV_CHUNK)
        r = r_ref[rows, :]
        k = k_ref[rows, :]
        v = v_ref[rows, :]
        ys, states = _wkv_chunk(r, k, v, lw_ref[rows, :], kn_ref[rows, :], a_ref[rows, :], states)
        rkr = r * k * rk_ref[...]
        for h in range(N_RWKV_HEADS):
            sl = slice(h * HEAD_DIM, (h + 1) * HEAD_DIM)
            y = ys[h]
            mu = jnp.mean(y, axis=-1, keepdims=True)
            d = y - mu
            var = jnp.mean(d * d, axis=-1, keepdims=True)
            yn = d * lax.rsqrt(var + RWKV_GN_EPS) * lg_ref[:, sl] + lb_ref[:, sl]
            bonus = jnp.sum(rkr[:, sl], axis=-1, keepdims=True) * v[:, sl]
            o_ref[rows, sl] = ((yn + bonus) * g_ref[rows, sl]).astype(BF16)
    for h in range(N_RWKV_HEADS):
        s_scr[h] = states[h]


def _wkv(r, k, v, lw, kn, a, g, r_k, lnx_g, lnx_b, *, batch, seq, rows):
    T = r.shape[0]
    nb = seq // rows
    blk = pl.BlockSpec((rows, D_RWKV), lambda b, i: (b * nb + i, 0))
    par = pl.BlockSpec((1, D_RWKV), lambda b, i: (0, 0))
    return pl.pallas_call(
        functools.partial(_wkv_kernel, n_chunks=rows // WKV_CHUNK),
        grid=(batch, nb),
        in_specs=[blk] * 7 + [par] * 3,
        out_specs=blk,
        out_shape=jax.ShapeDtypeStruct((T, D_RWKV), BF16),
        scratch_shapes=[pltpu.VMEM((N_RWKV_HEADS, HEAD_DIM, HEAD_DIM), F32)],
        compiler_params=_cparams(("parallel", "arbitrary")),
        name="wkv7",
    )(r, k, v, lw, kn, a, g, r_k.reshape(1, -1), lnx_g.reshape(1, -1), lnx_b.reshape(1, -1))


def _dattn_kernel(q_ref, k_ref, v_ref, lq1_ref, lk1_ref, lq2_ref, lk2_ref, sg_ref, o_ref,
                  qs_scr, m_scr, l_scr, acc_scr, *, tq, lam_init):
    qi = pl.program_id(2)
    q = q_ref[...]
    lane = lax.broadcasted_iota(jnp.int32, q.shape, 1)
    zero = jnp.zeros_like(q)
    qs_scr[:tq, :] = jnp.where(lane < HEAD_DIM, q, zero)
    qs_scr[tq:, :] = jnp.where(lane < HEAD_DIM, zero, q)
    m_scr[...] = jnp.full_like(m_scr, -jnp.inf)
    l_scr[...] = jnp.zeros_like(l_scr)
    acc_scr[...] = jnp.zeros_like(acc_scr)

    def step(j, masked):
        start = pl.multiple_of(j * tq, tq)
        s = _dot_nt(qs_scr[...], k_ref[pl.ds(start, tq), :])
        if masked:
            r_pos = lax.broadcasted_iota(jnp.int32, s.shape, 0) % tq
            c_pos = lax.broadcasted_iota(jnp.int32, s.shape, 1)
            s = jnp.where(c_pos <= r_pos, s, -jnp.inf)
        m_prev = m_scr[...]
        m_new = jnp.maximum(m_prev, jnp.max(s, axis=-1, keepdims=True))
        alpha = jnp.exp(m_prev - m_new)
        p = jnp.exp(s - m_new)
        l_scr[...] = alpha * l_scr[...] + jnp.sum(p, axis=-1, keepdims=True)
        acc_scr[...] = alpha * acc_scr[...] + jnp.dot(
            p.astype(BF16), v_ref[pl.ds(start, tq), :], preferred_element_type=F32)
        m_scr[...] = m_new

    def body(j, carry):
        step(j, False)
        return carry

    lax.fori_loop(0, qi, body, 0)
    step(qi, True)

    lam = (jnp.exp(jnp.sum(lq1_ref[...] * lk1_ref[...], axis=-1, keepdims=True))
           - jnp.exp(jnp.sum(lq2_ref[...] * lk2_ref[...], axis=-1, keepdims=True)) + lam_init)
    on = acc_scr[...] / l_scr[...]
    o = on[:tq, :] - lam * on[tq:, :]
    o = o * lax.rsqrt(jnp.mean(o * o, axis=-1, keepdims=True) + 1e-5) * sg_ref[...]
    o_ref[...] = (o * (1.0 - lam_init)).astype(BF16)


def _dattn(q, k, v, lq1, lk1, lq2, lk2, subln_g, *, batch, seq, tq, lam_init):
    T = q.shape[0]
    nq = seq // tq
    hd = 2 * HEAD_DIM
    qmap = lambda b, h, i: (b * nq + i, h)
    kvmap = lambda b, h, i: (b, h)
    par = lambda b, h, i: (0, 0)
    vec = lambda a: a.reshape(1, -1)
    return pl.pallas_call(
        functools.partial(_dattn_kernel, tq=tq, lam_init=lam_init),
        grid=(batch, N_DIFF_HEADS, nq),
        in_specs=[
            pl.BlockSpec((tq, hd), qmap),
            pl.BlockSpec((seq, hd), kvmap),
            pl.BlockSpec((seq, hd), kvmap),
            pl.BlockSpec((1, HEAD_DIM), par),
            pl.BlockSpec((1, HEAD_DIM), par),
            pl.BlockSpec((1, HEAD_DIM), par),
            pl.BlockSpec((1, HEAD_DIM), par),
            pl.BlockSpec((1, hd), par),
        ],
        out_specs=pl.BlockSpec((tq, hd), qmap),
        out_shape=jax.ShapeDtypeStruct((T, D_DIFF), BF16),
        scratch_shapes=[
            pltpu.VMEM((2 * tq, hd), BF16),
            pltpu.VMEM((2 * tq, 1), F32),
            pltpu.VMEM((2 * tq, 1), F32),
            pltpu.VMEM((2 * tq, hd), F32),
        ],
        compiler_params=_cparams(("parallel", "parallel", "arbitrary")),
        name="diff_attn",
    )(q, k, v, vec(lq1), vec(lk1), vec(lq2), vec(lk2), vec(subln_g))


def _proj_ln_kernel(yr_ref, yd_ref, w_ref, x_ref, g_ref, b_ref, o_ref):
    mix = jnp.dot(yr_ref[...], w_ref[:D_RWKV, :], preferred_element_type=F32)
    mix = mix + jnp.dot(yd_ref[...], w_ref[D_RWKV:, :], preferred_element_type=F32)
    o_ref[...] = _layer_norm(DEEPNORM_ALPHA * x_ref[...] + mix, g_ref[...], b_ref[...])


def _proj_ln(yr, yd, w_out, x2, g, b, *, tm):
    T = x2.shape[0]
    row = lambda i: (i, 0)
    full = lambda i: (0, 0)
    return pl.pallas_call(
        _proj_ln_kernel,
        grid=(T // tm,),
        in_specs=[
            pl.BlockSpec((tm, D_RWKV), row),
            pl.BlockSpec((tm, D_DIFF), row),
            pl.BlockSpec((D_RWKV + D_DIFF, D_MODEL), full),
            pl.BlockSpec((tm, D_MODEL), row),
            pl.BlockSpec((1, D_MODEL), full),
            pl.BlockSpec((1, D_MODEL), full),
        ],
        out_specs=pl.BlockSpec((tm, D_MODEL), row),
        out_shape=jax.ShapeDtypeStruct((T, D_MODEL), F32),
        compiler_params=_cparams(("parallel",)),
        name="ab_out_ln",
    )(yr, yd, w_out, x2, g.reshape(1, -1), b.reshape(1, -1))


def _gmlp_kernel(x_ref, win_ref, bin_ref, lng_ref, lnb_ref, ws_ref, bs_ref, wout_ref, g_ref, b_ref,
                 o_ref, gated_scr, *, tm):
    x = x_ref[...]
    h = jnp.dot(x.astype(BF16), win_ref[...], preferred_element_type=F32) + bin_ref[...]
    h = 0.5 * h * (1.0 + lax.erf(h * (0.5 ** 0.5)))
    u = h[:, :D_GMLP]
    v = _layer_norm(h[:, D_GMLP:], lng_ref[...], lnb_ref[...]).astype(BF16)
    row = lax.broadcasted_iota(jnp.int32, (CHUNK, CHUNK), 0)
    col = lax.broadcasted_iota(jnp.int32, (CHUNK, CHUNK), 1)
    gw = D_GMLP // GMLP_GROUPS
    for gi in range(GMLP_GROUPS):
        ws = jnp.where(row >= col, ws_ref[gi], 0.0).astype(BF16)
        for c in range(tm // CHUNK):
            rs = slice(c * CHUNK, (c + 1) * CHUNK)
            cs = slice(gi * gw, (gi + 1) * gw)
            mixed = jnp.dot(ws, v[rs, cs], preferred_element_type=F32) + bs_ref[gi]
            gated_scr[rs, cs] = (u[rs, cs] * mixed).astype(BF16)
    mix = jnp.dot(gated_scr[...], wout_ref[...], preferred_element_type=F32)
    o_ref[...] = _layer_norm(DEEPNORM_ALPHA * x + mix, g_ref[...], b_ref[...])


def _gmlp(x2, w_in, b_in, ln_g, ln_b, w_s, b_s, w_out, g, b, *, tm):
    T = x2.shape[0]
    row = lambda i: (i, 0)
    full = lambda i: (0, 0)
    full3 = lambda i: (0, 0, 0)
    vec = lambda a: a.reshape(1, -1)
    gw = D_GMLP // GMLP_GROUPS
    bs_b = jnp.broadcast_to(b_s[:, :, None], (GMLP_GROUPS, CHUNK, gw))
    return pl.pallas_call(
        functools.partial(_gmlp_kernel, tm=tm),
        grid=(T // tm,),
        in_specs=[
            pl.BlockSpec((tm, D_MODEL), row),
            pl.BlockSpec((D_MODEL, 2 * D_GMLP), full),
            pl.BlockSpec((1, 2 * D_GMLP), full),
            pl.BlockSpec((1, D_GMLP), full),
            pl.BlockSpec((1, D_GMLP), full),
            pl.BlockSpec((GMLP_GROUPS, CHUNK, CHUNK), full3),
            pl.BlockSpec((GMLP_GROUPS, CHUNK, gw), full3),
            pl.BlockSpec((D_GMLP, D_MODEL), full),
            pl.BlockSpec((1, D_MODEL), full),
            pl.BlockSpec((1, D_MODEL), full),
        ],
        out_specs=pl.BlockSpec((tm, D_MODEL), row),
        out_shape=jax.ShapeDtypeStruct((T, D_MODEL), F32),
        scratch_shapes=[pltpu.VMEM((tm, D_GMLP), BF16)],
        compiler_params=_cparams(("parallel",)),
        name="gmlp",
    )(x2, w_in, vec(b_in), vec(ln_g), vec(ln_b), w_s, bs_b, w_out, vec(g), vec(b))


def _ffn_kernel(x_ref, xh_ref, wup_ref, cw_ref, cb_ref, wd_ref, g_ref, b_ref, o_ref,
                xb_scr, acc_scr, *, blocks_per_seq, tf, n_slabs):
    i = pl.program_id(0)
    j = pl.program_id(1)

    @pl.when(j == 0)
    def _():
        xb_scr[...] = x_ref[...].astype(BF16)
        acc_scr[...] = jnp.zeros_like(acc_scr)

    wup = wup_ref[0]
    gv = jnp.dot(xb_scr[...], wup, preferred_element_type=F32)
    gate = gv[:, :tf]
    val = gv[:, tf:]
    gh = jnp.dot(xh_ref[...].astype(BF16), wup[:, :tf], preferred_element_type=F32)
    gh = jnp.where(i % blocks_per_seq == 0, 0.0, gh)
    cw = cw_ref[0]
    conv = (cb_ref[0] + cw[0:1, :] * _shift_rows(gate, 2, gh) + cw[1:2, :] * _shift_rows(gate, 1, gh)
            + cw[2:3, :] * gate)
    hid = conv * jax.nn.sigmoid(conv) * val
    acc_scr[...] += jnp.dot(hid.astype(BF16), wd_ref[0], preferred_element_type=F32)

    @pl.when(j == n_slabs - 1)
    def _():
        o_ref[...] = _layer_norm(DEEPNORM_ALPHA * x_ref[...] + acc_scr[...], g_ref[...], b_ref[...])


def _ffn(x2, w_up, conv_w, conv_b, w_down, g, b, *, seq, tm, tf):
    T = x2.shape[0]
    n_slabs = D_FF // tf
    bps = seq // tm
    wg = w_up[:, :D_FF].reshape(D_MODEL, n_slabs, tf)
    wv = w_up[:, D_FF:].reshape(D_MODEL, n_slabs, tf)
    wup = jnp.transpose(jnp.concatenate([wg, wv], axis=2), (1, 0, 2)).astype(BF16)
    cw = jnp.transpose(conv_w.reshape(3, n_slabs, tf), (1, 0, 2))
    cb = conv_b.reshape(n_slabs, 1, tf)
    wd = w_down.reshape(n_slabs, tf, D_MODEL).astype(BF16)
    row = lambda i, j: (i, 0)
    full = lambda i, j: (0, 0)
    slab = lambda i, j: (j, 0, 0)
    halo = lambda i, j: (jnp.maximum(i * (tm // HALO_ROWS) - 1, 0), 0)
    return pl.pallas_call(
        functools.partial(_ffn_kernel, blocks_per_seq=bps, tf=tf, n_slabs=n_slabs),
        grid=(T // tm, n_slabs),
        in_specs=[
            pl.BlockSpec((tm, D_MODEL), row),
            pl.BlockSpec((HALO_ROWS, D_MODEL), halo),
            pl.BlockSpec((1, D_MODEL, 2 * tf), slab),
            pl.BlockSpec((1, 3, tf), slab),
            pl.BlockSpec((1, 1, tf), slab),
            pl.BlockSpec((1, tf, D_MODEL), slab),
            pl.BlockSpec((1, D_MODEL), full),
            pl.BlockSpec((1, D_MODEL), full),
        ],
        out_specs=pl.BlockSpec((tm, D_MODEL), row),
        out_shape=jax.ShapeDtypeStruct((T, D_MODEL), F32),
        scratch_shapes=[pltpu.VMEM((tm, D_MODEL), BF16), pltpu.VMEM((tm, D_MODEL), F32)],
        compiler_params=_cparams(("parallel", "arbitrary")),
        name="conv_ffn",
    )(x2, x2, wup, cw, cb, wd, g.reshape(1, -1), b.reshape(1, -1))


def _rope_tables(seq):
    half = ROPE_DIM // 2
    inv_freq = ROPE_THETA ** (-jnp.arange(0, ROPE_DIM, 2, dtype=F32) / ROPE_DIM)
    ang = jnp.arange(seq, dtype=F32)[:, None] * inv_freq[None, :]
    cos, sin = jnp.cos(ang), jnp.sin(ang)
    ones = jnp.ones((seq, HEAD_DIM - ROPE_DIM), F32)
    zeros = jnp.zeros((seq, HEAD_DIM - ROPE_DIM), F32)
    zh = jnp.zeros((seq, half), F32)
    c64 = jnp.concatenate([cos, cos, ones], axis=1)
    s1_64 = jnp.concatenate([zh, sin, zeros], axis=1)
    s2_64 = jnp.concatenate([-sin, zh, zeros], axis=1)
    two = lambda t: jnp.concatenate([t, t], axis=1)
    return two(c64), two(s1_64), two(s2_64)


def kernel(x, ab_w_in, ab_shift_mu, ab_w0, ab_w2, ab_a0, ab_a2, ab_g2, ab_k_k, ab_k_a, ab_r_k, ab_lnx_g, ab_lnx_b, ab_lam_q1, ab_lam_k1, ab_lam_q2, ab_lam_k2, ab_subln_g, ab_w_out, c_w_in, c_b_in, c_ln_g, c_ln_b, c_w_s, c_b_s, c_w_out, ln1_g, ln1_b, ffn_w_up, ffn_conv_w, ffn_conv_b, ffn_w_down, ln2_g, ln2_b):
    batch, seq, _ = x.shape
    x2 = x.reshape(batch * seq, D_MODEL)
    rc, rs1, rs2 = _rope_tables(seq)
    tm_in = min(512, seq)
    tm_ffn = min(1024, seq)
    tm_gmlp = min(512, seq)
    tq = min(512, seq)
    wkv_rows = min(256, seq)
    for i in range(DEPTH):
        j = i // 2
        if i % 2 == 0:
            (r, k, v, lw, kn, a, g, qd, kd, vd) = _ab_in(
                x2, ab_w_in[j].astype(BF16), ab_shift_mu[j], ab_w0[j], ab_w2[j], ab_a0[j],
                ab_a2[j].astype(BF16), ab_g2[j].astype(BF16), ab_k_k[j], ab_k_a[j], rc, rs1, rs2,
                seq=seq, tm=tm_in)
            y_r = _wkv(r, k, v, lw, kn, a, g, ab_r_k[j], ab_lnx_g[j], ab_lnx_b[j],
                       batch=batch, seq=seq, rows=wkv_rows)
            lam_init = 0.8 - 0.6 * math.exp(-0.3 * i)
            y_d = _dattn(qd, kd, vd, ab_lam_q1[j], ab_lam_k1[j], ab_lam_q2[j], ab_lam_k2[j],
                         ab_subln_g[j], batch=batch, seq=seq, tq=tq, lam_init=lam_init)
            x2 = _proj_ln(y_r, y_d, ab_w_out[j].astype(BF16), x2, ln1_g[i], ln1_b[i], tm=tm_in)
        else:
            x2 = _gmlp(x2, c_w_in[j].astype(BF16), c_b_in[j], c_ln_g[j], c_ln_b[j], c_w_s[j],
                       c_b_s[j], c_w_out[j].astype(BF16), ln1_g[i], ln1_b[i], tm=tm_gmlp)
        x2 = _ffn(x2, ffn_w_up[i], ffn_conv_w[i], ffn_conv_b[i], ffn_w_down[i], ln2_g[i], ln2_b[i],
                  seq=seq, tm=tm_ffn, tf=256)
    return x2.reshape(batch, seq, D_MODEL)
```

```python
import functools
import math

import jax
import jax.numpy as jnp
from jax import lax
from jax.experimental import pallas as pl
from jax.experimental.pallas import tpu as pltpu

F32 = jnp.float32
BF16 = jnp.bfloat16
HIGHEST = lax.Precision.HIGHEST

D_MODEL = 1024
HEAD_DIM = 64
N_RWKV_HEADS = 8
D_RWKV = N_RWKV_HEADS * HEAD_DIM
N_DIFF_HEADS = 4
D_DIFF = N_DIFF_HEADS * 2 * HEAD_DIM
DECAY_LORA = 64
AAA_LORA = 64
GATE_LORA = 128
RWKV_COLS = 3 * D_RWKV + DECAY_LORA + AAA_LORA + GATE_LORA
AB_COLS = RWKV_COLS + 3 * D_DIFF
RWKV_GN_EPS = 64e-5
ROPE_THETA = 500000.0
ROPE_DIM = HEAD_DIM // 4
CHUNK = 128
GMLP_GROUPS = 8
D_GMLP = D_MODEL
D_FF = 2816
DEPTH = 4
DEEPNORM_ALPHA = (2 * DEPTH) ** 0.25

LANES = 128
HALO_ROWS = 8
WKV_CHUNK = 64
VMEM_LIMIT = 56 * 1024 * 1024
LOG2E = math.log2(math.e)
ATTN_ROW_TILE = 1024


def _cparams(sem):
    return pltpu.CompilerParams(dimension_semantics=sem, vmem_limit_bytes=VMEM_LIMIT)


def _dot(a, b):
    return jnp.dot(a.astype(BF16), b.astype(BF16), preferred_element_type=F32)


def _dot_nt(a, b):
    return lax.dot_general(a.astype(BF16), b.astype(BF16), (((1,), (1,)), ((), ())),
                           preferred_element_type=F32)


def _layer_norm(z, g, b, eps=1e-5):
    mu = jnp.mean(z, axis=-1, keepdims=True)
    d = z - mu
    var = jnp.mean(d * d, axis=-1, keepdims=True)
    return d * lax.rsqrt(var + eps) * g + b


def _shift_rows(t, k, halo):
    rolled = pltpu.roll(t, k, 0)
    row = lax.broadcasted_iota(jnp.int32, t.shape, 0)
    out = rolled
    for j in range(k):
        out = jnp.where(row == j, halo[HALO_ROWS - k + j:HALO_ROWS - k + j + 1, :], out)
    return out


def _ab_in_kernel(x_ref, xh_ref, w_ref, mu_ref, w0_ref, w2_ref, a0_ref, a2_ref, g2_ref, kk_ref,
                  ka_ref, rc_ref, rs1_ref, rs2_ref,
                  r_out, k_out, v_out, lw_out, kn_out, a_out, g_out, q_out, kd_out, vd_out,
                  *, blocks_per_seq):
    i = pl.program_id(0)
    xb = x_ref[...].astype(BF16)
    p = jnp.dot(xb, w_ref[...], preferred_element_type=F32)
    ph = jnp.dot(xh_ref[...].astype(BF16), w_ref[:, :RWKV_COLS], preferred_element_type=F32)
    ph = jnp.where(i % blocks_per_seq == 0, 0.0, ph)
    pr = p[:, :RWKV_COLS]
    xs = pr + mu_ref[...] * (_shift_rows(pr, 1, ph) - pr)

    r = xs[:, :D_RWKV]
    k = xs[:, D_RWKV:2 * D_RWKV]
    v = xs[:, 2 * D_RWKV:3 * D_RWKV]
    o = 3 * D_RWKV
    xw = xs[:, o:o + DECAY_LORA]
    xa = xs[:, o + DECAY_LORA:o + DECAY_LORA + AAA_LORA]
    xg = xs[:, o + DECAY_LORA + AAA_LORA:]

    z = w0_ref[...] + jnp.dot(jnp.tanh(xw), w2_ref[...], precision=HIGHEST,
                              preferred_element_type=F32)
    softplus_neg = jnp.maximum(-z, 0.0) + jnp.log1p(jnp.exp(-jnp.abs(z)))
    w = -softplus_neg - 0.5
    lw_out[...] = -jnp.exp(w)
    a = jax.nn.sigmoid(a0_ref[...] + _dot(xa, a2_ref[...]))
    a_out[...] = a
    g_out[...] = _dot(jax.nn.sigmoid(xg), g2_ref[...])

    kx = k * kk_ref[...]
    sq = kx * kx
    lane = lax.broadcasted_iota(jnp.int32, (sq.shape[0], LANES), 1)
    lo = lane < HEAD_DIM
    for c in range(D_RWKV // LANES):
        blk = sq[:, c * LANES:(c + 1) * LANES]
        n_lo = jnp.sqrt(jnp.sum(jnp.where(lo, blk, 0.0), axis=-1, keepdims=True))
        n_hi = jnp.sqrt(jnp.sum(jnp.where(lo, 0.0, blk), axis=-1, keepdims=True))
        norm = jnp.maximum(jnp.where(lo, n_lo, n_hi), 1e-12)
        kn_out[:, c * LANES:(c + 1) * LANES] = kx[:, c * LANES:(c + 1) * LANES] / norm
    r_out[...] = r
    k_out[...] = k * (1.0 + (a - 1.0) * ka_ref[...])
    v_out[...] = v

    pd = p[:, RWKV_COLS:]
    rc = jnp.concatenate([rc_ref[...]] * (D_DIFF // LANES), axis=1)
    rs1 = jnp.concatenate([rs1_ref[...]] * (D_DIFF // LANES), axis=1)
    rs2 = jnp.concatenate([rs2_ref[...]] * (D_DIFF // LANES), axis=1)
    half = ROPE_DIM // 2

    def rope(t):
        return t * rc + pltpu.roll(t, half, 1) * rs1 + pltpu.roll(t, D_DIFF - half, 1) * rs2

    q_out[...] = (rope(pd[:, :D_DIFF]) * (HEAD_DIM ** -0.5 * LOG2E)).astype(BF16)
    kd_out[...] = rope(pd[:, D_DIFF:2 * D_DIFF]).astype(BF16)
    vd_out[...] = pd[:, 2 * D_DIFF:].astype(BF16)


def _ab_in(x2, w_in, mu, w0, w2, a0, a2, g2, k_k, k_a, rc, rs1, rs2, *, seq, tm):
    T = x2.shape[0]
    n = T // tm
    bps = seq // tm
    row = lambda i: (i, 0)
    full = lambda i: (0, 0)
    halo = lambda i: (jnp.maximum(i * (tm // HALO_ROWS) - 1, 0), 0)
    rope_map = lambda i: (i % bps, 0)
    vec = lambda a: a.reshape(1, -1)
    f32_out = jax.ShapeDtypeStruct((T, D_RWKV), F32)
    bf_out = jax.ShapeDtypeStruct((T, D_DIFF), BF16)
    out_spec = pl.BlockSpec((tm, D_RWKV), row)
    return pl.pallas_call(
        functools.partial(_ab_in_kernel, blocks_per_seq=bps),
        grid=(n,),
        in_specs=[
            pl.BlockSpec((tm, D_MODEL), row),
            pl.BlockSpec((HALO_ROWS, D_MODEL), halo),
            pl.BlockSpec((D_MODEL, AB_COLS), full),
            pl.BlockSpec((1, RWKV_COLS), full),
            pl.BlockSpec((1, D_RWKV), full),
            pl.BlockSpec((DECAY_LORA, D_RWKV), full),
            pl.BlockSpec((1, D_RWKV), full),
            pl.BlockSpec((AAA_LORA, D_RWKV), full),
            pl.BlockSpec((GATE_LORA, D_RWKV), full),
            pl.BlockSpec((1, D_RWKV), full),
            pl.BlockSpec((1, D_RWKV), full),
            pl.BlockSpec((tm, LANES), rope_map),
            pl.BlockSpec((tm, LANES), rope_map),
            pl.BlockSpec((tm, LANES), rope_map),
        ],
        out_specs=[out_spec] * 10,
        out_shape=[f32_out] * 7 + [bf_out] * 3,
        compiler_params=_cparams(("parallel",)),
        name="ab_in",
    )(x2, x2, w_in, vec(mu), vec(w0), w2, vec(a0), a2, g2, vec(k_k), vec(k_a), rc, rs1, rs2)


WKV_GROUP = 4
GW = WKV_GROUP * HEAD_DIM


def _wkv_masks():
    c = WKV_CHUNK
    row = lax.broadcasted_iota(jnp.int32, (c, GW), 0)
    col = lax.broadcasted_iota(jnp.int32, (c, GW), 1) % HEAD_DIM
    brow = lax.broadcasted_iota(jnp.int32, (GW, GW), 0) // HEAD_DIM
    bcol = lax.broadcasted_iota(jnp.int32, (GW, GW), 1) // HEAD_DIM
    tri = jnp.where(lax.broadcasted_iota(jnp.int32, (c, c), 0)
                    >= lax.broadcasted_iota(jnp.int32, (c, c), 1), 1.0, 0.0)
    bdf = jnp.where(brow == bcol, 1.0, 0.0)
    return dict(row=row, col=col, tri=tri, bdf=bdf, bd=bdf.astype(BF16))


def _bd(w, m):
    return jnp.concatenate([w.astype(BF16)] * WKV_GROUP, axis=0) * m["bd"]


def _mm(x, w_bd):
    return jnp.dot(x.astype(BF16), w_bd, preferred_element_type=F32)


def _mm_nt(x, w_bd):
    return lax.dot_general(x.astype(BF16), w_bd, (((1,), (1,)), ((), ())),
                           preferred_element_type=F32)


def _seg_sum(x, m):
    hi = x.astype(BF16)
    lo = (x - hi.astype(F32)).astype(BF16)
    s = jnp.dot(jnp.concatenate([hi, lo], axis=0), m["bd"], preferred_element_type=F32)
    return s[:x.shape[0]] + s[x.shape[0]:]


def _unit_lower_inverse(n_mats, m):
    row, col = m["row"], m["col"]
    eye = jnp.where(row == col, 1.0, 0.0)
    same16 = (row // 16) == (col // 16)
    pws = [jnp.where(same16, n, 0.0) for n in n_mats]
    ts = [eye + p for p in pws]
    pw_bds = [_bd(p, m) for p in pws]
    for _ in range(3):
        pws = [_mm(p, b) for p, b in zip(pws, pw_bds)]
        pw_bds = [_bd(p, m) for p in pws]
        ts = [t + _mm(t, b) for t, b in zip(ts, pw_bds)]
    blk = 16
    while blk < WKV_CHUNK:
        same_lo = (row // blk) == (col // blk)
        same_hi = (row // (2 * blk)) == (col // (2 * blk))
        sel = jnp.logical_and(same_hi, jnp.logical_not(same_lo))
        halves = [_mm(t, _bd(jnp.where(sel, n, 0.0), m)) for t, n in zip(ts, n_mats)]
        ts = [t + _mm(h, _bd(t, m)) for t, h in zip(ts, halves)]
        blk *= 2
    return ts


def _wkv_prepare(r, k, v, lw, kn, a, m):
    c = r.shape[0]
    cum = jnp.dot(m["tri"], lw, precision=HIGHEST, preferred_element_type=F32)
    mid = cum[c // 2 - 1:c // 2, :]
    end = cum[c - 1:c, :]
    e_abs = jnp.exp(cum)
    e_abs_prev = jnp.exp(cum - lw)
    e_mid = jnp.exp(-mid)
    e_neg = jnp.exp(mid - cum)
    e_end = jnp.exp(end - cum)
    g_end = jnp.exp(end)
    bvec = kn * a
    r0 = r * e_abs
    a0 = -kn * e_abs_prev
    rt = r0 * e_mid
    at = a0 * e_mid
    kt = k * e_neg
    bt = bvec * e_neg
    kh = k * e_end
    bh = bvec * e_end
    strict = m["row"] > m["col"]
    incl = m["row"] >= m["col"]
    out = []
    for gi in range(N_RWKV_HEADS // WKV_GROUP):
        sl = slice(gi * GW, (gi + 1) * GW)
        lhs = jnp.concatenate([at[:, sl], rt[:, sl]], axis=0)
        ab = _mm_nt(lhs, _bd(bt[:, sl], m))
        ak = _mm_nt(lhs, _bd(kt[:, sl], m))
        out.append(dict(
            a_ab=jnp.where(strict, ab[:c], 0.0), a_rb=jnp.where(incl, ab[c:], 0.0),
            a_ak=jnp.where(strict, ak[:c], 0.0), a_rk=jnp.where(incl, ak[c:], 0.0),
            vg=v[:, sl], a0=a0[:, sl], r0=r0[:, sl], g_end=g_end[:, sl],
            bk=jnp.concatenate([bh[:, sl], kh[:, sl]], axis=0).astype(BF16)))
    return out


def _wkv_kernel(r_ref, k_ref, v_ref, lw_ref, kn_ref, a_ref, g_ref, rk_ref, lg_ref, lb_ref,
                o_ref, s_scr, *, n_chunks):
    @pl.when(pl.program_id(1) == 0)
    def _():
        s_scr[...] = jnp.zeros_like(s_scr)

    n_groups = N_RWKV_HEADS // WKV_GROUP
    m = _wkv_masks()
    chunk_rows = [slice(ci * WKV_CHUNK, (ci + 1) * WKV_CHUNK) for ci in range(n_chunks)]

    probs = []
    for rows in chunk_rows:
        probs += _wkv_prepare(r_ref[rows, :], k_ref[rows, :], v_ref[rows, :], lw_ref[rows, :],
                              kn_ref[rows, :], a_ref[rows, :], m)
    ts = _unit_lower_inverse([p["a_ab"] for p in probs], m)
    vg_bds = [_bd(p["vg"], m) for p in probs]
    akvs = [_mm(p["a_ak"], vb) for p, vb in zip(probs, vg_bds)]
    t_bfs = [t.astype(BF16) for t in ts]
    xas = [_mm(t, _bd(p["a0"], m)) for t, p in zip(t_bfs, probs)]
    xvs = [_mm(t, _bd(akv, m)) for t, akv in zip(t_bfs, akvs)]
    yvs = [_mm(p["a_rk"], vb) for p, vb in zip(probs, vg_bds)]

    states = [s_scr[gi] for gi in range(n_groups)]
    for ci, rows in enumerate(chunk_rows):
        rkr = r_ref[rows, :] * k_ref[rows, :] * rk_ref[...]
        for gi in range(n_groups):
            p = probs[ci * n_groups + gi]
            sl = slice(gi * GW, (gi + 1) * GW)
            s0 = states[gi]
            s0_bd = _bd(s0, m)
            u = _mm_nt(xas[ci * n_groups + gi], s0_bd) + xvs[ci * n_groups + gi]
            y = _mm_nt(p["r0"], s0_bd) + _mm(p["a_rb"], _bd(u, m)) + yvs[ci * n_groups + gi]
            uv = jnp.concatenate([u, p["vg"]], axis=0).astype(BF16)
            gram = lax.dot_general(uv, p["bk"], (((0,), (0,)), ((), ())),
                                   preferred_element_type=F32) * m["bdf"]
            upd = gram[:HEAD_DIM]
            for hh in range(1, WKV_GROUP):
                upd = upd + gram[hh * HEAD_DIM:(hh + 1) * HEAD_DIM]
            states[gi] = s0 * p["g_end"] + upd
            d = y - _seg_sum(y, m) * (1.0 / HEAD_DIM)
            var = _seg_sum(d * d, m) * (1.0 / HEAD_DIM)
            yn = d * lax.rsqrt(var + RWKV_GN_EPS) * lg_ref[:, sl] + lb_ref[:, sl]
            bonus = _seg_sum(rkr[:, sl], m) * p["vg"]
            o_ref[rows, sl] = ((yn + bonus) * g_ref[rows, sl]).astype(BF16)
    for gi in range(n_groups):
        s_scr[gi] = states[gi]


def _wkv(r, k, v, lw, kn, a, g, r_k, lnx_g, lnx_b, *, batch, seq, rows):
    T = r.shape[0]
    nb = seq // rows
    blk = pl.BlockSpec((rows, D_RWKV), lambda b, i: (b * nb + i, 0))
    par = pl.BlockSpec((1, D_RWKV), lambda b, i: (0, 0))
    return pl.pallas_call(
        functools.partial(_wkv_kernel, n_chunks=rows // WKV_CHUNK),
        grid=(batch, nb),
        in_specs=[blk] * 7 + [par] * 3,
        out_specs=blk,
        out_shape=jax.ShapeDtypeStruct((T, D_RWKV), BF16),
        scratch_shapes=[pltpu.VMEM((N_RWKV_HEADS // WKV_GROUP, HEAD_DIM, GW), F32)],
        compiler_params=_cparams(("parallel", "arbitrary")),
        name="wkv7",
    )(r, k, v, lw, kn, a, g, r_k.reshape(1, -1), lnx_g.reshape(1, -1), lnx_b.reshape(1, -1))


def _dattn_kernel(q_ref, k_ref, v_ref, lq1_ref, lk1_ref, lq2_ref, lk2_ref, sg_ref, o_ref,
                  qs_scr, m_scr, l_scr, acc_scr, *, tq, lam_init):
    qi = pl.program_id(2)
    q = q_ref[...]
    lane = lax.broadcasted_iota(jnp.int32, q.shape, 1)
    zero = jnp.zeros_like(q)
    qs_scr[:tq, :] = jnp.where(lane < HEAD_DIM, q, zero)
    qs_scr[tq:, :] = jnp.where(lane < HEAD_DIM, zero, q)
    m_scr[...] = jnp.full_like(m_scr, -jnp.inf)
    l_scr[...] = jnp.zeros_like(l_scr)
    acc_scr[...] = jnp.zeros_like(acc_scr)

    def step(j, masked):
        start = pl.multiple_of(j * tq, tq)
        kb = k_ref[pl.ds(start, tq), :]
        vb = v_ref[pl.ds(start, tq), :]
        for rt in range(2 * tq // ATTN_ROW_TILE):
            rows = slice(rt * ATTN_ROW_TILE, (rt + 1) * ATTN_ROW_TILE)
            s = _dot_nt(qs_scr[rows, :], kb)
            if masked:
                r_pos = (lax.broadcasted_iota(jnp.int32, s.shape, 0) + rt * ATTN_ROW_TILE) % tq
                c_pos = lax.broadcasted_iota(jnp.int32, s.shape, 1)
                s = jnp.where(c_pos <= r_pos, s, -jnp.inf)
            m_prev = m_scr[rows, :]
            m_new = jnp.maximum(m_prev, jnp.max(s, axis=-1, keepdims=True))
            alpha = jnp.exp2(m_prev - m_new)
            p = jnp.exp2(s - jnp.concatenate([m_new] * (tq // LANES), axis=1))
            l_scr[rows, :] = alpha * l_scr[rows, :] + jnp.sum(p, axis=-1, keepdims=True)
            acc_scr[rows, :] = alpha * acc_scr[rows, :] + jnp.dot(
                p.astype(BF16), vb, preferred_element_type=F32)
            m_scr[rows, :] = m_new

    def body(j, carry):
        step(j, False)
        return carry

    lax.fori_loop(0, qi, body, 0)
    step(qi, True)

    lam = (jnp.exp(jnp.sum(lq1_ref[...] * lk1_ref[...], axis=-1, keepdims=True))
           - jnp.exp(jnp.sum(lq2_ref[...] * lk2_ref[...], axis=-1, keepdims=True)) + lam_init)
    on = acc_scr[...] / l_scr[...]
    o = on[:tq, :] - lam * on[tq:, :]
    o = o * lax.rsqrt(jnp.mean(o * o, axis=-1, keepdims=True) + 1e-5) * sg_ref[...]
    o_ref[...] = (o * (1.0 - lam_init)).astype(BF16)


def _dattn(q, k, v, lq1, lk1, lq2, lk2, subln_g, *, batch, seq, tq, lam_init):
    T = q.shape[0]
    nq = seq // tq
    hd = 2 * HEAD_DIM
    qmap = lambda b, h, i: (b * nq + i, h)
    kvmap = lambda b, h, i: (b, h)
    par = lambda b, h, i: (0, 0)
    vec = lambda a: a.reshape(1, -1)
    return pl.pallas_call(
        functools.partial(_dattn_kernel, tq=tq, lam_init=lam_init),
        grid=(batch, N_DIFF_HEADS, nq),
        in_specs=[
            pl.BlockSpec((tq, hd), qmap),
            pl.BlockSpec((seq, hd), kvmap),
            pl.BlockSpec((seq, hd), kvmap),
            pl.BlockSpec((1, HEAD_DIM), par),
            pl.BlockSpec((1, HEAD_DIM), par),
            pl.BlockSpec((1, HEAD_DIM), par),
            pl.BlockSpec((1, HEAD_DIM), par),
            pl.BlockSpec((1, hd), par),
        ],
        out_specs=pl.BlockSpec((tq, hd), qmap),
        out_shape=jax.ShapeDtypeStruct((T, D_DIFF), BF16),
        scratch_shapes=[
            pltpu.VMEM((2 * tq, hd), BF16),
            pltpu.VMEM((2 * tq, LANES), F32),
            pltpu.VMEM((2 * tq, LANES), F32),
            pltpu.VMEM((2 * tq, hd), F32),
        ],
        compiler_params=_cparams(("parallel", "parallel", "arbitrary")),
        name="diff_attn",
    )(q, k, v, vec(lq1), vec(lk1), vec(lq2), vec(lk2), vec(subln_g))


def _proj_ln_kernel(yr_ref, yd_ref, w_ref, x_ref, g_ref, b_ref, o_ref):
    mix = jnp.dot(yr_ref[...], w_ref[:D_RWKV, :], preferred_element_type=F32)
    mix = mix + jnp.dot(yd_ref[...], w_ref[D_RWKV:, :], preferred_element_type=F32)
    o_ref[...] = _layer_norm(DEEPNORM_ALPHA * x_ref[...] + mix, g_ref[...], b_ref[...])


def _proj_ln(yr, yd, w_out, x2, g, b, *, tm):
    T = x2.shape[0]
    row = lambda i: (i, 0)
    full = lambda i: (0, 0)
    return pl.pallas_call(
        _proj_ln_kernel,
        grid=(T // tm,),
        in_specs=[
            pl.BlockSpec((tm, D_RWKV), row),
            pl.BlockSpec((tm, D_DIFF), row),
            pl.BlockSpec((D_RWKV + D_DIFF, D_MODEL), full),
            pl.BlockSpec((tm, D_MODEL), row),
            pl.BlockSpec((1, D_MODEL), full),
            pl.BlockSpec((1, D_MODEL), full),
        ],
        out_specs=pl.BlockSpec((tm, D_MODEL), row),
        out_shape=jax.ShapeDtypeStruct((T, D_MODEL), F32),
        compiler_params=_cparams(("parallel",)),
        name="ab_out_ln",
    )(yr, yd, w_out, x2, g.reshape(1, -1), b.reshape(1, -1))


def _gmlp_kernel(x_ref, win_ref, bin_ref, lng_ref, lnb_ref, ws_ref, bs_ref, wout_ref, g_ref, b_ref,
                 o_ref, gated_scr, *, tm):
    x = x_ref[...]
    h = jnp.dot(x.astype(BF16), win_ref[...], preferred_element_type=F32) + bin_ref[...]
    h = 0.5 * h * (1.0 + lax.erf(h * (0.5 ** 0.5)))
    u = h[:, :D_GMLP]
    v = _layer_norm(h[:, D_GMLP:], lng_ref[...], lnb_ref[...]).astype(BF16)
    row = lax.broadcasted_iota(jnp.int32, (CHUNK, CHUNK), 0)
    col = lax.broadcasted_iota(jnp.int32, (CHUNK, CHUNK), 1)
    gw = D_GMLP // GMLP_GROUPS
    for gi in range(GMLP_GROUPS):
        ws = jnp.where(row >= col, ws_ref[gi], 0.0).astype(BF16)
        for c in range(tm // CHUNK):
            rs = slice(c * CHUNK, (c + 1) * CHUNK)
            cs = slice(gi * gw, (gi + 1) * gw)
            mixed = jnp.dot(ws, v[rs, cs], preferred_element_type=F32) + bs_ref[gi]
            gated_scr[rs, cs] = (u[rs, cs] * mixed).astype(BF16)
    mix = jnp.dot(gated_scr[...], wout_ref[...], preferred_element_type=F32)
    o_ref[...] = _layer_norm(DEEPNORM_ALPHA * x + mix, g_ref[...], b_ref[...])


def _gmlp(x2, w_in, b_in, ln_g, ln_b, w_s, b_s, w_out, g, b, *, tm):
    T = x2.shape[0]
    row = lambda i: (i, 0)
    full = lambda i: (0, 0)
    full3 = lambda i: (0, 0, 0)
    vec = lambda a: a.reshape(1, -1)
    gw = D_GMLP // GMLP_GROUPS
    bs_b = jnp.broadcast_to(b_s[:, :, None], (GMLP_GROUPS, CHUNK, gw))
    return pl.pallas_call(
        functools.partial(_gmlp_kernel, tm=tm),
        grid=(T // tm,),
        in_specs=[
            pl.BlockSpec((tm, D_MODEL), row),
            pl.BlockSpec((D_MODEL, 2 * D_GMLP), full),
            pl.BlockSpec((1, 2 * D_GMLP), full),
            pl.BlockSpec((1, D_GMLP), full),
            pl.BlockSpec((1, D_GMLP), full),
            pl.BlockSpec((GMLP_GROUPS, CHUNK, CHUNK), full3),
            pl.BlockSpec((GMLP_GROUPS, CHUNK, gw), full3),
            pl.BlockSpec((D_GMLP, D_MODEL), full),
            pl.BlockSpec((1, D_MODEL), full),
            pl.BlockSpec((1, D_MODEL), full),
        ],
        out_specs=pl.BlockSpec((tm, D_MODEL), row),
        out_shape=jax.ShapeDtypeStruct((T, D_MODEL), F32),
        scratch_shapes=[pltpu.VMEM((tm, D_GMLP), BF16)],
        compiler_params=_cparams(("parallel",)),
        name="gmlp",
    )(x2, w_in, vec(b_in), vec(ln_g), vec(ln_b), w_s, bs_b, w_out, vec(g), vec(b))


def _ffn_kernel(x_ref, xh_ref, wup_ref, cw_ref, cb_ref, wd_ref, g_ref, b_ref, o_ref,
                xb_scr, acc_scr, *, blocks_per_seq, tf, n_slabs):
    i = pl.program_id(0)
    j = pl.program_id(1)

    @pl.when(j == 0)
    def _():
        xb_scr[...] = x_ref[...].astype(BF16)
        acc_scr[...] = jnp.zeros_like(acc_scr)

    wup = wup_ref[0]
    gv = jnp.dot(xb_scr[...], wup, preferred_element_type=F32)
    gate = gv[:, :tf]
    val = gv[:, tf:]
    gh = jnp.dot(xh_ref[...].astype(BF16), wup[:, :tf], preferred_element_type=F32)
    gh = jnp.where(i % blocks_per_seq == 0, 0.0, gh)
    cw = cw_ref[0]
    conv = (cb_ref[0] + cw[0:1, :] * _shift_rows(gate, 2, gh) + cw[1:2, :] * _shift_rows(gate, 1, gh)
            + cw[2:3, :] * gate)
    hid = conv * jax.nn.sigmoid(conv) * val
    acc_scr[...] += jnp.dot(hid.astype(BF16), wd_ref[0], preferred_element_type=F32)

    @pl.when(j == n_slabs - 1)
    def _():
        o_ref[...] = _layer_norm(DEEPNORM_ALPHA * x_ref[...] + acc_scr[...], g_ref[...], b_ref[...])


def _ffn(x2, w_up, conv_w, conv_b, w_down, g, b, *, seq, tm, tf):
    T = x2.shape[0]
    n_slabs = D_FF // tf
    bps = seq // tm
    wg = w_up[:, :D_FF].reshape(D_MODEL, n_slabs, tf)
    wv = w_up[:, D_FF:].reshape(D_MODEL, n_slabs, tf)
    wup = jnp.transpose(jnp.concatenate([wg, wv], axis=2), (1, 0, 2)).astype(BF16)
    cw = jnp.transpose(conv_w.reshape(3, n_slabs, tf), (1, 0, 2))
    cb = conv_b.reshape(n_slabs, 1, tf)
    wd = w_down.reshape(n_slabs, tf, D_MODEL).astype(BF16)
    row = lambda i, j: (i, 0)
    full = lambda i, j: (0, 0)
    slab = lambda i, j: (j, 0, 0)
    halo = lambda i, j: (jnp.maximum(i * (tm // HALO_ROWS) - 1, 0), 0)
    return pl.pallas_call(
        functools.partial(_ffn_kernel, blocks_per_seq=bps, tf=tf, n_slabs=n_slabs),
        grid=(T // tm, n_slabs),
        in_specs=[
            pl.BlockSpec((tm, D_MODEL), row),
            pl.BlockSpec((HALO_ROWS, D_MODEL), halo),
            pl.BlockSpec((1, D_MODEL, 2 * tf), slab),
            pl.BlockSpec((1, 3, tf), slab),
            pl.BlockSpec((1, 1, tf), slab),
            pl.BlockSpec((1, tf, D_MODEL), slab),
            pl.BlockSpec((1, D_MODEL), full),
            pl.BlockSpec((1, D_MODEL), full),
        ],
        out_specs=pl.BlockSpec((tm, D_MODEL), row),
        out_shape=jax.ShapeDtypeStruct((T, D_MODEL), F32),
        scratch_shapes=[pltpu.VMEM((tm, D_MODEL), BF16), pltpu.VMEM((tm, D_MODEL), F32)],
        compiler_params=_cparams(("parallel", "arbitrary")),
        name="conv_ffn",
    )(x2, x2, wup, cw, cb, wd, g.reshape(1, -1), b.reshape(1, -1))


def _rope_tables(seq):
    half = ROPE_DIM // 2
    inv_freq = ROPE_THETA ** (-jnp.arange(0, ROPE_DIM, 2, dtype=F32) / ROPE_DIM)
    ang = jnp.arange(seq, dtype=F32)[:, None] * inv_freq[None, :]
    cos, sin = jnp.cos(ang), jnp.sin(ang)
    ones = jnp.ones((seq, HEAD_DIM - ROPE_DIM), F32)
    zeros = jnp.zeros((seq, HEAD_DIM - ROPE_DIM), F32)
    zh = jnp.zeros((seq, half), F32)
    c64 = jnp.concatenate([cos, cos, ones], axis=1)
    s1_64 = jnp.concatenate([zh, sin, zeros], axis=1)
    s2_64 = jnp.concatenate([-sin, zh, zeros], axis=1)
    two = lambda t: jnp.concatenate([t, t], axis=1)
    return two(c64), two(s1_64), two(s2_64)


def kernel(x, ab_w_in, ab_shift_mu, ab_w0, ab_w2, ab_a0, ab_a2, ab_g2, ab_k_k, ab_k_a, ab_r_k, ab_lnx_g, ab_lnx_b, ab_lam_q1, ab_lam_k1, ab_lam_q2, ab_lam_k2, ab_subln_g, ab_w_out, c_w_in, c_b_in, c_ln_g, c_ln_b, c_w_s, c_b_s, c_w_out, ln1_g, ln1_b, ffn_w_up, ffn_conv_w, ffn_conv_b, ffn_w_down, ln2_g, ln2_b):
    batch, seq, _ = x.shape
    x2 = x.reshape(batch * seq, D_MODEL)
    rc, rs1, rs2 = _rope_tables(seq)
    tm_in = min(512, seq)
    tm_ffn = min(1024, seq)
    tm_gmlp = min(512, seq)
    tq = min(512, seq)
    wkv_rows = min(256, seq)
    for i in range(DEPTH):
        j = i // 2
        if i % 2 == 0:
            (r, k, v, lw, kn, a, g, qd, kd, vd) = _ab_in(
                x2, ab_w_in[j].astype(BF16), ab_shift_mu[j], ab_w0[j], ab_w2[j], ab_a0[j],
                ab_a2[j].astype(BF16), ab_g2[j].astype(BF16), ab_k_k[j], ab_k_a[j], rc, rs1, rs2,
                seq=seq, tm=tm_in)
            y_r = _wkv(r, k, v, lw, kn, a, g, ab_r_k[j], ab_lnx_g[j], ab_lnx_b[j],
                       batch=batch, seq=seq, rows=wkv_rows)
            lam_init = 0.8 - 0.6 * math.exp(-0.3 * i)
            y_d = _dattn(qd, kd, vd, ab_lam_q1[j], ab_lam_k1[j], ab_lam_q2[j], ab_lam_k2[j],
                         ab_subln_g[j], batch=batch, seq=seq, tq=tq, lam_init=lam_init)
            x2 = _proj_ln(y_r, y_d, ab_w_out[j].astype(BF16), x2, ln1_g[i], ln1_b[i], tm=tm_in)
        else:
            x2 = _gmlp(x2, c_w_in[j].astype(BF16), c_b_in[j], c_ln_g[j], c_ln_b[j], c_w_s[j],
                       c_b_s[j], c_w_out[j].astype(BF16), ln1_g[i], ln1_b[i], tm=tm_gmlp)
        x2 = _ffn(x2, ffn_w_up[i], ffn_conv_w[i], ffn_conv_b[i], ffn_w_down[i], ln2_g[i], ln2_b[i],
                  seq=seq, tm=tm_ffn, tf=256)
    return x2.reshape(batch, seq, D_MODEL)
```

```python
import functools
import math

import jax
import jax.numpy as jnp
from jax import lax
from jax.experimental import pallas as pl
from jax.experimental.pallas import tpu as pltpu

F32 = jnp.float32
BF16 = jnp.bfloat16
HIGHEST = lax.Precision.HIGHEST

D_MODEL = 1024
HEAD_DIM = 64
N_RWKV_HEADS = 8
D_RWKV = N_RWKV_HEADS * HEAD_DIM
N_DIFF_HEADS = 4
D_DIFF = N_DIFF_HEADS * 2 * HEAD_DIM
DECAY_LORA = 64
AAA_LORA = 64
GATE_LORA = 128
RWKV_COLS = 3 * D_RWKV + DECAY_LORA + AAA_LORA + GATE_LORA
AB_COLS = RWKV_COLS + 3 * D_DIFF
RWKV_GN_EPS = 64e-5
ROPE_THETA = 500000.0
ROPE_DIM = HEAD_DIM // 4
CHUNK = 128
GMLP_GROUPS = 8
D_GMLP = D_MODEL
D_FF = 2816
DEPTH = 4
DEEPNORM_ALPHA = (2 * DEPTH) ** 0.25

LANES = 128
HALO_ROWS = 8
WKV_CHUNK = 64
VMEM_LIMIT = 56 * 1024 * 1024
LOG2E = math.log2(math.e)


def _cparams(sem):
    return pltpu.CompilerParams(dimension_semantics=sem, vmem_limit_bytes=VMEM_LIMIT)


def _dot(a, b):
    return jnp.dot(a.astype(BF16), b.astype(BF16), preferred_element_type=F32)


def _dot_nt(a, b):
    return lax.dot_general(a.astype(BF16), b.astype(BF16), (((1,), (1,)), ((), ())),
                           preferred_element_type=F32)


def _layer_norm(z, g, b, eps=1e-5):
    mu = jnp.mean(z, axis=-1, keepdims=True)
    d = z - mu
    var = jnp.mean(d * d, axis=-1, keepdims=True)
    return d * lax.rsqrt(var + eps) * g + b


def _shift_rows(t, k, halo):
    rolled = pltpu.roll(t, k, 0)
    row = lax.broadcasted_iota(jnp.int32, t.shape, 0)
    out = rolled
    for j in range(k):
        out = jnp.where(row == j, halo[HALO_ROWS - k + j:HALO_ROWS - k + j + 1, :], out)
    return out


def _ab_in_kernel(x_ref, xh_ref, w_ref, mu_ref, w0_ref, w2_ref, a0_ref, a2_ref, g2_ref, kk_ref,
                  ka_ref, rc_ref, rs1_ref, rs2_ref,
                  r_out, k_out, v_out, lw_out, kn_out, a_out, g_out, q_out, kd_out, vd_out,
                  *, blocks_per_seq):
    i = pl.program_id(0)
    xb = x_ref[...].astype(BF16)
    p = jnp.dot(xb, w_ref[...], preferred_element_type=F32)
    ph = jnp.dot(xh_ref[...].astype(BF16), w_ref[:, :RWKV_COLS], preferred_element_type=F32)
    ph = jnp.where(i % blocks_per_seq == 0, 0.0, ph)
    pr = p[:, :RWKV_COLS]
    xs = pr + mu_ref[...] * (_shift_rows(pr, 1, ph) - pr)

    r = xs[:, :D_RWKV]
    k = xs[:, D_RWKV:2 * D_RWKV]
    v = xs[:, 2 * D_RWKV:3 * D_RWKV]
    o = 3 * D_RWKV
    xw = xs[:, o:o + DECAY_LORA]
    xa = xs[:, o + DECAY_LORA:o + DECAY_LORA + AAA_LORA]
    xg = xs[:, o + DECAY_LORA + AAA_LORA:]

    z = w0_ref[...] + jnp.dot(jnp.tanh(xw), w2_ref[...], precision=HIGHEST,
                              preferred_element_type=F32)
    softplus_neg = jnp.maximum(-z, 0.0) + jnp.log1p(jnp.exp(-jnp.abs(z)))
    w = -softplus_neg - 0.5
    lw_out[...] = -jnp.exp(w)
    a = jax.nn.sigmoid(a0_ref[...] + _dot(xa, a2_ref[...]))
    a_out[...] = a
    g_out[...] = _dot(jax.nn.sigmoid(xg), g2_ref[...])

    kx = k * kk_ref[...]
    sq = kx * kx
    lane = lax.broadcasted_iota(jnp.int32, (sq.shape[0], LANES), 1)
    lo = lane < HEAD_DIM
    for c in range(D_RWKV // LANES):
        blk = sq[:, c * LANES:(c + 1) * LANES]
        n_lo = jnp.sqrt(jnp.sum(jnp.where(lo, blk, 0.0), axis=-1, keepdims=True))
        n_hi = jnp.sqrt(jnp.sum(jnp.where(lo, 0.0, blk), axis=-1, keepdims=True))
        norm = jnp.maximum(jnp.where(lo, n_lo, n_hi), 1e-12)
        kn_out[:, c * LANES:(c + 1) * LANES] = kx[:, c * LANES:(c + 1) * LANES] / norm
    r_out[...] = r
    k_out[...] = k * (1.0 + (a - 1.0) * ka_ref[...])
    v_out[...] = v

    pd = p[:, RWKV_COLS:]
    rc = jnp.concatenate([rc_ref[...]] * (D_DIFF // LANES), axis=1)
    rs1 = jnp.concatenate([rs1_ref[...]] * (D_DIFF // LANES), axis=1)
    rs2 = jnp.concatenate([rs2_ref[...]] * (D_DIFF // LANES), axis=1)
    half = ROPE_DIM // 2

    def rope(t):
        return t * rc + pltpu.roll(t, half, 1) * rs1 + pltpu.roll(t, D_DIFF - half, 1) * rs2

    q_out[...] = (rope(pd[:, :D_DIFF]) * (HEAD_DIM ** -0.5 * LOG2E)).astype(BF16)
    kd_out[...] = rope(pd[:, D_DIFF:2 * D_DIFF]).astype(BF16)
    vd_out[...] = pd[:, 2 * D_DIFF:].astype(BF16)


def _ab_in(x2, w_in, mu, w0, w2, a0, a2, g2, k_k, k_a, rc, rs1, rs2, *, seq, tm):
    T = x2.shape[0]
    n = T // tm
    bps = seq // tm
    row = lambda i: (i, 0)
    full = lambda i: (0, 0)
    halo = lambda i: (jnp.maximum(i * (tm // HALO_ROWS) - 1, 0), 0)
    rope_map = lambda i: (i % bps, 0)
    vec = lambda a: a.reshape(1, -1)
    f32_out = jax.ShapeDtypeStruct((T, D_RWKV), F32)
    bf_out = jax.ShapeDtypeStruct((T, D_DIFF), BF16)
    out_spec = pl.BlockSpec((tm, D_RWKV), row)
    return pl.pallas_call(
        functools.partial(_ab_in_kernel, blocks_per_seq=bps),
        grid=(n,),
        in_specs=[
            pl.BlockSpec((tm, D_MODEL), row),
            pl.BlockSpec((HALO_ROWS, D_MODEL), halo),
            pl.BlockSpec((D_MODEL, AB_COLS), full),
            pl.BlockSpec((1, RWKV_COLS), full),
            pl.BlockSpec((1, D_RWKV), full),
            pl.BlockSpec((DECAY_LORA, D_RWKV), full),
            pl.BlockSpec((1, D_RWKV), full),
            pl.BlockSpec((AAA_LORA, D_RWKV), full),
            pl.BlockSpec((GATE_LORA, D_RWKV), full),
            pl.BlockSpec((1, D_RWKV), full),
            pl.BlockSpec((1, D_RWKV), full),
            pl.BlockSpec((tm, LANES), rope_map),
            pl.BlockSpec((tm, LANES), rope_map),
            pl.BlockSpec((tm, LANES), rope_map),
        ],
        out_specs=[out_spec] * 10,
        out_shape=[f32_out] * 7 + [bf_out] * 3,
        compiler_params=_cparams(("parallel",)),
        name="ab_in",
    )(x2, x2, w_in, vec(mu), vec(w0), w2, vec(a0), a2, g2, vec(k_k), vec(k_a), rc, rs1, rs2)


WKV_GROUP = 4
GW = WKV_GROUP * HEAD_DIM


def _wkv_masks():
    c = WKV_CHUNK
    row = lax.broadcasted_iota(jnp.int32, (c, GW), 0)
    col = lax.broadcasted_iota(jnp.int32, (c, GW), 1) % HEAD_DIM
    brow = lax.broadcasted_iota(jnp.int32, (GW, GW), 0) // HEAD_DIM
    bcol = lax.broadcasted_iota(jnp.int32, (GW, GW), 1) // HEAD_DIM
    tri = jnp.where(lax.broadcasted_iota(jnp.int32, (c, c), 0)
                    >= lax.broadcasted_iota(jnp.int32, (c, c), 1), 1.0, 0.0)
    bdf = jnp.where(brow == bcol, 1.0, 0.0)
    return dict(row=row, col=col, tri=tri, bdf=bdf, bd=bdf.astype(BF16))


def _bd(w, m):
    return jnp.concatenate([w.astype(BF16)] * WKV_GROUP, axis=0) * m["bd"]


def _mm(x, w_bd):
    return jnp.dot(x.astype(BF16), w_bd, preferred_element_type=F32)


def _mm_nt(x, w_bd):
    return lax.dot_general(x.astype(BF16), w_bd, (((1,), (1,)), ((), ())),
                           preferred_element_type=F32)


def _seg_sum(x, m):
    hi = x.astype(BF16)
    lo = (x - hi.astype(F32)).astype(BF16)
    s = jnp.dot(jnp.concatenate([hi, lo], axis=0), m["bd"], preferred_element_type=F32)
    return s[:x.shape[0]] + s[x.shape[0]:]


def _unit_lower_inverse(n_mats, m):
    row, col = m["row"], m["col"]
    eye = jnp.where(row == col, 1.0, 0.0)
    same16 = (row // 16) == (col // 16)
    pws = [jnp.where(same16, n, 0.0) for n in n_mats]
    ts = [eye + p for p in pws]
    pw_bds = [_bd(p, m) for p in pws]
    for _ in range(3):
        pws = [_mm(p, b) for p, b in zip(pws, pw_bds)]
        pw_bds = [_bd(p, m) for p in pws]
        ts = [t + _mm(t, b) for t, b in zip(ts, pw_bds)]
    blk = 16
    while blk < WKV_CHUNK:
        same_lo = (row // blk) == (col // blk)
        same_hi = (row // (2 * blk)) == (col // (2 * blk))
        sel = jnp.logical_and(same_hi, jnp.logical_not(same_lo))
        halves = [_mm(t, _bd(jnp.where(sel, n, 0.0), m)) for t, n in zip(ts, n_mats)]
        ts = [t + _mm(h, _bd(t, m)) for t, h in zip(ts, halves)]
        blk *= 2
    return ts


def _wkv_prepare(r, k, v, lw, kn, a, m):
    c = r.shape[0]
    cum = jnp.dot(m["tri"], lw, precision=HIGHEST, preferred_element_type=F32)
    mid = cum[c // 2 - 1:c // 2, :]
    end = cum[c - 1:c, :]
    e_abs = jnp.exp(cum)
    e_abs_prev = jnp.exp(cum - lw)
    e_mid = jnp.exp(-mid)
    e_neg = jnp.exp(mid - cum)
    e_end = jnp.exp(end - cum)
    g_end = jnp.exp(end)
    bvec = kn * a
    r0 = r * e_abs
    a0 = -kn * e_abs_prev
    rt = r0 * e_mid
    at = a0 * e_mid
    kt = k * e_neg
    bt = bvec * e_neg
    kh = k * e_end
    bh = bvec * e_end
    strict = m["row"] > m["col"]
    incl = m["row"] >= m["col"]
    out = []
    for gi in range(N_RWKV_HEADS // WKV_GROUP):
        sl = slice(gi * GW, (gi + 1) * GW)
        lhs = jnp.concatenate([at[:, sl], rt[:, sl]], axis=0)
        ab = _mm_nt(lhs, _bd(bt[:, sl], m))
        ak = _mm_nt(lhs, _bd(kt[:, sl], m))
        out.append(dict(
            a_ab=jnp.where(strict, ab[:c], 0.0), a_rb=jnp.where(incl, ab[c:], 0.0),
            a_ak=jnp.where(strict, ak[:c], 0.0), a_rk=jnp.where(incl, ak[c:], 0.0),
            vg=v[:, sl], a0=a0[:, sl], r0=r0[:, sl], g_end=g_end[:, sl],
            bk=jnp.concatenate([bh[:, sl], kh[:, sl]], axis=0).astype(BF16)))
    return out


def _wkv_kernel(r_ref, k_ref, v_ref, lw_ref, kn_ref, a_ref, g_ref, rk_ref, lg_ref, lb_ref,
                o_ref, s_scr, *, n_chunks):
    @pl.when(pl.program_id(1) == 0)
    def _():
        s_scr[...] = jnp.zeros_like(s_scr)

    n_groups = N_RWKV_HEADS // WKV_GROUP
    m = _wkv_masks()
    chunk_rows = [slice(ci * WKV_CHUNK, (ci + 1) * WKV_CHUNK) for ci in range(n_chunks)]

    probs = []
    for rows in chunk_rows:
        probs += _wkv_prepare(r_ref[rows, :], k_ref[rows, :], v_ref[rows, :], lw_ref[rows, :],
                              kn_ref[rows, :], a_ref[rows, :], m)
    ts = _unit_lower_inverse([p["a_ab"] for p in probs], m)
    vg_bds = [_bd(p["vg"], m) for p in probs]
    akvs = [_mm(p["a_ak"], vb) for p, vb in zip(probs, vg_bds)]
    t_bfs = [t.astype(BF16) for t in ts]
    xas = [_mm(t, _bd(p["a0"], m)) for t, p in zip(t_bfs, probs)]
    xvs = [_mm(t, _bd(akv, m)) for t, akv in zip(t_bfs, akvs)]
    yvs = [_mm(p["a_rk"], vb) for p, vb in zip(probs, vg_bds)]

    states = [s_scr[gi] for gi in range(n_groups)]
    for ci, rows in enumerate(chunk_rows):
        rkr = r_ref[rows, :] * k_ref[rows, :] * rk_ref[...]
        for gi in range(n_groups):
            p = probs[ci * n_groups + gi]
            sl = slice(gi * GW, (gi + 1) * GW)
            s0 = states[gi]
            s0_bd = _bd(s0, m)
            u = _mm_nt(xas[ci * n_groups + gi], s0_bd) + xvs[ci * n_groups + gi]
            y = _mm_nt(p["r0"], s0_bd) + _mm(p["a_rb"], _bd(u, m)) + yvs[ci * n_groups + gi]
            uv = jnp.concatenate([u, p["vg"]], axis=0).astype(BF16)
            gram = lax.dot_general(uv, p["bk"], (((0,), (0,)), ((), ())),
                                   preferred_element_type=F32) * m["bdf"]
            upd = gram[:HEAD_DIM]
            for hh in range(1, WKV_GROUP):
                upd = upd + gram[hh * HEAD_DIM:(hh + 1) * HEAD_DIM]
            states[gi] = s0 * p["g_end"] + upd
            d = y - _seg_sum(y, m) * (1.0 / HEAD_DIM)
            var = _seg_sum(d * d, m) * (1.0 / HEAD_DIM)
            yn = d * lax.rsqrt(var + RWKV_GN_EPS) * lg_ref[:, sl] + lb_ref[:, sl]
            bonus = _seg_sum(rkr[:, sl], m) * p["vg"]
            o_ref[rows, sl] = ((yn + bonus) * g_ref[rows, sl]).astype(BF16)
    for gi in range(n_groups):
        s_scr[gi] = states[gi]


def _wkv(r, k, v, lw, kn, a, g, r_k, lnx_g, lnx_b, *, batch, seq, rows):
    T = r.shape[0]
    nb = seq // rows
    blk = pl.BlockSpec((rows, D_RWKV), lambda b, i: (b * nb + i, 0))
    par = pl.BlockSpec((1, D_RWKV), lambda b, i: (0, 0))
    return pl.pallas_call(
        functools.partial(_wkv_kernel, n_chunks=rows // WKV_CHUNK),
        grid=(batch, nb),
        in_specs=[blk] * 7 + [par] * 3,
        out_specs=blk,
        out_shape=jax.ShapeDtypeStruct((T, D_RWKV), BF16),
        scratch_shapes=[pltpu.VMEM((N_RWKV_HEADS // WKV_GROUP, HEAD_DIM, GW), F32)],
        compiler_params=_cparams(("parallel", "arbitrary")),
        name="wkv7",
    )(r, k, v, lw, kn, a, g, r_k.reshape(1, -1), lnx_g.reshape(1, -1), lnx_b.reshape(1, -1))


def _dattn_kernel(q_ref, k_ref, v_ref, lq1_ref, lk1_ref, lq2_ref, lk2_ref, sg_ref, o_ref,
                  qs_scr, m_scr, l_scr, acc_scr, s0_scr, s1_scr, p0_scr, p1_scr, al0_scr, al1_scr,
                  *, tq, lam_init):
    qi = pl.program_id(2)
    q = q_ref[...]
    lane = lax.broadcasted_iota(jnp.int32, q.shape, 1)
    zero = jnp.zeros_like(q)
    qs_scr[:tq, :] = jnp.where(lane < HEAD_DIM, q, zero)
    qs_scr[tq:, :] = jnp.where(lane < HEAD_DIM, zero, q)
    m_scr[...] = jnp.full_like(m_scr, -jnp.inf)
    l_scr[...] = jnp.zeros_like(l_scr)
    acc_scr[...] = jnp.zeros_like(acc_scr)

    bufs = ((s0_scr, p0_scr, al0_scr), (s1_scr, p1_scr, al1_scr))

    def scores(j, par):
        start = pl.multiple_of(j * tq, tq)
        bufs[par][0][...] = _dot_nt(qs_scr[...], k_ref[pl.ds(start, tq), :])

    def softmax(par, masked):
        s_ref, p_ref, al_ref = bufs[par]
        s = s_ref[...]
        if masked:
            r_pos = lax.broadcasted_iota(jnp.int32, s.shape, 0) % tq
            c_pos = lax.broadcasted_iota(jnp.int32, s.shape, 1)
            s = jnp.where(c_pos <= r_pos, s, -jnp.inf)
        m_prev = m_scr[...]
        m_new = jnp.maximum(m_prev, jnp.max(s, axis=-1, keepdims=True))
        alpha = jnp.exp2(m_prev - m_new)
        p = jnp.exp2(s - jnp.concatenate([m_new] * (tq // LANES), axis=1))
        l_scr[...] = alpha * l_scr[...] + jnp.sum(p, axis=-1, keepdims=True)
        m_scr[...] = m_new
        p_ref[...] = p.astype(BF16)
        al_ref[...] = alpha

    def values(j, par):
        _, p_ref, al_ref = bufs[par]
        start = pl.multiple_of(j * tq, tq)
        acc_scr[...] = al_ref[...] * acc_scr[...] + jnp.dot(
            p_ref[...], v_ref[pl.ds(start, tq), :], preferred_element_type=F32)

    scores(0, 0)

    @pl.when(qi == 0)
    def _():
        softmax(0, True)
        values(0, 0)

    @pl.when(qi > 0)
    def _():
        scores(1, 1)
        softmax(0, False)

        def body(i, carry):
            t = 2 + 2 * i
            values(t - 2, 0)
            scores(t, 0)
            softmax(1, False)
            values(t - 1, 1)
            scores(t + 1, 1)
            softmax(0, False)
            return carry

        lax.fori_loop(0, (qi - 1) // 2, body, 0)

        @pl.when(qi % 2 == 1)
        def _():
            values(qi - 1, 0)
            softmax(1, True)
            values(qi, 1)

        @pl.when(qi % 2 == 0)
        def _():
            values(qi - 2, 0)
            scores(qi, 0)
            softmax(1, False)
            values(qi - 1, 1)
            softmax(0, True)
            values(qi, 0)

    lam = (jnp.exp(jnp.sum(lq1_ref[...] * lk1_ref[...], axis=-1, keepdims=True))
           - jnp.exp(jnp.sum(lq2_ref[...] * lk2_ref[...], axis=-1, keepdims=True)) + lam_init)
    on = acc_scr[...] / l_scr[...]
    o = on[:tq, :] - lam * on[tq:, :]
    o = o * lax.rsqrt(jnp.mean(o * o, axis=-1, keepdims=True) + 1e-5) * sg_ref[...]
    o_ref[...] = (o * (1.0 - lam_init)).astype(BF16)


def _dattn(q, k, v, lq1, lk1, lq2, lk2, subln_g, *, batch, seq, tq, lam_init):
    T = q.shape[0]
    nq = seq // tq
    hd = 2 * HEAD_DIM
    qmap = lambda b, h, i: (b * nq + i, h)
    kvmap = lambda b, h, i: (b, h)
    par = lambda b, h, i: (0, 0)
    vec = lambda a: a.reshape(1, -1)
    return pl.pallas_call(
        functools.partial(_dattn_kernel, tq=tq, lam_init=lam_init),
        grid=(batch, N_DIFF_HEADS, nq),
        in_specs=[
            pl.BlockSpec((tq, hd), qmap),
            pl.BlockSpec((seq, hd), kvmap),
            pl.BlockSpec((seq, hd), kvmap),
            pl.BlockSpec((1, HEAD_DIM), par),
            pl.BlockSpec((1, HEAD_DIM), par),
            pl.BlockSpec((1, HEAD_DIM), par),
            pl.BlockSpec((1, HEAD_DIM), par),
            pl.BlockSpec((1, hd), par),
        ],
        out_specs=pl.BlockSpec((tq, hd), qmap),
        out_shape=jax.ShapeDtypeStruct((T, D_DIFF), BF16),
        scratch_shapes=[
            pltpu.VMEM((2 * tq, hd), BF16),
            pltpu.VMEM((2 * tq, LANES), F32),
            pltpu.VMEM((2 * tq, LANES), F32),
            pltpu.VMEM((2 * tq, hd), F32),
            pltpu.VMEM((2 * tq, tq), F32),
            pltpu.VMEM((2 * tq, tq), F32),
            pltpu.VMEM((2 * tq, tq), BF16),
            pltpu.VMEM((2 * tq, tq), BF16),
            pltpu.VMEM((2 * tq, LANES), F32),
            pltpu.VMEM((2 * tq, LANES), F32),
        ],
        compiler_params=_cparams(("parallel", "parallel", "arbitrary")),
        name="diff_attn",
    )(q, k, v, vec(lq1), vec(lk1), vec(lq2), vec(lk2), vec(subln_g))


def _proj_ln_kernel(yr_ref, yd_ref, w_ref, x_ref, g_ref, b_ref, o_ref):
    mix = jnp.dot(yr_ref[...], w_ref[:D_RWKV, :], preferred_element_type=F32)
    mix = mix + jnp.dot(yd_ref[...], w_ref[D_RWKV:, :], preferred_element_type=F32)
    o_ref[...] = _layer_norm(DEEPNORM_ALPHA * x_ref[...] + mix, g_ref[...], b_ref[...])


def _proj_ln(yr, yd, w_out, x2, g, b, *, tm):
    T = x2.shape[0]
    row = lambda i: (i, 0)
    full = lambda i: (0, 0)
    return pl.pallas_call(
        _proj_ln_kernel,
        grid=(T // tm,),
        in_specs=[
            pl.BlockSpec((tm, D_RWKV), row),
            pl.BlockSpec((tm, D_DIFF), row),
            pl.BlockSpec((D_RWKV + D_DIFF, D_MODEL), full),
            pl.BlockSpec((tm, D_MODEL), row),
            pl.BlockSpec((1, D_MODEL), full),
            pl.BlockSpec((1, D_MODEL), full),
        ],
        out_specs=pl.BlockSpec((tm, D_MODEL), row),
        out_shape=jax.ShapeDtypeStruct((T, D_MODEL), F32),
        compiler_params=_cparams(("parallel",)),
        name="ab_out_ln",
    )(yr, yd, w_out, x2, g.reshape(1, -1), b.reshape(1, -1))


def _gmlp_kernel(x_ref, win_ref, bin_ref, lng_ref, lnb_ref, ws_ref, bs_ref, wout_ref, g_ref, b_ref,
                 o_ref, gated_scr, *, tm):
    x = x_ref[...]
    h = jnp.dot(x.astype(BF16), win_ref[...], preferred_element_type=F32) + bin_ref[...]
    h = 0.5 * h * (1.0 + lax.erf(h * (0.5 ** 0.5)))
    u = h[:, :D_GMLP]
    v = _layer_norm(h[:, D_GMLP:], lng_ref[...], lnb_ref[...]).astype(BF16)
    row = lax.broadcasted_iota(jnp.int32, (CHUNK, CHUNK), 0)
    col = lax.broadcasted_iota(jnp.int32, (CHUNK, CHUNK), 1)
    gw = D_GMLP // GMLP_GROUPS
    for gi in range(GMLP_GROUPS):
        ws = jnp.where(row >= col, ws_ref[gi], 0.0).astype(BF16)
        for c in range(tm // CHUNK):
            rs = slice(c * CHUNK, (c + 1) * CHUNK)
            cs = slice(gi * gw, (gi + 1) * gw)
            mixed = jnp.dot(ws, v[rs, cs], preferred_element_type=F32) + bs_ref[gi]
            gated_scr[rs, cs] = (u[rs, cs] * mixed).astype(BF16)
    mix = jnp.dot(gated_scr[...], wout_ref[...], preferred_element_type=F32)
    o_ref[...] = _layer_norm(DEEPNORM_ALPHA * x + mix, g_ref[...], b_ref[...])


def _gmlp(x2, w_in, b_in, ln_g, ln_b, w_s, b_s, w_out, g, b, *, tm):
    T = x2.shape[0]
    row = lambda i: (i, 0)
    full = lambda i: (0, 0)
    full3 = lambda i: (0, 0, 0)
    vec = lambda a: a.reshape(1, -1)
    gw = D_GMLP // GMLP_GROUPS
    bs_b = jnp.broadcast_to(b_s[:, :, None], (GMLP_GROUPS, CHUNK, gw))
    return pl.pallas_call(
        functools.partial(_gmlp_kernel, tm=tm),
        grid=(T // tm,),
        in_specs=[
            pl.BlockSpec((tm, D_MODEL), row),
            pl.BlockSpec((D_MODEL, 2 * D_GMLP), full),
            pl.BlockSpec((1, 2 * D_GMLP), full),
            pl.BlockSpec((1, D_GMLP), full),
            pl.BlockSpec((1, D_GMLP), full),
            pl.BlockSpec((GMLP_GROUPS, CHUNK, CHUNK), full3),
            pl.BlockSpec((GMLP_GROUPS, CHUNK, gw), full3),
            pl.BlockSpec((D_GMLP, D_MODEL), full),
            pl.BlockSpec((1, D_MODEL), full),
            pl.BlockSpec((1, D_MODEL), full),
        ],
        out_specs=pl.BlockSpec((tm, D_MODEL), row),
        out_shape=jax.ShapeDtypeStruct((T, D_MODEL), F32),
        scratch_shapes=[pltpu.VMEM((tm, D_GMLP), BF16)],
        compiler_params=_cparams(("parallel",)),
        name="gmlp",
    )(x2, w_in, vec(b_in), vec(ln_g), vec(ln_b), w_s, bs_b, w_out, vec(g), vec(b))


def _ffn_kernel(x_ref, xh_ref, wup_ref, cw_ref, cb_ref, wd_ref, g_ref, b_ref, o_ref,
                *, blocks_per_seq, tf, n_slabs):
    x = x_ref[...]
    xb = x.astype(BF16)
    xhb = xh_ref[...].astype(BF16)
    seq_start = pl.program_id(0) % blocks_per_seq == 0

    def up(j):
        wup = wup_ref[j]
        gv = jnp.dot(xb, wup, preferred_element_type=F32)
        gh = jnp.dot(xhb, wup[:, :tf], preferred_element_type=F32)
        return gv, jnp.where(seq_start, 0.0, gh)

    acc = None
    nxt = up(0)
    for j in range(n_slabs):
        gv, gh = nxt
        if j + 1 < n_slabs:
            nxt = up(j + 1)
        gate = gv[:, :tf]
        val = gv[:, tf:]
        cw = cw_ref[j]
        conv = (cb_ref[j] + cw[0:1, :] * _shift_rows(gate, 2, gh)
                + cw[1:2, :] * _shift_rows(gate, 1, gh) + cw[2:3, :] * gate)
        hid = conv * jax.nn.sigmoid(conv) * val
        down = jnp.dot(hid.astype(BF16), wd_ref[j], preferred_element_type=F32)
        acc = down if acc is None else acc + down
    o_ref[...] = _layer_norm(DEEPNORM_ALPHA * x + acc, g_ref[...], b_ref[...])


def _ffn(x2, w_up, conv_w, conv_b, w_down, g, b, *, seq, tm, tf):
    T = x2.shape[0]
    n_slabs = D_FF // tf
    bps = seq // tm
    wg = w_up[:, :D_FF].reshape(D_MODEL, n_slabs, tf)
    wv = w_up[:, D_FF:].reshape(D_MODEL, n_slabs, tf)
    wup = jnp.transpose(jnp.concatenate([wg, wv], axis=2), (1, 0, 2)).astype(BF16)
    cw = jnp.transpose(conv_w.reshape(3, n_slabs, tf), (1, 0, 2))
    cb = conv_b.reshape(n_slabs, 1, tf)
    wd = w_down.reshape(n_slabs, tf, D_MODEL).astype(BF16)
    row = lambda i: (i, 0)
    full = lambda i: (0, 0)
    full3 = lambda i: (0, 0, 0)
    halo = lambda i: (jnp.maximum(i * (tm // HALO_ROWS) - 1, 0), 0)
    once = pl.Buffered(1)
    return pl.pallas_call(
        functools.partial(_ffn_kernel, blocks_per_seq=bps, tf=tf, n_slabs=n_slabs),
        grid=(T // tm,),
        in_specs=[
            pl.BlockSpec((tm, D_MODEL), row),
            pl.BlockSpec((HALO_ROWS, D_MODEL), halo),
            pl.BlockSpec((n_slabs, D_MODEL, 2 * tf), full3, pipeline_mode=once),
            pl.BlockSpec((n_slabs, 3, tf), full3, pipeline_mode=once),
            pl.BlockSpec((n_slabs, 1, tf), full3, pipeline_mode=once),
            pl.BlockSpec((n_slabs, tf, D_MODEL), full3, pipeline_mode=once),
            pl.BlockSpec((1, D_MODEL), full),
            pl.BlockSpec((1, D_MODEL), full),
        ],
        out_specs=pl.BlockSpec((tm, D_MODEL), row),
        out_shape=jax.ShapeDtypeStruct((T, D_MODEL), F32),
        compiler_params=_cparams(("parallel",)),
        name="conv_ffn",
    )(x2, x2, wup, cw, cb, wd, g.reshape(1, -1), b.reshape(1, -1))


def _rope_tables(seq):
    half = ROPE_DIM // 2
    inv_freq = ROPE_THETA ** (-jnp.arange(0, ROPE_DIM, 2, dtype=F32) / ROPE_DIM)
    ang = jnp.arange(seq, dtype=F32)[:, None] * inv_freq[None, :]
    cos, sin = jnp.cos(ang), jnp.sin(ang)
    ones = jnp.ones((seq, HEAD_DIM - ROPE_DIM), F32)
    zeros = jnp.zeros((seq, HEAD_DIM - ROPE_DIM), F32)
    zh = jnp.zeros((seq, half), F32)
    c64 = jnp.concatenate([cos, cos, ones], axis=1)
    s1_64 = jnp.concatenate([zh, sin, zeros], axis=1)
    s2_64 = jnp.concatenate([-sin, zh, zeros], axis=1)
    two = lambda t: jnp.concatenate([t, t], axis=1)
    return two(c64), two(s1_64), two(s2_64)


def kernel(x, ab_w_in, ab_shift_mu, ab_w0, ab_w2, ab_a0, ab_a2, ab_g2, ab_k_k, ab_k_a, ab_r_k, ab_lnx_g, ab_lnx_b, ab_lam_q1, ab_lam_k1, ab_lam_q2, ab_lam_k2, ab_subln_g, ab_w_out, c_w_in, c_b_in, c_ln_g, c_ln_b, c_w_s, c_b_s, c_w_out, ln1_g, ln1_b, ffn_w_up, ffn_conv_w, ffn_conv_b, ffn_w_down, ln2_g, ln2_b):
    batch, seq, _ = x.shape
    x2 = x.reshape(batch * seq, D_MODEL)
    rc, rs1, rs2 = _rope_tables(seq)
    tm_in = min(512, seq)
    tm_ffn = min(512, seq)
    tm_gmlp = min(512, seq)
    tq = min(512, seq)
    wkv_rows = min(256, seq)
    for i in range(DEPTH):
        j = i // 2
        if i % 2 == 0:
            (r, k, v, lw, kn, a, g, qd, kd, vd) = _ab_in(
                x2, ab_w_in[j].astype(BF16), ab_shift_mu[j], ab_w0[j], ab_w2[j], ab_a0[j],
                ab_a2[j].astype(BF16), ab_g2[j].astype(BF16), ab_k_k[j], ab_k_a[j], rc, rs1, rs2,
                seq=seq, tm=tm_in)
            y_r = _wkv(r, k, v, lw, kn, a, g, ab_r_k[j], ab_lnx_g[j], ab_lnx_b[j],
                       batch=batch, seq=seq, rows=wkv_rows)
            lam_init = 0.8 - 0.6 * math.exp(-0.3 * i)
            y_d = _dattn(qd, kd, vd, ab_lam_q1[j], ab_lam_k1[j], ab_lam_q2[j], ab_lam_k2[j],
                         ab_subln_g[j], batch=batch, seq=seq, tq=tq, lam_init=lam_init)
            x2 = _proj_ln(y_r, y_d, ab_w_out[j].astype(BF16), x2, ln1_g[i], ln1_b[i], tm=tm_in)
        else:
            x2 = _gmlp(x2, c_w_in[j].astype(BF16), c_b_in[j], c_ln_g[j], c_ln_b[j], c_w_s[j],
                       c_b_s[j], c_w_out[j].astype(BF16), ln1_g[i], ln1_b[i], tm=tm_gmlp)
        x2 = _ffn(x2, ffn_w_up[i], ffn_conv_w[i], ffn_conv_b[i], ffn_w_down[i], ln2_g[i], ln2_b[i],
                  seq=seq, tm=tm_ffn, tf=256)
    return x2.reshape(batch, seq, D_MODEL)
```

```python
import functools
import math

import jax
import jax.numpy as jnp
from jax import lax
from jax.experimental import pallas as pl
from jax.experimental.pallas import tpu as pltpu

F32 = jnp.float32
BF16 = jnp.bfloat16
HIGHEST = lax.Precision.HIGHEST

D_MODEL = 1024
HEAD_DIM = 64
N_RWKV_HEADS = 8
D_RWKV = N_RWKV_HEADS * HEAD_DIM
N_DIFF_HEADS = 4
D_DIFF = N_DIFF_HEADS * 2 * HEAD_DIM
DECAY_LORA = 64
AAA_LORA = 64
GATE_LORA = 128
RWKV_COLS = 3 * D_RWKV + DECAY_LORA + AAA_LORA + GATE_LORA
AB_COLS = RWKV_COLS + 3 * D_DIFF
RWKV_GN_EPS = 64e-5
ROPE_THETA = 500000.0
ROPE_DIM = HEAD_DIM // 4
CHUNK = 128
GMLP_GROUPS = 8
D_GMLP = D_MODEL
D_FF = 2816
DEPTH = 4
DEEPNORM_ALPHA = (2 * DEPTH) ** 0.25

LANES = 128
HALO_ROWS = 8
WKV_CHUNK = 64
VMEM_LIMIT = 56 * 1024 * 1024
LOG2E = math.log2(math.e)


def _cparams(sem):
    return pltpu.CompilerParams(dimension_semantics=sem, vmem_limit_bytes=VMEM_LIMIT)


def _dot(a, b):
    return jnp.dot(a.astype(BF16), b.astype(BF16), preferred_element_type=F32)


def _dot_nt(a, b):
    return lax.dot_general(a.astype(BF16), b.astype(BF16), (((1,), (1,)), ((), ())),
                           preferred_element_type=F32)


def _layer_norm(z, g, b, eps=1e-5):
    mu = jnp.mean(z, axis=-1, keepdims=True)
    d = z - mu
    var = jnp.mean(d * d, axis=-1, keepdims=True)
    return d * lax.rsqrt(var + eps) * g + b


def _shift_rows(t, k, halo):
    rolled = pltpu.roll(t, k, 0)
    row = lax.broadcasted_iota(jnp.int32, t.shape, 0)
    out = rolled
    for j in range(k):
        out = jnp.where(row == j, halo[HALO_ROWS - k + j:HALO_ROWS - k + j + 1, :], out)
    return out


def _ab_in_kernel(x_ref, xh_ref, w_ref, mu_ref, w0_ref, w2_ref, a0_ref, a2_ref, g2_ref, kk_ref,
                  ka_ref, rc_ref, rs1_ref, rs2_ref,
                  r_out, k_out, v_out, lw_out, kn_out, a_out, g_out, q_out, kd_out, vd_out,
                  *, blocks_per_seq):
    i = pl.program_id(0)
    xb = x_ref[...].astype(BF16)
    p = jnp.dot(xb, w_ref[...], preferred_element_type=F32)
    ph = jnp.dot(xh_ref[...].astype(BF16), w_ref[:, :RWKV_COLS], preferred_element_type=F32)
    ph = jnp.where(i % blocks_per_seq == 0, 0.0, ph)
    pr = p[:, :RWKV_COLS]
    xs = pr + mu_ref[...] * (_shift_rows(pr, 1, ph) - pr)

    r = xs[:, :D_RWKV]
    k = xs[:, D_RWKV:2 * D_RWKV]
    v = xs[:, 2 * D_RWKV:3 * D_RWKV]
    o = 3 * D_RWKV
    xw = xs[:, o:o + DECAY_LORA]
    xa = xs[:, o + DECAY_LORA:o + DECAY_LORA + AAA_LORA]
    xg = xs[:, o + DECAY_LORA + AAA_LORA:]

    z = w0_ref[...] + jnp.dot(jnp.tanh(xw), w2_ref[...], precision=HIGHEST,
                              preferred_element_type=F32)
    softplus_neg = jnp.maximum(-z, 0.0) + jnp.log1p(jnp.exp(-jnp.abs(z)))
    w = -softplus_neg - 0.5
    lw_out[...] = -jnp.exp(w)
    a = jax.nn.sigmoid(a0_ref[...] + _dot(xa, a2_ref[...]))
    a_out[...] = a
    g_out[...] = _dot(jax.nn.sigmoid(xg), g2_ref[...])

    kx = k * kk_ref[...]
    sq = kx * kx
    lane = lax.broadcasted_iota(jnp.int32, (sq.shape[0], LANES), 1)
    lo = lane < HEAD_DIM
    for c in range(D_RWKV // LANES):
        blk = sq[:, c * LANES:(c + 1) * LANES]
        n_lo = jnp.sqrt(jnp.sum(jnp.where(lo, blk, 0.0), axis=-1, keepdims=True))
        n_hi = jnp.sqrt(jnp.sum(jnp.where(lo, 0.0, blk), axis=-1, keepdims=True))
        norm = jnp.maximum(jnp.where(lo, n_lo, n_hi), 1e-12)
        kn_out[:, c * LANES:(c + 1) * LANES] = kx[:, c * LANES:(c + 1) * LANES] / norm
    r_out[...] = r
    k_out[...] = k * (1.0 + (a - 1.0) * ka_ref[...])
    v_out[...] = v

    pd = p[:, RWKV_COLS:]
    rc = jnp.concatenate([rc_ref[...]] * (D_DIFF // LANES), axis=1)
    rs1 = jnp.concatenate([rs1_ref[...]] * (D_DIFF // LANES), axis=1)
    rs2 = jnp.concatenate([rs2_ref[...]] * (D_DIFF // LANES), axis=1)
    half = ROPE_DIM // 2

    def rope(t):
        return t * rc + pltpu.roll(t, half, 1) * rs1 + pltpu.roll(t, D_DIFF - half, 1) * rs2

    q_out[...] = (rope(pd[:, :D_DIFF]) * (HEAD_DIM ** -0.5 * LOG2E)).astype(BF16)
    kd_out[...] = rope(pd[:, D_DIFF:2 * D_DIFF]).astype(BF16)
    vd_out[...] = pd[:, 2 * D_DIFF:].astype(BF16)


def _ab_in(x2, w_in, mu, w0, w2, a0, a2, g2, k_k, k_a, rc, rs1, rs2, *, seq, tm):
    T = x2.shape[0]
    n = T // tm
    bps = seq // tm
    row = lambda i: (i, 0)
    full = lambda i: (0, 0)
    halo = lambda i: (jnp.maximum(i * (tm // HALO_ROWS) - 1, 0), 0)
    rope_map = lambda i: (i % bps, 0)
    vec = lambda a: a.reshape(1, -1)
    f32_out = jax.ShapeDtypeStruct((T, D_RWKV), F32)
    bf_out = jax.ShapeDtypeStruct((T, D_DIFF), BF16)
    out_spec = pl.BlockSpec((tm, D_RWKV), row)
    return pl.pallas_call(
        functools.partial(_ab_in_kernel, blocks_per_seq=bps),
        grid=(n,),
        in_specs=[
            pl.BlockSpec((tm, D_MODEL), row),
            pl.BlockSpec((HALO_ROWS, D_MODEL), halo),
            pl.BlockSpec((D_MODEL, AB_COLS), full),
            pl.BlockSpec((1, RWKV_COLS), full),
            pl.BlockSpec((1, D_RWKV), full),
            pl.BlockSpec((DECAY_LORA, D_RWKV), full),
            pl.BlockSpec((1, D_RWKV), full),
            pl.BlockSpec((AAA_LORA, D_RWKV), full),
            pl.BlockSpec((GATE_LORA, D_RWKV), full),
            pl.BlockSpec((1, D_RWKV), full),
            pl.BlockSpec((1, D_RWKV), full),
            pl.BlockSpec((tm, LANES), rope_map),
            pl.BlockSpec((tm, LANES), rope_map),
            pl.BlockSpec((tm, LANES), rope_map),
        ],
        out_specs=[out_spec] * 10,
        out_shape=[f32_out] * 7 + [bf_out] * 3,
        compiler_params=_cparams(("parallel",)),
        name="ab_in",
    )(x2, x2, w_in, vec(mu), vec(w0), w2, vec(a0), a2, g2, vec(k_k), vec(k_a), rc, rs1, rs2)


WKV_GROUP = 4
GW = WKV_GROUP * HEAD_DIM


def _wkv_masks():
    c = WKV_CHUNK
    row = lax.broadcasted_iota(jnp.int32, (c, GW), 0)
    col = lax.broadcasted_iota(jnp.int32, (c, GW), 1) % HEAD_DIM
    brow = lax.broadcasted_iota(jnp.int32, (GW, GW), 0) // HEAD_DIM
    bcol = lax.broadcasted_iota(jnp.int32, (GW, GW), 1) // HEAD_DIM
    bd = jnp.where(brow == bcol, 1.0, 0.0).astype(BF16)
    brow2 = lax.broadcasted_iota(jnp.int32, (2 * c * WKV_GROUP, GW), 0) // (2 * c)
    bcol2 = lax.broadcasted_iota(jnp.int32, (2 * c * WKV_GROUP, GW), 1) // HEAD_DIM
    bd2 = jnp.where(brow2 == bcol2, 1.0, 0.0).astype(BF16)
    strict_incl = jnp.concatenate([row > col, row >= col], axis=0)
    return dict(row=row, col=col, bd=bd, bd2=bd2, strict_incl=strict_incl)


def _chunk_cumsum(lw, n_chunks):
    n = n_chunks * WKV_CHUNK
    r = lax.broadcasted_iota(jnp.int32, (n, n), 0)
    c = lax.broadcasted_iota(jnp.int32, (n, n), 1)
    tri = jnp.where(jnp.logical_and(r // WKV_CHUNK == c // WKV_CHUNK, r >= c), 1.0, 0.0).astype(BF16)
    h1 = lw.astype(BF16)
    rem = lw - h1.astype(F32)
    h2 = rem.astype(BF16)
    h3 = (rem - h2.astype(F32)).astype(BF16)
    cum = jnp.dot(tri, h1, preferred_element_type=F32)
    cum = cum + jnp.dot(tri, h2, preferred_element_type=F32)
    return cum + jnp.dot(tri, h3, preferred_element_type=F32)


def _bd(w, m):
    return jnp.concatenate([w.astype(BF16)] * WKV_GROUP, axis=0) * m["bd"]


def _mm(x, w_bd):
    return jnp.dot(x.astype(BF16), w_bd, preferred_element_type=F32)


def _mm_nt(x, w_bd):
    return lax.dot_general(x.astype(BF16), w_bd, (((1,), (1,)), ((), ())),
                           preferred_element_type=F32)


def _seg_sum(x, m):
    hi = x.astype(BF16)
    lo = (x - hi.astype(F32)).astype(BF16)
    s = jnp.dot(jnp.concatenate([hi, lo], axis=0), m["bd"], preferred_element_type=F32)
    return s[:x.shape[0]] + s[x.shape[0]:]


def _unit_lower_inverse(n_mats, m):
    row, col = m["row"], m["col"]
    eye = jnp.where(row == col, 1.0, 0.0)
    same16 = (row // 16) == (col // 16)
    c = WKV_CHUNK
    pws = [jnp.where(same16, n, 0.0) for n in n_mats]
    ts = [eye + p for p in pws]
    pws = [_mm(p, _bd(p, m)) for p in pws]
    for lvl in range(3):
        pw_bds = [_bd(p, m) for p in pws]
        if lvl < 2:
            both = [_mm(jnp.concatenate([t, p], axis=0), b) for t, p, b in zip(ts, pws, pw_bds)]
            ts = [t + bo[:c] for t, bo in zip(ts, both)]
            pws = [bo[c:] for bo in both]
        else:
            ts = [t + _mm(t, b) for t, b in zip(ts, pw_bds)]
    blk = 16
    while blk < WKV_CHUNK:
        same_lo = (row // blk) == (col // blk)
        same_hi = (row // (2 * blk)) == (col // (2 * blk))
        sel = jnp.logical_and(same_hi, jnp.logical_not(same_lo))
        halves = [_mm(t, _bd(jnp.where(sel, n, 0.0), m)) for t, n in zip(ts, n_mats)]
        ts = [t + _mm(h, _bd(t, m)) for t, h in zip(ts, halves)]
        blk *= 2
    return ts


def _wkv_prepare(r, k, v, lw, cum, kn, a, m):
    c = r.shape[0]
    mid = cum[c // 2 - 1:c // 2, :]
    end = cum[c - 1:c, :]
    e_abs = jnp.exp(cum)
    e_abs_prev = jnp.exp(cum - lw)
    e_mid = jnp.exp(-mid)
    e_neg = jnp.exp(mid - cum)
    e_end = jnp.exp(end - cum)
    g_end = jnp.exp(end)
    bvec = kn * a
    r0 = r * e_abs
    a0 = -kn * e_abs_prev
    rt = r0 * e_mid
    at = a0 * e_mid
    kt = k * e_neg
    bt = bvec * e_neg
    kh = k * e_end
    bh = bvec * e_end
    out = []
    for gi in range(N_RWKV_HEADS // WKV_GROUP):
        sl = slice(gi * GW, (gi + 1) * GW)
        lhs = jnp.concatenate([at[:, sl], rt[:, sl]], axis=0)
        ab = jnp.where(m["strict_incl"], _mm_nt(lhs, _bd(bt[:, sl], m)), 0.0)
        ak = jnp.where(m["strict_incl"], _mm_nt(lhs, _bd(kt[:, sl], m)), 0.0)
        bk_t = jnp.transpose(jnp.concatenate([bh[:, sl], kh[:, sl]], axis=0))
        bk_t = jnp.concatenate([bk_t[hh * HEAD_DIM:(hh + 1) * HEAD_DIM] for hh in range(WKV_GROUP)],
                               axis=1).astype(BF16)
        g_diag = jnp.where(m["row"] == m["col"], g_end[:, sl], 0.0)
        out.append(dict(a_ab=ab[:c], a_rb=ab[c:], ak=ak, vg=v[:, sl], a0=a0[:, sl], r0=r0[:, sl],
                        g_diag=g_diag, bk_t=bk_t))
    return out


def _wkv_kernel(r_ref, k_ref, v_ref, lw_ref, kn_ref, a_ref, g_ref, rk_ref, lg_ref, lb_ref,
                o_ref, s_scr, *, n_chunks):
    @pl.when(pl.program_id(1) == 0)
    def _():
        s_scr[...] = jnp.zeros_like(s_scr)

    n_groups = N_RWKV_HEADS // WKV_GROUP
    m = _wkv_masks()
    chunk_rows = [slice(ci * WKV_CHUNK, (ci + 1) * WKV_CHUNK) for ci in range(n_chunks)]

    c = WKV_CHUNK
    lw_all = lw_ref[...]
    cum_all = _chunk_cumsum(lw_all, n_chunks)
    probs = []
    for rows in chunk_rows:
        probs += _wkv_prepare(r_ref[rows, :], k_ref[rows, :], v_ref[rows, :], lw_all[rows, :],
                              cum_all[rows, :], kn_ref[rows, :], a_ref[rows, :], m)
    n_probs = len(probs)
    split = lambda stacked: [stacked[i * c:(i + 1) * c] for i in range(n_probs)]
    ts = _unit_lower_inverse([p["a_ab"] for p in probs], m)
    vg_bds = [_bd(p["vg"], m) for p in probs]
    akv_yvs = [_mm(p["ak"], vb) for p, vb in zip(probs, vg_bds)]
    t_bfs = [t.astype(BF16) for t in ts]
    xas = [_mm(t, _bd(p["a0"], m)) for t, p in zip(t_bfs, probs)]
    xvs = [_mm(t, _bd(ay[:c], m)) for t, ay in zip(t_bfs, akv_yvs)]
    xrs = [jnp.concatenate([xa, p["r0"]], axis=0).astype(BF16) for xa, p in zip(xas, probs)]
    decays = split(_seg_sum(jnp.concatenate([p["g_diag"] for p in probs], axis=0), m))
    rkr = r_ref[...] * k_ref[...] * rk_ref[...]
    bonus_w = split(_seg_sum(jnp.concatenate(
        [rkr[rows, gi * GW:(gi + 1) * GW] for rows in chunk_rows for gi in range(n_groups)], axis=0), m))

    states = [s_scr[gi] for gi in range(n_groups)]
    ys = []
    for ci in range(n_chunks):
        for gi in range(n_groups):
            i = ci * n_groups + gi
            p = probs[i]
            s0 = states[gi]
            uy = _mm(xrs[i], _bd(s0, m))
            u = uy[:c] + xvs[i]
            ys.append(uy[c:] + _mm(p["a_rb"], _bd(u, m)) + akv_yvs[i][c:])
            uv = jnp.concatenate([u, p["vg"]], axis=0).astype(BF16)
            uv_bd = jnp.concatenate([uv] * WKV_GROUP, axis=0) * m["bd2"]
            states[gi] = s0 * decays[i] + jnp.dot(p["bk_t"], uv_bd, preferred_element_type=F32)
    for gi in range(n_groups):
        s_scr[gi] = states[gi]

    y_all = jnp.concatenate(ys, axis=0)
    d_all = y_all - _seg_sum(y_all, m) * (1.0 / HEAD_DIM)
    inv_all = lax.rsqrt(_seg_sum(d_all * d_all, m) * (1.0 / HEAD_DIM) + RWKV_GN_EPS)
    for ci, rows in enumerate(chunk_rows):
        for gi in range(n_groups):
            i = ci * n_groups + gi
            sl = slice(gi * GW, (gi + 1) * GW)
            yn = d_all[i * c:(i + 1) * c] * inv_all[i * c:(i + 1) * c] * lg_ref[:, sl] + lb_ref[:, sl]
            o_ref[rows, sl] = ((yn + bonus_w[i] * probs[i]["vg"]) * g_ref[rows, sl]).astype(BF16)


def _wkv(r, k, v, lw, kn, a, g, r_k, lnx_g, lnx_b, *, batch, seq, rows):
    T = r.shape[0]
    nb = seq // rows
    blk = pl.BlockSpec((rows, D_RWKV), lambda b, i: (b * nb + i, 0))
    par = pl.BlockSpec((1, D_RWKV), lambda b, i: (0, 0))
    return pl.pallas_call(
        functools.partial(_wkv_kernel, n_chunks=rows // WKV_CHUNK),
        grid=(batch, nb),
        in_specs=[blk] * 7 + [par] * 3,
        out_specs=blk,
        out_shape=jax.ShapeDtypeStruct((T, D_RWKV), BF16),
        scratch_shapes=[pltpu.VMEM((N_RWKV_HEADS // WKV_GROUP, HEAD_DIM, GW), F32)],
        compiler_params=_cparams(("parallel", "arbitrary")),
        name="wkv7",
    )(r, k, v, lw, kn, a, g, r_k.reshape(1, -1), lnx_g.reshape(1, -1), lnx_b.reshape(1, -1))


def _dattn_kernel(q_ref, k_ref, v_ref, lq1_ref, lk1_ref, lq2_ref, lk2_ref, sg_ref, o_ref,
                  qs_scr, m_scr, l_scr, acc_scr, s0_scr, s1_scr, p0_scr, p1_scr, al0_scr, al1_scr,
                  *, tq, lam_init):
    qi = pl.program_id(2)
    q = q_ref[...]
    lane = lax.broadcasted_iota(jnp.int32, q.shape, 1)
    zero = jnp.zeros_like(q)
    qs_scr[:tq, :] = jnp.where(lane < HEAD_DIM, q, zero)
    qs_scr[tq:, :] = jnp.where(lane < HEAD_DIM, zero, q)
    m_scr[...] = jnp.full_like(m_scr, -jnp.inf)
    l_scr[...] = jnp.zeros_like(l_scr)
    acc_scr[...] = jnp.zeros_like(acc_scr)

    bufs = ((s0_scr, p0_scr, al0_scr), (s1_scr, p1_scr, al1_scr))

    def scores(j, par):
        start = pl.multiple_of(j * tq, tq)
        bufs[par][0][...] = _dot_nt(k_ref[pl.ds(start, tq), :], qs_scr[...])

    def softmax(par, masked):
        s_ref, p_ref, al_ref = bufs[par]
        s = s_ref[...]
        if masked:
            k_pos = lax.broadcasted_iota(jnp.int32, s.shape, 0)
            q_pos = lax.broadcasted_iota(jnp.int32, s.shape, 1) % tq
            s = jnp.where(k_pos <= q_pos, s, -jnp.inf)
        m_prev = m_scr[...]
        m_new = jnp.maximum(m_prev, jnp.max(s, axis=0, keepdims=True))
        alpha = jnp.exp2(m_prev - m_new)
        p = jnp.exp2(s - m_new)
        l_scr[...] = alpha * l_scr[...] + jnp.sum(p, axis=0, keepdims=True)
        m_scr[...] = m_new
        p_ref[...] = p.astype(BF16)
        al_ref[...] = alpha

    def values(j, par):
        _, p_ref, al_ref = bufs[par]
        start = pl.multiple_of(j * tq, tq)
        pv = lax.dot_general(v_ref[pl.ds(start, tq), :], p_ref[...], (((0,), (0,)), ((), ())),
                             preferred_element_type=F32)
        acc_scr[...] = al_ref[...] * acc_scr[...] + pv

    scores(0, 0)

    @pl.when(qi == 0)
    def _():
        softmax(0, True)
        values(0, 0)

    @pl.when(qi > 0)
    def _():
        scores(1, 1)
        softmax(0, False)

        def body(i, carry):
            t = 2 + 2 * i
            values(t - 2, 0)
            scores(t, 0)
            softmax(1, False)
            values(t - 1, 1)
            scores(t + 1, 1)
            softmax(0, False)
            return carry

        lax.fori_loop(0, (qi - 1) // 2, body, 0)

        @pl.when(qi % 2 == 1)
        def _():
            values(qi - 1, 0)
            softmax(1, True)
            values(qi, 1)

        @pl.when(qi % 2 == 0)
        def _():
            values(qi - 2, 0)
            scores(qi, 0)
            softmax(1, False)
            values(qi - 1, 1)
            softmax(0, True)
            values(qi, 0)

    lam = (jnp.exp(jnp.sum(lq1_ref[...] * lk1_ref[...], axis=-1, keepdims=True))
           - jnp.exp(jnp.sum(lq2_ref[...] * lk2_ref[...], axis=-1, keepdims=True)) + lam_init)
    on = acc_scr[...] / l_scr[...]
    o = jnp.transpose(on[:, :tq] - lam * on[:, tq:])
    o = o * lax.rsqrt(jnp.mean(o * o, axis=-1, keepdims=True) + 1e-5) * sg_ref[...]
    o_ref[...] = (o * (1.0 - lam_init)).astype(BF16)


def _dattn(q, k, v, lq1, lk1, lq2, lk2, subln_g, *, batch, seq, tq, lam_init):
    T = q.shape[0]
    nq = seq // tq
    hd = 2 * HEAD_DIM
    qmap = lambda b, h, i: (b * nq + i, h)
    kvmap = lambda b, h, i: (b, h)
    par = lambda b, h, i: (0, 0)
    vec = lambda a: a.reshape(1, -1)
    return pl.pallas_call(
        functools.partial(_dattn_kernel, tq=tq, lam_init=lam_init),
        grid=(batch, N_DIFF_HEADS, nq),
        in_specs=[
            pl.BlockSpec((tq, hd), qmap),
            pl.BlockSpec((seq, hd), kvmap),
            pl.BlockSpec((seq, hd), kvmap),
            pl.BlockSpec((1, HEAD_DIM), par),
            pl.BlockSpec((1, HEAD_DIM), par),
            pl.BlockSpec((1, HEAD_DIM), par),
            pl.BlockSpec((1, HEAD_DIM), par),
            pl.BlockSpec((1, hd), par),
        ],
        out_specs=pl.BlockSpec((tq, hd), qmap),
        out_shape=jax.ShapeDtypeStruct((T, D_DIFF), BF16),
        scratch_shapes=[
            pltpu.VMEM((2 * tq, hd), BF16),
            pltpu.VMEM((1, 2 * tq), F32),
            pltpu.VMEM((1, 2 * tq), F32),
            pltpu.VMEM((hd, 2 * tq), F32),
            pltpu.VMEM((tq, 2 * tq), F32),
            pltpu.VMEM((tq, 2 * tq), F32),
            pltpu.VMEM((tq, 2 * tq), BF16),
            pltpu.VMEM((tq, 2 * tq), BF16),
            pltpu.VMEM((1, 2 * tq), F32),
            pltpu.VMEM((1, 2 * tq), F32),
        ],
        compiler_params=_cparams(("parallel", "parallel", "arbitrary")),
        name="diff_attn",
    )(q, k, v, vec(lq1), vec(lk1), vec(lq2), vec(lk2), vec(subln_g))


def _proj_ln_kernel(yr_ref, yd_ref, w_ref, x_ref, g_ref, b_ref, o_ref):
    mix = jnp.dot(yr_ref[...], w_ref[:D_RWKV, :], preferred_element_type=F32)
    mix = mix + jnp.dot(yd_ref[...], w_ref[D_RWKV:, :], preferred_element_type=F32)
    o_ref[...] = _layer_norm(DEEPNORM_ALPHA * x_ref[...] + mix, g_ref[...], b_ref[...])


def _proj_ln(yr, yd, w_out, x2, g, b, *, tm):
    T = x2.shape[0]
    row = lambda i: (i, 0)
    full = lambda i: (0, 0)
    return pl.pallas_call(
        _proj_ln_kernel,
        grid=(T // tm,),
        in_specs=[
            pl.BlockSpec((tm, D_RWKV), row),
            pl.BlockSpec((tm, D_DIFF), row),
            pl.BlockSpec((D_RWKV + D_DIFF, D_MODEL), full),
            pl.BlockSpec((tm, D_MODEL), row),
            pl.BlockSpec((1, D_MODEL), full),
            pl.BlockSpec((1, D_MODEL), full),
        ],
        out_specs=pl.BlockSpec((tm, D_MODEL), row),
        out_shape=jax.ShapeDtypeStruct((T, D_MODEL), F32),
        compiler_params=_cparams(("parallel",)),
        name="ab_out_ln",
    )(yr, yd, w_out, x2, g.reshape(1, -1), b.reshape(1, -1))


def _gmlp_kernel(x_ref, win_ref, bin_ref, lng_ref, lnb_ref, ws_ref, bs_ref, wout_ref, g_ref, b_ref,
                 o_ref, gated_scr, *, tm):
    x = x_ref[...]
    h = jnp.dot(x.astype(BF16), win_ref[...], preferred_element_type=F32) + bin_ref[...]
    h = 0.5 * h * (1.0 + lax.erf(h * (0.5 ** 0.5)))
    u = h[:, :D_GMLP]
    v = _layer_norm(h[:, D_GMLP:], lng_ref[...], lnb_ref[...]).astype(BF16)
    row = lax.broadcasted_iota(jnp.int32, (CHUNK, CHUNK), 0)
    col = lax.broadcasted_iota(jnp.int32, (CHUNK, CHUNK), 1)
    gw = D_GMLP // GMLP_GROUPS
    for gi in range(GMLP_GROUPS):
        ws = jnp.where(row >= col, ws_ref[gi], 0.0).astype(BF16)
        for c in range(tm // CHUNK):
            rs = slice(c * CHUNK, (c + 1) * CHUNK)
            cs = slice(gi * gw, (gi + 1) * gw)
            mixed = jnp.dot(ws, v[rs, cs], preferred_element_type=F32) + bs_ref[gi]
            gated_scr[rs, cs] = (u[rs, cs] * mixed).astype(BF16)
    mix = jnp.dot(gated_scr[...], wout_ref[...], preferred_element_type=F32)
    o_ref[...] = _layer_norm(DEEPNORM_ALPHA * x + mix, g_ref[...], b_ref[...])


def _gmlp(x2, w_in, b_in, ln_g, ln_b, w_s, b_s, w_out, g, b, *, tm):
    T = x2.shape[0]
    row = lambda i: (i, 0)
    full = lambda i: (0, 0)
    full3 = lambda i: (0, 0, 0)
    vec = lambda a: a.reshape(1, -1)
    gw = D_GMLP // GMLP_GROUPS
    bs_b = jnp.broadcast_to(b_s[:, :, None], (GMLP_GROUPS, CHUNK, gw))
    return pl.pallas_call(
        functools.partial(_gmlp_kernel, tm=tm),
        grid=(T // tm,),
        in_specs=[
            pl.BlockSpec((tm, D_MODEL), row),
            pl.BlockSpec((D_MODEL, 2 * D_GMLP), full),
            pl.BlockSpec((1, 2 * D_GMLP), full),
            pl.BlockSpec((1, D_GMLP), full),
            pl.BlockSpec((1, D_GMLP), full),
            pl.BlockSpec((GMLP_GROUPS, CHUNK, CHUNK), full3),
            pl.BlockSpec((GMLP_GROUPS, CHUNK, gw), full3),
            pl.BlockSpec((D_GMLP, D_MODEL), full),
            pl.BlockSpec((1, D_MODEL), full),
            pl.BlockSpec((1, D_MODEL), full),
        ],
        out_specs=pl.BlockSpec((tm, D_MODEL), row),
        out_shape=jax.ShapeDtypeStruct((T, D_MODEL), F32),
        scratch_shapes=[pltpu.VMEM((tm, D_GMLP), BF16)],
        compiler_params=_cparams(("parallel",)),
        name="gmlp",
    )(x2, w_in, vec(b_in), vec(ln_g), vec(ln_b), w_s, bs_b, w_out, vec(g), vec(b))


def _ffn_kernel(x_ref, xh_ref, wup_ref, cw_ref, cb_ref, wd_ref, g_ref, b_ref, o_ref,
                *, blocks_per_seq, tf, n_slabs):
    x = x_ref[...]
    xb = x.astype(BF16)
    xhb = xh_ref[...].astype(BF16)
    seq_start = pl.program_id(0) % blocks_per_seq == 0

    def up(j):
        wup = wup_ref[j]
        gv = jnp.dot(xb, wup, preferred_element_type=F32)
        gh = jnp.dot(xhb, wup[:, :tf], preferred_element_type=F32)
        return gv, jnp.where(seq_start, 0.0, gh)

    acc = None
    nxt = up(0)
    for j in range(n_slabs):
        gv, gh = nxt
        if j + 1 < n_slabs:
            nxt = up(j + 1)
        gate = gv[:, :tf]
        val = gv[:, tf:]
        cw = cw_ref[j]
        conv = (cb_ref[j] + cw[0:1, :] * _shift_rows(gate, 2, gh)
                + cw[1:2, :] * _shift_rows(gate, 1, gh) + cw[2:3, :] * gate)
        hid = conv * jax.nn.sigmoid(conv) * val
        down = jnp.dot(hid.astype(BF16), wd_ref[j], preferred_element_type=F32)
        acc = down if acc is None else acc + down
    o_ref[...] = _layer_norm(DEEPNORM_ALPHA * x + acc, g_ref[...], b_ref[...])


def _ffn(x2, w_up, conv_w, conv_b, w_down, g, b, *, seq, tm, tf):
    T = x2.shape[0]
    n_slabs = D_FF // tf
    bps = seq // tm
    wg = w_up[:, :D_FF].reshape(D_MODEL, n_slabs, tf)
    wv = w_up[:, D_FF:].reshape(D_MODEL, n_slabs, tf)
    wup = jnp.transpose(jnp.concatenate([wg, wv], axis=2), (1, 0, 2)).astype(BF16)
    cw = jnp.transpose(conv_w.reshape(3, n_slabs, tf), (1, 0, 2))
    cb = conv_b.reshape(n_slabs, 1, tf)
    wd = w_down.reshape(n_slabs, tf, D_MODEL).astype(BF16)
    row = lambda i: (i, 0)
    full = lambda i: (0, 0)
    full3 = lambda i: (0, 0, 0)
    halo = lambda i: (jnp.maximum(i * (tm // HALO_ROWS) - 1, 0), 0)
    once = pl.Buffered(1)
    return pl.pallas_call(
        functools.partial(_ffn_kernel, blocks_per_seq=bps, tf=tf, n_slabs=n_slabs),
        grid=(T // tm,),
        in_specs=[
            pl.BlockSpec((tm, D_MODEL), row),
            pl.BlockSpec((HALO_ROWS, D_MODEL), halo),
            pl.BlockSpec((n_slabs, D_MODEL, 2 * tf), full3, pipeline_mode=once),
            pl.BlockSpec((n_slabs, 3, tf), full3, pipeline_mode=once),
            pl.BlockSpec((n_slabs, 1, tf), full3, pipeline_mode=once),
            pl.BlockSpec((n_slabs, tf, D_MODEL), full3, pipeline_mode=once),
            pl.BlockSpec((1, D_MODEL), full),
            pl.BlockSpec((1, D_MODEL), full),
        ],
        out_specs=pl.BlockSpec((tm, D_MODEL), row),
        out_shape=jax.ShapeDtypeStruct((T, D_MODEL), F32),
        compiler_params=_cparams(("parallel",)),
        name="conv_ffn",
    )(x2, x2, wup, cw, cb, wd, g.reshape(1, -1), b.reshape(1, -1))


def _rope_tables(seq):
    half = ROPE_DIM // 2
    inv_freq = ROPE_THETA ** (-jnp.arange(0, ROPE_DIM, 2, dtype=F32) / ROPE_DIM)
    ang = jnp.arange(seq, dtype=F32)[:, None] * inv_freq[None, :]
    cos, sin = jnp.cos(ang), jnp.sin(ang)
    ones = jnp.ones((seq, HEAD_DIM - ROPE_DIM), F32)
    zeros = jnp.zeros((seq, HEAD_DIM - ROPE_DIM), F32)
    zh = jnp.zeros((seq, half), F32)
    c64 = jnp.concatenate([cos, cos, ones], axis=1)
    s1_64 = jnp.concatenate([zh, sin, zeros], axis=1)
    s2_64 = jnp.concatenate([-sin, zh, zeros], axis=1)
    two = lambda t: jnp.concatenate([t, t], axis=1)
    return two(c64), two(s1_64), two(s2_64)


def kernel(x, ab_w_in, ab_shift_mu, ab_w0, ab_w2, ab_a0, ab_a2, ab_g2, ab_k_k, ab_k_a, ab_r_k, ab_lnx_g, ab_lnx_b, ab_lam_q1, ab_lam_k1, ab_lam_q2, ab_lam_k2, ab_subln_g, ab_w_out, c_w_in, c_b_in, c_ln_g, c_ln_b, c_w_s, c_b_s, c_w_out, ln1_g, ln1_b, ffn_w_up, ffn_conv_w, ffn_conv_b, ffn_w_down, ln2_g, ln2_b):
    batch, seq, _ = x.shape
    x2 = x.reshape(batch * seq, D_MODEL)
    rc, rs1, rs2 = _rope_tables(seq)
    tm_in = min(512, seq)
    tm_ffn = min(512, seq)
    tm_gmlp = min(512, seq)
    tq = min(512, seq)
    wkv_rows = min(256, seq)
    for i in range(DEPTH):
        j = i // 2
        if i % 2 == 0:
            (r, k, v, lw, kn, a, g, qd, kd, vd) = _ab_in(
                x2, ab_w_in[j].astype(BF16), ab_shift_mu[j], ab_w0[j], ab_w2[j], ab_a0[j],
                ab_a2[j].astype(BF16), ab_g2[j].astype(BF16), ab_k_k[j], ab_k_a[j], rc, rs1, rs2,
                seq=seq, tm=tm_in)
            y_r = _wkv(r, k, v, lw, kn, a, g, ab_r_k[j], ab_lnx_g[j], ab_lnx_b[j],
                       batch=batch, seq=seq, rows=wkv_rows)
            lam_init = 0.8 - 0.6 * math.exp(-0.3 * i)
            y_d = _dattn(qd, kd, vd, ab_lam_q1[j], ab_lam_k1[j], ab_lam_q2[j], ab_lam_k2[j],
                         ab_subln_g[j], batch=batch, seq=seq, tq=tq, lam_init=lam_init)
            x2 = _proj_ln(y_r, y_d, ab_w_out[j].astype(BF16), x2, ln1_g[i], ln1_b[i], tm=tm_in)
        else:
            x2 = _gmlp(x2, c_w_in[j].astype(BF16), c_b_in[j], c_ln_g[j], c_ln_b[j], c_w_s[j],
                       c_b_s[j], c_w_out[j].astype(BF16), ln1_g[i], ln1_b[i], tm=tm_gmlp)
        x2 = _ffn(x2, ffn_w_up[i], ffn_conv_w[i], ffn_conv_b[i], ffn_w_down[i], ln2_g[i], ln2_b[i],
                  seq=seq, tm=tm_ffn, tf=256)
    return x2.reshape(batch, seq, D_MODEL)
```

```python
import functools
import math

import jax
import jax.numpy as jnp
from jax import lax
from jax.experimental import pallas as pl
from jax.experimental.pallas import tpu as pltpu

F32 = jnp.float32
BF16 = jnp.bfloat16
HIGHEST = lax.Precision.HIGHEST

D_MODEL = 1024
HEAD_DIM = 64
N_RWKV_HEADS = 8
D_RWKV = N_RWKV_HEADS * HEAD_DIM
N_DIFF_HEADS = 4
D_DIFF = N_DIFF_HEADS * 2 * HEAD_DIM
DECAY_LORA = 64
AAA_LORA = 64
GATE_LORA = 128
RWKV_COLS = 3 * D_RWKV + DECAY_LORA + AAA_LORA + GATE_LORA
AB_COLS = RWKV_COLS + 3 * D_DIFF
RWKV_GN_EPS = 64e-5
ROPE_THETA = 500000.0
ROPE_DIM = HEAD_DIM // 4
CHUNK = 128
GMLP_GROUPS = 8
D_GMLP = D_MODEL
D_FF = 2816
DEPTH = 4
DEEPNORM_ALPHA = (2 * DEPTH) ** 0.25

LANES = 128
HALO_ROWS = 16
WKV_CHUNK = 64
VMEM_LIMIT = 56 * 1024 * 1024
LOG2E = math.log2(math.e)


def _cparams(sem):
    return pltpu.CompilerParams(dimension_semantics=sem, vmem_limit_bytes=VMEM_LIMIT)


def _dot(a, b):
    return jnp.dot(a.astype(BF16), b.astype(BF16), preferred_element_type=F32)


def _dot_nt(a, b):
    return lax.dot_general(a.astype(BF16), b.astype(BF16), (((1,), (1,)), ((), ())),
                           preferred_element_type=F32)


def _layer_norm(z, g, b, eps=1e-5):
    mu = jnp.mean(z, axis=-1, keepdims=True)
    d = z - mu
    var = jnp.mean(d * d, axis=-1, keepdims=True)
    return d * lax.rsqrt(var + eps) * g + b


def _shift_rows(t, k, halo):
    rolled = pltpu.roll(t, k, 0)
    row = lax.broadcasted_iota(jnp.int32, t.shape, 0)
    out = rolled
    for j in range(k):
        out = jnp.where(row == j, halo[HALO_ROWS - k + j:HALO_ROWS - k + j + 1, :], out)
    return out


def _ab_in_kernel(x_ref, xh_ref, w_ref, mu_ref, w0_ref, w2_ref, a0_ref, a2_ref, g2_ref, kk_ref,
                  ka_ref, rc_ref, rs1_ref, rs2_ref,
                  r_out, k_out, v_out, lw_out, kn_out, a_out, g_out, q_out, kd_out, vd_out,
                  *, blocks_per_seq):
    seq_start = pl.program_id(0) % blocks_per_seq == 0
    tm = x_ref.shape[0]
    xe = jnp.concatenate([x_ref[...], xh_ref[...]], axis=0).astype(BF16)
    xb = xe[:tm]

    def proj(c0, c1):
        return jnp.dot(xb, w_ref[:, c0:c1], preferred_element_type=F32)

    def proj_shifted(c0, c1):
        pe = jnp.dot(xe, w_ref[:, c0:c1], preferred_element_type=F32)
        p = pe[:tm]
        ph = jnp.where(seq_start, 0.0, pe[tm:])
        return p + mu_ref[:, c0:c1] * (_shift_rows(p, 1, ph) - p)

    o = 3 * D_RWKV
    pq = proj(RWKV_COLS, RWKV_COLS + D_DIFF)
    pk = proj(RWKV_COLS + D_DIFF, RWKV_COLS + 2 * D_DIFF)
    xl = proj_shifted(o, RWKV_COLS)
    k = proj_shifted(D_RWKV, 2 * D_RWKV)
    r = proj_shifted(0, D_RWKV)
    v = proj_shifted(2 * D_RWKV, o)
    vd_out[...] = proj(RWKV_COLS + 2 * D_DIFF, AB_COLS).astype(BF16)
    xw = xl[:, :DECAY_LORA]
    xa = xl[:, DECAY_LORA:DECAY_LORA + AAA_LORA]
    xg = xl[:, DECAY_LORA + AAA_LORA:]

    rc = jnp.concatenate([rc_ref[...]] * (D_DIFF // LANES), axis=1)
    rs1 = jnp.concatenate([rs1_ref[...]] * (D_DIFF // LANES), axis=1)
    rs2 = jnp.concatenate([rs2_ref[...]] * (D_DIFF // LANES), axis=1)
    half = ROPE_DIM // 2

    def rope(t):
        return t * rc + pltpu.roll(t, half, 1) * rs1 + pltpu.roll(t, D_DIFF - half, 1) * rs2

    q_out[...] = (rope(pq) * (HEAD_DIM ** -0.5 * LOG2E)).astype(BF16)
    kd_out[...] = rope(pk).astype(BF16)

    z = w0_ref[...] + jnp.dot(jnp.tanh(xw), w2_ref[...], precision=HIGHEST,
                              preferred_element_type=F32)
    softplus_neg = jnp.maximum(-z, 0.0) + jnp.log1p(jnp.exp(-jnp.abs(z)))
    w = -softplus_neg - 0.5
    lw_out[...] = -jnp.exp(w)
    a = jax.nn.sigmoid(a0_ref[...] + _dot(xa, a2_ref[...]))
    a_out[...] = a
    g_out[...] = _dot(jax.nn.sigmoid(xg), g2_ref[...])

    kx = k * kk_ref[...]
    sq = kx * kx
    lane = lax.broadcasted_iota(jnp.int32, (sq.shape[0], LANES), 1)
    lo = lane < HEAD_DIM
    for c in range(D_RWKV // LANES):
        blk = sq[:, c * LANES:(c + 1) * LANES]
        n_lo = jnp.sqrt(jnp.sum(jnp.where(lo, blk, 0.0), axis=-1, keepdims=True))
        n_hi = jnp.sqrt(jnp.sum(jnp.where(lo, 0.0, blk), axis=-1, keepdims=True))
        norm = jnp.maximum(jnp.where(lo, n_lo, n_hi), 1e-12)
        kn_out[:, c * LANES:(c + 1) * LANES] = kx[:, c * LANES:(c + 1) * LANES] / norm
    r_out[...] = r
    k_out[...] = k * (1.0 + (a - 1.0) * ka_ref[...])
    v_out[...] = v


def _ab_in(x2, w_in, mu, w0, w2, a0, a2, g2, k_k, k_a, rc, rs1, rs2, *, seq, tm):
    T = x2.shape[0]
    n = T // tm
    bps = seq // tm
    row = lambda i: (i, 0)
    full = lambda i: (0, 0)
    halo = lambda i: (jnp.maximum(i * (tm // HALO_ROWS) - 1, 0), 0)
    rope_map = lambda i: (i % bps, 0)
    vec = lambda a: a.reshape(1, -1)
    f32_out = jax.ShapeDtypeStruct((T, D_RWKV), F32)
    bf_out = jax.ShapeDtypeStruct((T, D_DIFF), BF16)
    out_spec = pl.BlockSpec((tm, D_RWKV), row)
    return pl.pallas_call(
        functools.partial(_ab_in_kernel, blocks_per_seq=bps),
        grid=(n,),
        in_specs=[
            pl.BlockSpec((tm, D_MODEL), row),
            pl.BlockSpec((HALO_ROWS, D_MODEL), halo),
            pl.BlockSpec((D_MODEL, AB_COLS), full),
            pl.BlockSpec((1, RWKV_COLS), full),
            pl.BlockSpec((1, D_RWKV), full),
            pl.BlockSpec((DECAY_LORA, D_RWKV), full),
            pl.BlockSpec((1, D_RWKV), full),
            pl.BlockSpec((AAA_LORA, D_RWKV), full),
            pl.BlockSpec((GATE_LORA, D_RWKV), full),
            pl.BlockSpec((1, D_RWKV), full),
            pl.BlockSpec((1, D_RWKV), full),
            pl.BlockSpec((tm, LANES), rope_map),
            pl.BlockSpec((tm, LANES), rope_map),
            pl.BlockSpec((tm, LANES), rope_map),
        ],
        out_specs=[out_spec] * 10,
        out_shape=[f32_out] * 7 + [bf_out] * 3,
        compiler_params=_cparams(("parallel",)),
        name="ab_in",
    )(x2, x2, w_in, vec(mu), vec(w0), w2, vec(a0), a2, g2, vec(k_k), vec(k_a), rc, rs1, rs2)


WKV_GROUP = 4
GW = WKV_GROUP * HEAD_DIM


def _wkv_masks():
    c = WKV_CHUNK
    row = lax.broadcasted_iota(jnp.int32, (c, GW), 0)
    col = lax.broadcasted_iota(jnp.int32, (c, GW), 1) % HEAD_DIM
    brow = lax.broadcasted_iota(jnp.int32, (GW, GW), 0) // HEAD_DIM
    bcol = lax.broadcasted_iota(jnp.int32, (GW, GW), 1) // HEAD_DIM
    bd = jnp.where(brow == bcol, 1.0, 0.0).astype(BF16)
    brow2 = lax.broadcasted_iota(jnp.int32, (2 * c * WKV_GROUP, GW), 0) // (2 * c)
    bcol2 = lax.broadcasted_iota(jnp.int32, (2 * c * WKV_GROUP, GW), 1) // HEAD_DIM
    bd2 = jnp.where(brow2 == bcol2, 1.0, 0.0).astype(BF16)
    strict_incl = jnp.concatenate([row > col, row >= col], axis=0)
    return dict(row=row, col=col, bd=bd, bd2=bd2, strict_incl=strict_incl)


def _chunk_cumsum(lw, n_chunks):
    n = n_chunks * WKV_CHUNK
    r = lax.broadcasted_iota(jnp.int32, (n, n), 0)
    c = lax.broadcasted_iota(jnp.int32, (n, n), 1)
    tri = jnp.where(jnp.logical_and(r // WKV_CHUNK == c // WKV_CHUNK, r >= c), 1.0, 0.0).astype(BF16)
    h1 = lw.astype(BF16)
    rem = lw - h1.astype(F32)
    h2 = rem.astype(BF16)
    h3 = (rem - h2.astype(F32)).astype(BF16)
    cum = jnp.dot(tri, h1, preferred_element_type=F32)
    cum = cum + jnp.dot(tri, h2, preferred_element_type=F32)
    return cum + jnp.dot(tri, h3, preferred_element_type=F32)


def _bd(w, m):
    return jnp.concatenate([w.astype(BF16)] * WKV_GROUP, axis=0) * m["bd"]


def _mm(x, w_bd):
    return jnp.dot(x.astype(BF16), w_bd, preferred_element_type=F32)


def _mm_nt(x, w_bd):
    return lax.dot_general(x.astype(BF16), w_bd, (((1,), (1,)), ((), ())),
                           preferred_element_type=F32)


def _seg_sum(x, m):
    hi = x.astype(BF16)
    lo = (x - hi.astype(F32)).astype(BF16)
    s = jnp.dot(jnp.concatenate([hi, lo], axis=0), m["bd"], preferred_element_type=F32)
    return s[:x.shape[0]] + s[x.shape[0]:]


def _unit_lower_inverse(n_mats, m):
    row, col = m["row"], m["col"]
    eye = jnp.where(row == col, 1.0, 0.0)
    same16 = (row // 16) == (col // 16)
    c = WKV_CHUNK
    pws = [jnp.where(same16, n, 0.0) for n in n_mats]
    ts = [eye + p for p in pws]
    pws = [_mm(p, _bd(p, m)) for p in pws]
    for lvl in range(3):
        pw_bds = [_bd(p, m) for p in pws]
        if lvl < 2:
            both = [_mm(jnp.concatenate([t, p], axis=0), b) for t, p, b in zip(ts, pws, pw_bds)]
            ts = [t + bo[:c] for t, bo in zip(ts, both)]
            pws = [bo[c:] for bo in both]
        else:
            ts = [t + _mm(t, b) for t, b in zip(ts, pw_bds)]
    blk = 16
    while blk < WKV_CHUNK:
        same_lo = (row // blk) == (col // blk)
        same_hi = (row // (2 * blk)) == (col // (2 * blk))
        sel = jnp.logical_and(same_hi, jnp.logical_not(same_lo))
        halves = [_mm(t, _bd(jnp.where(sel, n, 0.0), m)) for t, n in zip(ts, n_mats)]
        ts = [t + _mm(h, _bd(t, m)) for t, h in zip(ts, halves)]
        blk *= 2
    return ts


def _wkv_prepare(r, k, v, lw, cum, kn, a, m):
    c = r.shape[0]
    mid = cum[c // 2 - 1:c // 2, :]
    end = cum[c - 1:c, :]
    e_abs = jnp.exp(cum)
    e_abs_prev = jnp.exp(cum - lw)
    e_mid = jnp.exp(-mid)
    e_neg = jnp.exp(mid - cum)
    e_end = jnp.exp(end - cum)
    g_end = jnp.exp(end)
    bvec = kn * a
    r0 = r * e_abs
    a0 = -kn * e_abs_prev
    rt = r0 * e_mid
    at = a0 * e_mid
    kt = k * e_neg
    bt = bvec * e_neg
    kh = k * e_end
    bh = bvec * e_end
    out = []
    for gi in range(N_RWKV_HEADS // WKV_GROUP):
        sl = slice(gi * GW, (gi + 1) * GW)
        lhs = jnp.concatenate([at[:, sl], rt[:, sl]], axis=0)
        ab = jnp.where(m["strict_incl"], _mm_nt(lhs, _bd(bt[:, sl], m)), 0.0)
        ak = jnp.where(m["strict_incl"], _mm_nt(lhs, _bd(kt[:, sl], m)), 0.0)
        bk_t = jnp.transpose(jnp.concatenate([bh[:, sl], kh[:, sl]], axis=0))
        bk_t = jnp.concatenate([bk_t[hh * HEAD_DIM:(hh + 1) * HEAD_DIM] for hh in range(WKV_GROUP)],
                               axis=1).astype(BF16)
        g_diag = jnp.where(m["row"] == m["col"], g_end[:, sl], 0.0)
        out.append(dict(a_ab=ab[:c], a_rb=ab[c:], ak=ak, vg=v[:, sl], a0=a0[:, sl], r0=r0[:, sl],
                        g_diag=g_diag, bk_t=bk_t))
    return out


def _wkv_kernel(r_ref, k_ref, v_ref, lw_ref, kn_ref, a_ref, g_ref, rk_ref, lg_ref, lb_ref,
                o_ref, s_scr, *, n_chunks):
    @pl.when(pl.program_id(1) == 0)
    def _():
        s_scr[...] = jnp.zeros_like(s_scr)

    n_groups = N_RWKV_HEADS // WKV_GROUP
    m = _wkv_masks()
    chunk_rows = [slice(ci * WKV_CHUNK, (ci + 1) * WKV_CHUNK) for ci in range(n_chunks)]

    c = WKV_CHUNK
    lw_all = lw_ref[...]
    cum_all = _chunk_cumsum(lw_all, n_chunks)
    probs = []
    for rows in chunk_rows:
        probs += _wkv_prepare(r_ref[rows, :], k_ref[rows, :], v_ref[rows, :], lw_all[rows, :],
                              cum_all[rows, :], kn_ref[rows, :], a_ref[rows, :], m)
    n_probs = len(probs)
    split = lambda stacked: [stacked[i * c:(i + 1) * c] for i in range(n_probs)]
    ts = _unit_lower_inverse([p["a_ab"] for p in probs], m)
    vg_bds = [_bd(p["vg"], m) for p in probs]
    akv_yvs = [_mm(p["ak"], vb) for p, vb in zip(probs, vg_bds)]
    t_bfs = [t.astype(BF16) for t in ts]
    xas = [_mm(t, _bd(p["a0"], m)) for t, p in zip(t_bfs, probs)]
    xvs = [_mm(t, _bd(ay[:c], m)) for t, ay in zip(t_bfs, akv_yvs)]
    xrs = [jnp.concatenate([xa, p["r0"]], axis=0).astype(BF16) for xa, p in zip(xas, probs)]
    decays = split(_seg_sum(jnp.concatenate([p["g_diag"] for p in probs], axis=0), m))
    rkr = r_ref[...] * k_ref[...] * rk_ref[...]
    bonus_w = split(_seg_sum(jnp.concatenate(
        [rkr[rows, gi * GW:(gi + 1) * GW] for rows in chunk_rows for gi in range(n_groups)], axis=0), m))

    states = [s_scr[gi] for gi in range(n_groups)]
    ys = []
    for ci in range(n_chunks):
        for gi in range(n_groups):
            i = ci * n_groups + gi
            p = probs[i]
            s0 = states[gi]
            uy = _mm(xrs[i], _bd(s0, m))
            u = uy[:c] + xvs[i]
            ys.append(uy[c:] + _mm(p["a_rb"], _bd(u, m)) + akv_yvs[i][c:])
            uv = jnp.concatenate([u, p["vg"]], axis=0).astype(BF16)
            uv_bd = jnp.concatenate([uv] * WKV_GROUP, axis=0) * m["bd2"]
            states[gi] = s0 * decays[i] + jnp.dot(p["bk_t"], uv_bd, preferred_element_type=F32)
    for gi in range(n_groups):
        s_scr[gi] = states[gi]

    y_all = jnp.concatenate(ys, axis=0)
    d_all = y_all - _seg_sum(y_all, m) * (1.0 / HEAD_DIM)
    inv_all = lax.rsqrt(_seg_sum(d_all * d_all, m) * (1.0 / HEAD_DIM) + RWKV_GN_EPS)
    for ci, rows in enumerate(chunk_rows):
        for gi in range(n_groups):
            i = ci * n_groups + gi
            sl = slice(gi * GW, (gi + 1) * GW)
            yn = d_all[i * c:(i + 1) * c] * inv_all[i * c:(i + 1) * c] * lg_ref[:, sl] + lb_ref[:, sl]
            o_ref[rows, sl] = ((yn + bonus_w[i] * probs[i]["vg"]) * g_ref[rows, sl]).astype(BF16)


def _wkv(r, k, v, lw, kn, a, g, r_k, lnx_g, lnx_b, *, batch, seq, rows):
    T = r.shape[0]
    nb = seq // rows
    blk = pl.BlockSpec((rows, D_RWKV), lambda b, i: (b * nb + i, 0))
    par = pl.BlockSpec((1, D_RWKV), lambda b, i: (0, 0))
    return pl.pallas_call(
        functools.partial(_wkv_kernel, n_chunks=rows // WKV_CHUNK),
        grid=(batch, nb),
        in_specs=[blk] * 7 + [par] * 3,
        out_specs=blk,
        out_shape=jax.ShapeDtypeStruct((T, D_RWKV), BF16),
        scratch_shapes=[pltpu.VMEM((N_RWKV_HEADS // WKV_GROUP, HEAD_DIM, GW), F32)],
        compiler_params=_cparams(("parallel", "arbitrary")),
        name="wkv7",
    )(r, k, v, lw, kn, a, g, r_k.reshape(1, -1), lnx_g.reshape(1, -1), lnx_b.reshape(1, -1))


def _dattn_kernel(q_ref, k_ref, v_ref, lq1_ref, lk1_ref, lq2_ref, lk2_ref, sg_ref, o_ref,
                  qs_scr, m_scr, l_scr, acc_scr, s0_scr, s1_scr, p0_scr, p1_scr, al0_scr, al1_scr,
                  *, tq, lam_init):
    qi = pl.program_id(2)
    q = q_ref[...]
    lane = lax.broadcasted_iota(jnp.int32, q.shape, 1)
    zero = jnp.zeros_like(q)
    qs_scr[:tq, :] = jnp.where(lane < HEAD_DIM, q, zero)
    qs_scr[tq:, :] = jnp.where(lane < HEAD_DIM, zero, q)
    m_scr[...] = jnp.full_like(m_scr, -jnp.inf)
    l_scr[...] = jnp.zeros_like(l_scr)
    acc_scr[...] = jnp.zeros_like(acc_scr)

    bufs = ((s0_scr, p0_scr, al0_scr), (s1_scr, p1_scr, al1_scr))

    def scores(j, par):
        start = pl.multiple_of(j * tq, tq)
        bufs[par][0][...] = _dot_nt(qs_scr[...], k_ref[pl.ds(start, tq), :])

    def softmax(par, masked):
        s_ref, p_ref, al_ref = bufs[par]
        s = s_ref[...]
        if masked:
            r_pos = lax.broadcasted_iota(jnp.int32, s.shape, 0) % tq
            c_pos = lax.broadcasted_iota(jnp.int32, s.shape, 1)
            s = jnp.where(c_pos <= r_pos, s, -jnp.inf)
        m_prev = m_scr[...]
        m_new = jnp.maximum(m_prev, jnp.max(s, axis=-1, keepdims=True))
        alpha = jnp.exp2(m_prev - m_new)
        p = jnp.exp2(s - jnp.concatenate([m_new] * (tq // LANES), axis=1))
        l_scr[...] = alpha * l_scr[...] + jnp.sum(p, axis=-1, keepdims=True)
        m_scr[...] = m_new
        p_ref[...] = p.astype(BF16)
        al_ref[...] = alpha

    def values(j, par):
        _, p_ref, al_ref = bufs[par]
        start = pl.multiple_of(j * tq, tq)
        acc_scr[...] = al_ref[...] * acc_scr[...] + jnp.dot(
            p_ref[...], v_ref[pl.ds(start, tq), :], preferred_element_type=F32)

    scores(0, 0)

    @pl.when(qi == 0)
    def _():
        softmax(0, True)
        values(0, 0)

    @pl.when(qi > 0)
    def _():
        scores(1, 1)
        softmax(0, False)

        def body(i, carry):
            t = 2 + 2 * i
            values(t - 2, 0)
            scores(t, 0)
            softmax(1, False)
            values(t - 1, 1)
            scores(t + 1, 1)
            softmax(0, False)
            return carry

        lax.fori_loop(0, (qi - 1) // 2, body, 0)

        @pl.when(qi % 2 == 1)
        def _():
            values(qi - 1, 0)
            softmax(1, True)
            values(qi, 1)

        @pl.when(qi % 2 == 0)
        def _():
            values(qi - 2, 0)
            scores(qi, 0)
            softmax(1, False)
            values(qi - 1, 1)
            softmax(0, True)
            values(qi, 0)

    lam = (jnp.exp(jnp.sum(lq1_ref[...] * lk1_ref[...], axis=-1, keepdims=True))
           - jnp.exp(jnp.sum(lq2_ref[...] * lk2_ref[...], axis=-1, keepdims=True)) + lam_init)
    on = acc_scr[...] / l_scr[...]
    o = on[:tq, :] - lam * on[tq:, :]
    o = o * lax.rsqrt(jnp.mean(o * o, axis=-1, keepdims=True) + 1e-5) * sg_ref[...]
    o_ref[...] = (o * (1.0 - lam_init)).astype(BF16)


def _dattn(q, k, v, lq1, lk1, lq2, lk2, subln_g, *, batch, seq, tq, lam_init):
    T = q.shape[0]
    nq = seq // tq
    hd = 2 * HEAD_DIM
    qmap = lambda b, h, i: (b * nq + i, h)
    kvmap = lambda b, h, i: (b, h)
    par = lambda b, h, i: (0, 0)
    vec = lambda a: a.reshape(1, -1)
    return pl.pallas_call(
        functools.partial(_dattn_kernel, tq=tq, lam_init=lam_init),
        grid=(batch, N_DIFF_HEADS, nq),
        in_specs=[
            pl.BlockSpec((tq, hd), qmap),
            pl.BlockSpec((seq, hd), kvmap),
            pl.BlockSpec((seq, hd), kvmap),
            pl.BlockSpec((1, HEAD_DIM), par),
            pl.BlockSpec((1, HEAD_DIM), par),
            pl.BlockSpec((1, HEAD_DIM), par),
            pl.BlockSpec((1, HEAD_DIM), par),
            pl.BlockSpec((1, hd), par),
        ],
        out_specs=pl.BlockSpec((tq, hd), qmap),
        out_shape=jax.ShapeDtypeStruct((T, D_DIFF), BF16),
        scratch_shapes=[
            pltpu.VMEM((2 * tq, hd), BF16),
            pltpu.VMEM((2 * tq, LANES), F32),
            pltpu.VMEM((2 * tq, LANES), F32),
            pltpu.VMEM((2 * tq, hd), F32),
            pltpu.VMEM((2 * tq, tq), F32),
            pltpu.VMEM((2 * tq, tq), F32),
            pltpu.VMEM((2 * tq, tq), BF16),
            pltpu.VMEM((2 * tq, tq), BF16),
            pltpu.VMEM((2 * tq, LANES), F32),
            pltpu.VMEM((2 * tq, LANES), F32),
        ],
        compiler_params=_cparams(("parallel", "parallel", "arbitrary")),
        name="diff_attn",
    )(q, k, v, vec(lq1), vec(lk1), vec(lq2), vec(lk2), vec(subln_g))


def _proj_ln_kernel(yr_ref, yd_ref, w_ref, x_ref, g_ref, b_ref, o_ref):
    mix = jnp.dot(yr_ref[...], w_ref[:D_RWKV, :], preferred_element_type=F32)
    mix = mix + jnp.dot(yd_ref[...], w_ref[D_RWKV:, :], preferred_element_type=F32)
    o_ref[...] = _layer_norm(DEEPNORM_ALPHA * x_ref[...] + mix, g_ref[...], b_ref[...])


def _proj_ln(yr, yd, w_out, x2, g, b, *, tm):
    T = x2.shape[0]
    row = lambda i: (i, 0)
    full = lambda i: (0, 0)
    return pl.pallas_call(
        _proj_ln_kernel,
        grid=(T // tm,),
        in_specs=[
            pl.BlockSpec((tm, D_RWKV), row),
            pl.BlockSpec((tm, D_DIFF), row),
            pl.BlockSpec((D_RWKV + D_DIFF, D_MODEL), full),
            pl.BlockSpec((tm, D_MODEL), row),
            pl.BlockSpec((1, D_MODEL), full),
            pl.BlockSpec((1, D_MODEL), full),
        ],
        out_specs=pl.BlockSpec((tm, D_MODEL), row),
        out_shape=jax.ShapeDtypeStruct((T, D_MODEL), F32),
        compiler_params=_cparams(("parallel",)),
        name="ab_out_ln",
    )(yr, yd, w_out, x2, g.reshape(1, -1), b.reshape(1, -1))


def _gmlp_kernel(x_ref, win_ref, bin_ref, lng_ref, lnb_ref, ws_ref, bs_ref, wout_ref, g_ref, b_ref,
                 o_ref, gated_scr, *, tm):
    x = x_ref[...]
    xb = x.astype(BF16)

    def gelu(h):
        return 0.5 * h * (1.0 + lax.erf(h * (0.5 ** 0.5)))

    hv = jnp.dot(xb, win_ref[:, D_GMLP:], preferred_element_type=F32) + bin_ref[:, D_GMLP:]
    hu = jnp.dot(xb, win_ref[:, :D_GMLP], preferred_element_type=F32) + bin_ref[:, :D_GMLP]
    v = _layer_norm(gelu(hv), lng_ref[...], lnb_ref[...]).astype(BF16)
    u = gelu(hu)
    row = lax.broadcasted_iota(jnp.int32, (CHUNK, CHUNK), 0)
    col = lax.broadcasted_iota(jnp.int32, (CHUNK, CHUNK), 1)
    gw = D_GMLP // GMLP_GROUPS
    n_chunks = tm // CHUNK
    for gi in range(GMLP_GROUPS):
        cs = slice(gi * gw, (gi + 1) * gw)
        ws = jnp.where(row >= col, ws_ref[gi], 0.0).astype(BF16)
        v_side = jnp.concatenate([v[c * CHUNK:(c + 1) * CHUNK, cs] for c in range(n_chunks)], axis=1)
        mixed = jnp.dot(ws, v_side, preferred_element_type=F32)
        for c in range(n_chunks):
            rs = slice(c * CHUNK, (c + 1) * CHUNK)
            gated_scr[rs, cs] = (u[rs, cs] * (mixed[:, c * gw:(c + 1) * gw] + bs_ref[gi])).astype(BF16)
    mix = jnp.dot(gated_scr[...], wout_ref[...], preferred_element_type=F32)
    o_ref[...] = _layer_norm(DEEPNORM_ALPHA * x + mix, g_ref[...], b_ref[...])


def _gmlp(x2, w_in, b_in, ln_g, ln_b, w_s, b_s, w_out, g, b, *, tm):
    T = x2.shape[0]
    row = lambda i: (i, 0)
    full = lambda i: (0, 0)
    full3 = lambda i: (0, 0, 0)
    vec = lambda a: a.reshape(1, -1)
    gw = D_GMLP // GMLP_GROUPS
    bs_b = jnp.broadcast_to(b_s[:, :, None], (GMLP_GROUPS, CHUNK, gw))
    return pl.pallas_call(
        functools.partial(_gmlp_kernel, tm=tm),
        grid=(T // tm,),
        in_specs=[
            pl.BlockSpec((tm, D_MODEL), row),
            pl.BlockSpec((D_MODEL, 2 * D_GMLP), full),
            pl.BlockSpec((1, 2 * D_GMLP), full),
            pl.BlockSpec((1, D_GMLP), full),
            pl.BlockSpec((1, D_GMLP), full),
            pl.BlockSpec((GMLP_GROUPS, CHUNK, CHUNK), full3),
            pl.BlockSpec((GMLP_GROUPS, CHUNK, gw), full3),
            pl.BlockSpec((D_GMLP, D_MODEL), full),
            pl.BlockSpec((1, D_MODEL), full),
            pl.BlockSpec((1, D_MODEL), full),
        ],
        out_specs=pl.BlockSpec((tm, D_MODEL), row),
        out_shape=jax.ShapeDtypeStruct((T, D_MODEL), F32),
        scratch_shapes=[pltpu.VMEM((tm, D_GMLP), BF16)],
        compiler_params=_cparams(("parallel",)),
        name="gmlp",
    )(x2, w_in, vec(b_in), vec(ln_g), vec(ln_b), w_s, bs_b, w_out, vec(g), vec(b))


def _ffn_kernel(x_ref, xh_ref, wup_ref, cw_ref, cb_ref, wd_ref, g_ref, b_ref, o_ref,
                *, blocks_per_seq, tf, n_slabs):
    x = x_ref[...]
    tm = x.shape[0]
    xe = jnp.concatenate([x, xh_ref[...]], axis=0).astype(BF16)
    seq_start = pl.program_id(0) % blocks_per_seq == 0

    def up(j):
        gve = jnp.dot(xe, wup_ref[j], preferred_element_type=F32)
        return gve[:tm], jnp.where(seq_start, 0.0, gve[tm:, :tf])

    acc = None
    nxt = up(0)
    for j in range(n_slabs):
        gv, gh = nxt
        if j + 1 < n_slabs:
            nxt = up(j + 1)
        gate = gv[:, :tf]
        val = gv[:, tf:]
        cw = cw_ref[j]
        conv = (cb_ref[j] + cw[0:1, :] * _shift_rows(gate, 2, gh)
                + cw[1:2, :] * _shift_rows(gate, 1, gh) + cw[2:3, :] * gate)
        hid = conv * jax.nn.sigmoid(conv) * val
        down = jnp.dot(hid.astype(BF16), wd_ref[j], preferred_element_type=F32)
        acc = down if acc is None else acc + down
    o_ref[...] = _layer_norm(DEEPNORM_ALPHA * x + acc, g_ref[...], b_ref[...])


def _ffn(x2, w_up, conv_w, conv_b, w_down, g, b, *, seq, tm, tf):
    T = x2.shape[0]
    n_slabs = D_FF // tf
    bps = seq // tm
    wg = w_up[:, :D_FF].reshape(D_MODEL, n_slabs, tf)
    wv = w_up[:, D_FF:].reshape(D_MODEL, n_slabs, tf)
    wup = jnp.transpose(jnp.concatenate([wg, wv], axis=2), (1, 0, 2)).astype(BF16)
    cw = jnp.transpose(conv_w.reshape(3, n_slabs, tf), (1, 0, 2))
    cb = conv_b.reshape(n_slabs, 1, tf)
    wd = w_down.reshape(n_slabs, tf, D_MODEL).astype(BF16)
    row = lambda i: (i, 0)
    full = lambda i: (0, 0)
    full3 = lambda i: (0, 0, 0)
    halo = lambda i: (jnp.maximum(i * (tm // HALO_ROWS) - 1, 0), 0)
    once = pl.Buffered(1)
    return pl.pallas_call(
        functools.partial(_ffn_kernel, blocks_per_seq=bps, tf=tf, n_slabs=n_slabs),
        grid=(T // tm,),
        in_specs=[
            pl.BlockSpec((tm, D_MODEL), row),
            pl.BlockSpec((HALO_ROWS, D_MODEL), halo),
            pl.BlockSpec((n_slabs, D_MODEL, 2 * tf), full3, pipeline_mode=once),
            pl.BlockSpec((n_slabs, 3, tf), full3, pipeline_mode=once),
            pl.BlockSpec((n_slabs, 1, tf), full3, pipeline_mode=once),
            pl.BlockSpec((n_slabs, tf, D_MODEL), full3, pipeline_mode=once),
            pl.BlockSpec((1, D_MODEL), full),
            pl.BlockSpec((1, D_MODEL), full),
        ],
        out_specs=pl.BlockSpec((tm, D_MODEL), row),
        out_shape=jax.ShapeDtypeStruct((T, D_MODEL), F32),
        compiler_params=_cparams(("parallel",)),
        name="conv_ffn",
    )(x2, x2, wup, cw, cb, wd, g.reshape(1, -1), b.reshape(1, -1))


def _rope_tables(seq):
    half = ROPE_DIM // 2
    inv_freq = ROPE_THETA ** (-jnp.arange(0, ROPE_DIM, 2, dtype=F32) / ROPE_DIM)
    ang = jnp.arange(seq, dtype=F32)[:, None] * inv_freq[None, :]
    cos, sin = jnp.cos(ang), jnp.sin(ang)
    ones = jnp.ones((seq, HEAD_DIM - ROPE_DIM), F32)
    zeros = jnp.zeros((seq, HEAD_DIM - ROPE_DIM), F32)
    zh = jnp.zeros((seq, half), F32)
    c64 = jnp.concatenate([cos, cos, ones], axis=1)
    s1_64 = jnp.concatenate([zh, sin, zeros], axis=1)
    s2_64 = jnp.concatenate([-sin, zh, zeros], axis=1)
    two = lambda t: jnp.concatenate([t, t], axis=1)
    return two(c64), two(s1_64), two(s2_64)


def kernel(x, ab_w_in, ab_shift_mu, ab_w0, ab_w2, ab_a0, ab_a2, ab_g2, ab_k_k, ab_k_a, ab_r_k, ab_lnx_g, ab_lnx_b, ab_lam_q1, ab_lam_k1, ab_lam_q2, ab_lam_k2, ab_subln_g, ab_w_out, c_w_in, c_b_in, c_ln_g, c_ln_b, c_w_s, c_b_s, c_w_out, ln1_g, ln1_b, ffn_w_up, ffn_conv_w, ffn_conv_b, ffn_w_down, ln2_g, ln2_b):
    batch, seq, _ = x.shape
    x2 = x.reshape(batch * seq, D_MODEL)
    rc, rs1, rs2 = _rope_tables(seq)
    tm_in = min(512, seq)
    tm_ffn = min(512, seq)
    tm_gmlp = min(512, seq)
    tq = min(512, seq)
    wkv_rows = min(256, seq)
    for i in range(DEPTH):
        j = i // 2
        if i % 2 == 0:
            (r, k, v, lw, kn, a, g, qd, kd, vd) = _ab_in(
                x2, ab_w_in[j].astype(BF16), ab_shift_mu[j], ab_w0[j], ab_w2[j], ab_a0[j],
                ab_a2[j].astype(BF16), ab_g2[j].astype(BF16), ab_k_k[j], ab_k_a[j], rc, rs1, rs2,
                seq=seq, tm=tm_in)
            y_r = _wkv(r, k, v, lw, kn, a, g, ab_r_k[j], ab_lnx_g[j], ab_lnx_b[j],
                       batch=batch, seq=seq, rows=wkv_rows)
            lam_init = 0.8 - 0.6 * math.exp(-0.3 * i)
            y_d = _dattn(qd, kd, vd, ab_lam_q1[j], ab_lam_k1[j], ab_lam_q2[j], ab_lam_k2[j],
                         ab_subln_g[j], batch=batch, seq=seq, tq=tq, lam_init=lam_init)
            x2 = _proj_ln(y_r, y_d, ab_w_out[j].astype(BF16), x2, ln1_g[i], ln1_b[i], tm=tm_in)
        else:
            x2 = _gmlp(x2, c_w_in[j].astype(BF16), c_b_in[j], c_ln_g[j], c_ln_b[j], c_w_s[j],
                       c_b_s[j], c_w_out[j].astype(BF16), ln1_g[i], ln1_b[i], tm=tm_gmlp)
        x2 = _ffn(x2, ffn_w_up[i], ffn_conv_w[i], ffn_conv_b[i], ffn_w_down[i], ln2_g[i], ln2_b[i],
                  seq=seq, tm=tm_ffn, tf=256)
    return x2.reshape(batch, seq, D_MODEL)
```

```python
import functools
import math

import jax
import jax.numpy as jnp
from jax import lax
from jax.experimental import pallas as pl
from jax.experimental.pallas import tpu as pltpu

F32 = jnp.float32
BF16 = jnp.bfloat16
HIGHEST = lax.Precision.HIGHEST

D_MODEL = 1024
HEAD_DIM = 64
N_RWKV_HEADS = 8
D_RWKV = N_RWKV_HEADS * HEAD_DIM
N_DIFF_HEADS = 4
D_DIFF = N_DIFF_HEADS * 2 * HEAD_DIM
DECAY_LORA = 64
AAA_LORA = 64
GATE_LORA = 128
RWKV_COLS = 3 * D_RWKV + DECAY_LORA + AAA_LORA + GATE_LORA
AB_COLS = RWKV_COLS + 3 * D_DIFF
RWKV_GN_EPS = 64e-5
ROPE_THETA = 500000.0
ROPE_DIM = HEAD_DIM // 4
CHUNK = 128
GMLP_GROUPS = 8
D_GMLP = D_MODEL
D_FF = 2816
DEPTH = 4
DEEPNORM_ALPHA = (2 * DEPTH) ** 0.25

LANES = 128
HALO_ROWS = 16
WKV_CHUNK = 64
VMEM_LIMIT = 56 * 1024 * 1024
LOG2E = math.log2(math.e)


def _cparams(sem):
    return pltpu.CompilerParams(dimension_semantics=sem, vmem_limit_bytes=VMEM_LIMIT)


def _dot(a, b):
    return jnp.dot(a.astype(BF16), b.astype(BF16), preferred_element_type=F32)


def _dot_nt(a, b):
    return lax.dot_general(a.astype(BF16), b.astype(BF16), (((1,), (1,)), ((), ())),
                           preferred_element_type=F32)


def _layer_norm(z, g, b, eps=1e-5):
    mu = jnp.mean(z, axis=-1, keepdims=True)
    d = z - mu
    var = jnp.mean(d * d, axis=-1, keepdims=True)
    return d * lax.rsqrt(var + eps) * g + b


def _shift_rows(t, k, halo):
    rolled = pltpu.roll(t, k, 0)
    row = lax.broadcasted_iota(jnp.int32, t.shape, 0)
    out = rolled
    for j in range(k):
        out = jnp.where(row == j, halo[HALO_ROWS - k + j:HALO_ROWS - k + j + 1, :], out)
    return out


def _ab_in_kernel(x_ref, xh_ref, w_ref, mu_ref, w0_ref, w2_ref, a0_ref, a2_ref, g2_ref, kk_ref,
                  ka_ref, rc_ref, rs1_ref, rs2_ref,
                  r_out, k_out, v_out, lw_out, kn_out, a_out, g_out, q_out, kd_out, vd_out,
                  *, blocks_per_seq):
    seq_start = pl.program_id(0) % blocks_per_seq == 0
    tm = x_ref.shape[0]
    xe = jnp.concatenate([x_ref[...], xh_ref[...]], axis=0).astype(BF16)
    xb = xe[:tm]

    def proj(c0, c1):
        return jnp.dot(xb, w_ref[:, c0:c1], preferred_element_type=F32)

    def proj_shifted(c0, c1):
        pe = jnp.dot(xe, w_ref[:, c0:c1], preferred_element_type=F32)
        p = pe[:tm]
        ph = jnp.where(seq_start, 0.0, pe[tm:])
        return p + mu_ref[:, c0:c1] * (_shift_rows(p, 1, ph) - p)

    o = 3 * D_RWKV
    pq = proj(RWKV_COLS, RWKV_COLS + D_DIFF)
    pk = proj(RWKV_COLS + D_DIFF, RWKV_COLS + 2 * D_DIFF)
    xl = proj_shifted(o, RWKV_COLS)
    k = proj_shifted(D_RWKV, 2 * D_RWKV)
    r = proj_shifted(0, D_RWKV)
    v = proj_shifted(2 * D_RWKV, o)
    vd_out[...] = proj(RWKV_COLS + 2 * D_DIFF, AB_COLS).astype(BF16)
    xw = xl[:, :DECAY_LORA]
    xa = xl[:, DECAY_LORA:DECAY_LORA + AAA_LORA]
    xg = xl[:, DECAY_LORA + AAA_LORA:]

    rc = jnp.concatenate([rc_ref[...]] * (D_DIFF // LANES), axis=1)
    rs1 = jnp.concatenate([rs1_ref[...]] * (D_DIFF // LANES), axis=1)
    rs2 = jnp.concatenate([rs2_ref[...]] * (D_DIFF // LANES), axis=1)
    half = ROPE_DIM // 2

    def rope(t):
        return t * rc + pltpu.roll(t, half, 1) * rs1 + pltpu.roll(t, D_DIFF - half, 1) * rs2

    q_out[...] = (rope(pq) * (HEAD_DIM ** -0.5 * LOG2E)).astype(BF16)
    kd_out[...] = rope(pk).astype(BF16)

    z = w0_ref[...] + jnp.dot(jnp.tanh(xw), w2_ref[...], precision=HIGHEST,
                              preferred_element_type=F32)
    softplus_neg = jnp.maximum(-z, 0.0) + jnp.log1p(jnp.exp(-jnp.abs(z)))
    w = -softplus_neg - 0.5
    lw_out[...] = -jnp.exp(w)
    a = jax.nn.sigmoid(a0_ref[...] + _dot(xa, a2_ref[...]))
    a_out[...] = a
    g_out[...] = _dot(jax.nn.sigmoid(xg), g2_ref[...])

    kx = k * kk_ref[...]
    sq = kx * kx
    lane = lax.broadcasted_iota(jnp.int32, (sq.shape[0], LANES), 1)
    lo = lane < HEAD_DIM
    for c in range(D_RWKV // LANES):
        blk = sq[:, c * LANES:(c + 1) * LANES]
        n_lo = jnp.sqrt(jnp.sum(jnp.where(lo, blk, 0.0), axis=-1, keepdims=True))
        n_hi = jnp.sqrt(jnp.sum(jnp.where(lo, 0.0, blk), axis=-1, keepdims=True))
        norm = jnp.maximum(jnp.where(lo, n_lo, n_hi), 1e-12)
        kn_out[:, c * LANES:(c + 1) * LANES] = kx[:, c * LANES:(c + 1) * LANES] / norm
    r_out[...] = r
    k_out[...] = k * (1.0 + (a - 1.0) * ka_ref[...])
    v_out[...] = v


def _ab_in(x2, w_in, mu, w0, w2, a0, a2, g2, k_k, k_a, rc, rs1, rs2, *, seq, tm):
    T = x2.shape[0]
    n = T // tm
    bps = seq // tm
    row = lambda i: (i, 0)
    full = lambda i: (0, 0)
    halo = lambda i: (jnp.maximum(i * (tm // HALO_ROWS) - 1, 0), 0)
    rope_map = lambda i: (i % bps, 0)
    vec = lambda a: a.reshape(1, -1)
    f32_out = jax.ShapeDtypeStruct((T, D_RWKV), F32)
    bf_out = jax.ShapeDtypeStruct((T, D_DIFF), BF16)
    out_spec = pl.BlockSpec((tm, D_RWKV), row)
    return pl.pallas_call(
        functools.partial(_ab_in_kernel, blocks_per_seq=bps),
        grid=(n,),
        in_specs=[
            pl.BlockSpec((tm, D_MODEL), row),
            pl.BlockSpec((HALO_ROWS, D_MODEL), halo),
            pl.BlockSpec((D_MODEL, AB_COLS), full),
            pl.BlockSpec((1, RWKV_COLS), full),
            pl.BlockSpec((1, D_RWKV), full),
            pl.BlockSpec((DECAY_LORA, D_RWKV), full),
            pl.BlockSpec((1, D_RWKV), full),
            pl.BlockSpec((AAA_LORA, D_RWKV), full),
            pl.BlockSpec((GATE_LORA, D_RWKV), full),
            pl.BlockSpec((1, D_RWKV), full),
            pl.BlockSpec((1, D_RWKV), full),
            pl.BlockSpec((tm, LANES), rope_map),
            pl.BlockSpec((tm, LANES), rope_map),
            pl.BlockSpec((tm, LANES), rope_map),
        ],
        out_specs=[out_spec] * 10,
        out_shape=[f32_out] * 7 + [bf_out] * 3,
        compiler_params=_cparams(("parallel",)),
        name="ab_in",
    )(x2, x2, w_in, vec(mu), vec(w0), w2, vec(a0), a2, g2, vec(k_k), vec(k_a), rc, rs1, rs2)


WKV_GROUP = 4
GW = WKV_GROUP * HEAD_DIM


def _wkv_masks():
    c = WKV_CHUNK
    row = lax.broadcasted_iota(jnp.int32, (c, GW), 0)
    col = lax.broadcasted_iota(jnp.int32, (c, GW), 1) % HEAD_DIM
    brow = lax.broadcasted_iota(jnp.int32, (GW, GW), 0) // HEAD_DIM
    bcol = lax.broadcasted_iota(jnp.int32, (GW, GW), 1) // HEAD_DIM
    bd = jnp.where(brow == bcol, 1.0, 0.0).astype(BF16)
    brow2 = lax.broadcasted_iota(jnp.int32, (2 * c * WKV_GROUP, GW), 0) // (2 * c)
    bcol2 = lax.broadcasted_iota(jnp.int32, (2 * c * WKV_GROUP, GW), 1) // HEAD_DIM
    bd2 = jnp.where(brow2 == bcol2, 1.0, 0.0).astype(BF16)
    strict_incl = jnp.concatenate([row > col, row >= col], axis=0)
    return dict(row=row, col=col, bd=bd, bd2=bd2, strict_incl=strict_incl)


def _chunk_cumsum(lw, n_chunks):
    n = 2 * WKV_CHUNK
    r = lax.broadcasted_iota(jnp.int32, (n, n), 0)
    c = lax.broadcasted_iota(jnp.int32, (n, n), 1)
    tri = jnp.where(jnp.logical_and(r // WKV_CHUNK == c // WKV_CHUNK, r >= c), 1.0, 0.0).astype(BF16)
    h1 = lw.astype(BF16)
    rem = lw - h1.astype(F32)
    h2 = rem.astype(BF16)
    h3 = (rem - h2.astype(F32)).astype(BF16)
    out = []
    for b in range(n_chunks // 2):
        rows = slice(b * n, (b + 1) * n)
        cum = jnp.dot(tri, h1[rows], preferred_element_type=F32)
        cum = cum + jnp.dot(tri, h2[rows], preferred_element_type=F32)
        out.append(cum + jnp.dot(tri, h3[rows], preferred_element_type=F32))
    return jnp.concatenate(out, axis=0)


def _bd(w, m):
    return jnp.concatenate([w.astype(BF16)] * WKV_GROUP, axis=0) * m["bd"]


def _mm(x, w_bd):
    return jnp.dot(x.astype(BF16), w_bd, preferred_element_type=F32)


def _mm_nt(x, w_bd):
    return lax.dot_general(x.astype(BF16), w_bd, (((1,), (1,)), ((), ())),
                           preferred_element_type=F32)


def _seg_sum(x, m):
    hi = x.astype(BF16)
    lo = (x - hi.astype(F32)).astype(BF16)
    s = jnp.dot(jnp.concatenate([hi, lo], axis=0), m["bd"], preferred_element_type=F32)
    return s[:x.shape[0]] + s[x.shape[0]:]


def _run_interleaved(*gens):
    gens = list(gens)
    while gens:
        for g in list(gens):
            try:
                next(g)
            except StopIteration:
                gens.remove(g)


def _unit_lower_inverse(n_mats, m):
    row, col = m["row"], m["col"]
    eye = jnp.where(row == col, 1.0, 0.0)
    same16 = (row // 16) == (col // 16)
    c = WKV_CHUNK
    pws = [jnp.where(same16, n, 0.0) for n in n_mats]
    ts = [eye + p for p in pws]
    pws = [_mm(p, _bd(p, m)) for p in pws]
    yield
    for lvl in range(3):
        pw_bds = [_bd(p, m) for p in pws]
        if lvl < 2:
            both = [_mm(jnp.concatenate([t, p], axis=0), b) for t, p, b in zip(ts, pws, pw_bds)]
            ts = [t + bo[:c] for t, bo in zip(ts, both)]
            pws = [bo[c:] for bo in both]
        else:
            ts = [t + _mm(t, b) for t, b in zip(ts, pw_bds)]
        yield
    blk = 16
    while blk < WKV_CHUNK:
        same_lo = (row // blk) == (col // blk)
        same_hi = (row // (2 * blk)) == (col // (2 * blk))
        sel = jnp.logical_and(same_hi, jnp.logical_not(same_lo))
        halves = [_mm(t, _bd(jnp.where(sel, n, 0.0), m)) for t, n in zip(ts, n_mats)]
        yield
        ts = [t + _mm(h, _bd(t, m)) for t, h in zip(ts, halves)]
        yield
        blk *= 2
    return ts


def _wkv_products(probs, m, res):
    c = WKV_CHUNK
    ts = yield from _unit_lower_inverse([p["a_ab"] for p in probs], m)
    vg_bds = [_bd(p["vg"], m) for p in probs]
    akv_yvs = [_mm(p["ak"], vb) for p, vb in zip(probs, vg_bds)]
    yield
    t_bfs = [t.astype(BF16) for t in ts]
    xas = [_mm(t, _bd(p["a0"], m)) for t, p in zip(t_bfs, probs)]
    yield
    res["xvs"] = [_mm(t, _bd(ay[:c], m)) for t, ay in zip(t_bfs, akv_yvs)]
    res["xrs"] = [jnp.concatenate([xa, p["r0"]], axis=0).astype(BF16) for xa, p in zip(xas, probs)]
    res["yvs"] = [ay[c:] for ay in akv_yvs]
    yield


def _wkv_recurrence(probs, res, decays, states, ys, m):
    c = WKV_CHUNK
    n_groups = len(states)
    for c0 in range(0, len(probs), n_groups):
        us = []
        for gi in range(n_groups):
            i = c0 + gi
            uy = _mm(res["xrs"][i], _bd(states[gi], m))
            us.append((uy[:c] + res["xvs"][i], uy[c:]))
        yield
        for gi in range(n_groups):
            i = c0 + gi
            p = probs[i]
            u, y0 = us[gi]
            ys.append(y0 + _mm(p["a_rb"], _bd(u, m)) + res["yvs"][i])
            uv = jnp.concatenate([u, p["vg"]], axis=0).astype(BF16)
            uv_bd = jnp.concatenate([uv] * WKV_GROUP, axis=0) * m["bd2"]
            states[gi] = states[gi] * decays[c0 + gi] + jnp.dot(
                p["bk_t"], uv_bd, preferred_element_type=F32)
        yield


def _wkv_prepare(r, k, v, lw, cum, kn, a, m):
    c = r.shape[0]
    mid = cum[c // 2 - 1:c // 2, :]
    end = cum[c - 1:c, :]
    e_abs = jnp.exp(cum)
    e_abs_prev = jnp.exp(cum - lw)
    e_mid = jnp.exp(-mid)
    e_neg = jnp.exp(mid - cum)
    e_end = jnp.exp(end - cum)
    g_end = jnp.exp(end)
    bvec = kn * a
    r0 = r * e_abs
    a0 = -kn * e_abs_prev
    rt = r0 * e_mid
    at = a0 * e_mid
    kt = k * e_neg
    bt = bvec * e_neg
    kh = k * e_end
    bh = bvec * e_end
    out = []
    for gi in range(N_RWKV_HEADS // WKV_GROUP):
        sl = slice(gi * GW, (gi + 1) * GW)
        lhs = jnp.concatenate([at[:, sl], rt[:, sl]], axis=0)
        ab = jnp.where(m["strict_incl"], _mm_nt(lhs, _bd(bt[:, sl], m)), 0.0)
        ak = jnp.where(m["strict_incl"], _mm_nt(lhs, _bd(kt[:, sl], m)), 0.0)
        bk_t = jnp.transpose(jnp.concatenate([bh[:, sl], kh[:, sl]], axis=0))
        bk_t = jnp.concatenate([bk_t[hh * HEAD_DIM:(hh + 1) * HEAD_DIM] for hh in range(WKV_GROUP)],
                               axis=1).astype(BF16)
        g_diag = jnp.where(m["row"] == m["col"], g_end[:, sl], 0.0)
        out.append(dict(a_ab=ab[:c], a_rb=ab[c:], ak=ak, vg=v[:, sl], a0=a0[:, sl], r0=r0[:, sl],
                        g_diag=g_diag, bk_t=bk_t))
    return out


def _wkv_kernel(r_ref, k_ref, v_ref, lw_ref, kn_ref, a_ref, g_ref, rk_ref, lg_ref, lb_ref,
                o_ref, s_scr, *, n_chunks):
    @pl.when(pl.program_id(1) == 0)
    def _():
        s_scr[...] = jnp.zeros_like(s_scr)

    n_groups = N_RWKV_HEADS // WKV_GROUP
    m = _wkv_masks()
    chunk_rows = [slice(ci * WKV_CHUNK, (ci + 1) * WKV_CHUNK) for ci in range(n_chunks)]

    c = WKV_CHUNK
    lw_all = lw_ref[...]
    cum_all = _chunk_cumsum(lw_all, n_chunks)
    probs = []
    for rows in chunk_rows:
        probs += _wkv_prepare(r_ref[rows, :], k_ref[rows, :], v_ref[rows, :], lw_all[rows, :],
                              cum_all[rows, :], kn_ref[rows, :], a_ref[rows, :], m)
    n_probs = len(probs)
    split = lambda stacked: [stacked[i * c:(i + 1) * c] for i in range(n_probs)]
    decays = split(_seg_sum(jnp.concatenate([p["g_diag"] for p in probs], axis=0), m))
    rkr = r_ref[...] * k_ref[...] * rk_ref[...]
    bonus_w = split(_seg_sum(jnp.concatenate(
        [rkr[rows, gi * GW:(gi + 1) * GW] for rows in chunk_rows for gi in range(n_groups)], axis=0), m))

    h = (n_chunks // 2) * n_groups
    res_a, res_b = {}, {}
    states = [s_scr[gi] for gi in range(n_groups)]
    ys = []
    _run_interleaved(_wkv_products(probs[:h], m, res_a))
    _run_interleaved(_wkv_recurrence(probs[:h], res_a, decays[:h], states, ys, m),
                     _wkv_products(probs[h:], m, res_b))
    _run_interleaved(_wkv_recurrence(probs[h:], res_b, decays[h:], states, ys, m))
    for gi in range(n_groups):
        s_scr[gi] = states[gi]

    y_all = jnp.concatenate(ys, axis=0)
    d_all = y_all - _seg_sum(y_all, m) * (1.0 / HEAD_DIM)
    inv_all = lax.rsqrt(_seg_sum(d_all * d_all, m) * (1.0 / HEAD_DIM) + RWKV_GN_EPS)
    for ci, rows in enumerate(chunk_rows):
        for gi in range(n_groups):
            i = ci * n_groups + gi
            sl = slice(gi * GW, (gi + 1) * GW)
            yn = d_all[i * c:(i + 1) * c] * inv_all[i * c:(i + 1) * c] * lg_ref[:, sl] + lb_ref[:, sl]
            o_ref[rows, sl] = ((yn + bonus_w[i] * probs[i]["vg"]) * g_ref[rows, sl]).astype(BF16)


def _wkv(r, k, v, lw, kn, a, g, r_k, lnx_g, lnx_b, *, batch, seq, rows):
    T = r.shape[0]
    nb = seq // rows
    blk = pl.BlockSpec((rows, D_RWKV), lambda b, i: (b * nb + i, 0))
    par = pl.BlockSpec((1, D_RWKV), lambda b, i: (0, 0))
    return pl.pallas_call(
        functools.partial(_wkv_kernel, n_chunks=rows // WKV_CHUNK),
        grid=(batch, nb),
        in_specs=[blk] * 7 + [par] * 3,
        out_specs=blk,
        out_shape=jax.ShapeDtypeStruct((T, D_RWKV), BF16),
        scratch_shapes=[pltpu.VMEM((N_RWKV_HEADS // WKV_GROUP, HEAD_DIM, GW), F32)],
        compiler_params=_cparams(("parallel", "arbitrary")),
        name="wkv7",
    )(r, k, v, lw, kn, a, g, r_k.reshape(1, -1), lnx_g.reshape(1, -1), lnx_b.reshape(1, -1))


def _dattn_kernel(q_ref, k_ref, v_ref, lq1_ref, lk1_ref, lq2_ref, lk2_ref, sg_ref, o_ref,
                  qs_scr, m_scr, l_scr, acc_scr, s0_scr, s1_scr, p0_scr, p1_scr, al0_scr, al1_scr,
                  *, tq, lam_init):
    qi = pl.program_id(2)
    q = q_ref[...]
    lane = lax.broadcasted_iota(jnp.int32, q.shape, 1)
    zero = jnp.zeros_like(q)
    qs_scr[:tq, :] = jnp.where(lane < HEAD_DIM, q, zero)
    qs_scr[tq:, :] = jnp.where(lane < HEAD_DIM, zero, q)
    m_scr[...] = jnp.full_like(m_scr, -jnp.inf)
    l_scr[...] = jnp.zeros_like(l_scr)
    acc_scr[...] = jnp.zeros_like(acc_scr)

    bufs = ((s0_scr, p0_scr, al0_scr), (s1_scr, p1_scr, al1_scr))

    def scores(j, par):
        start = pl.multiple_of(j * tq, tq)
        bufs[par][0][...] = _dot_nt(qs_scr[...], k_ref[pl.ds(start, tq), :]).astype(BF16)

    def softmax(par, masked):
        s_ref, p_ref, al_ref = bufs[par]
        s = s_ref[...]
        if masked:
            r_pos = lax.broadcasted_iota(jnp.int32, s.shape, 0) % tq
            c_pos = lax.broadcasted_iota(jnp.int32, s.shape, 1)
            s = jnp.where(c_pos <= r_pos, s, jnp.full_like(s, -jnp.inf))
        tiles = [s[:, c * LANES:(c + 1) * LANES] for c in range(tq // LANES)]
        mx = functools.reduce(jnp.maximum, tiles)
        m_prev = m_scr[...]
        m_new = jnp.maximum(m_prev, jnp.max(mx.astype(F32), axis=-1, keepdims=True))
        alpha = jnp.exp2(m_prev - m_new)
        mb = m_new.astype(BF16)
        ps = [jnp.exp2(t - mb) for t in tiles]
        part = functools.reduce(jnp.add, ps).astype(F32)
        l_scr[...] = alpha * l_scr[...] + jnp.sum(part, axis=-1, keepdims=True)
        m_scr[...] = m_new
        for c, p in enumerate(ps):
            p_ref[:, c * LANES:(c + 1) * LANES] = p
        al_ref[...] = alpha

    def values(j, par):
        _, p_ref, al_ref = bufs[par]
        start = pl.multiple_of(j * tq, tq)
        acc_scr[...] = al_ref[...] * acc_scr[...] + jnp.dot(
            p_ref[...], v_ref[pl.ds(start, tq), :], preferred_element_type=F32)

    scores(0, 0)

    @pl.when(qi == 0)
    def _():
        softmax(0, True)
        values(0, 0)

    @pl.when(qi > 0)
    def _():
        scores(1, 1)
        softmax(0, False)

        def body(i, carry):
            t = 2 + 2 * i
            values(t - 2, 0)
            scores(t, 0)
            softmax(1, False)
            values(t - 1, 1)
            scores(t + 1, 1)
            softmax(0, False)
            return carry

        lax.fori_loop(0, (qi - 1) // 2, body, 0)

        @pl.when(qi % 2 == 1)
        def _():
            values(qi - 1, 0)
            softmax(1, True)
            values(qi, 1)

        @pl.when(qi % 2 == 0)
        def _():
            values(qi - 2, 0)
            scores(qi, 0)
            softmax(1, False)
            values(qi - 1, 1)
            softmax(0, True)
            values(qi, 0)

    lam = (jnp.exp(jnp.sum(lq1_ref[...] * lk1_ref[...], axis=-1, keepdims=True))
           - jnp.exp(jnp.sum(lq2_ref[...] * lk2_ref[...], axis=-1, keepdims=True)) + lam_init)
    on = acc_scr[...] / l_scr[...]
    o = on[:tq, :] - lam * on[tq:, :]
    o = o * lax.rsqrt(jnp.mean(o * o, axis=-1, keepdims=True) + 1e-5) * sg_ref[...]
    o_ref[...] = (o * (1.0 - lam_init)).astype(BF16)


def _dattn(q, k, v, lq1, lk1, lq2, lk2, subln_g, *, batch, seq, tq, lam_init):
    T = q.shape[0]
    nq = seq // tq
    hd = 2 * HEAD_DIM
    qmap = lambda b, h, i: (b * nq + i, h)
    kvmap = lambda b, h, i: (b, h)
    par = lambda b, h, i: (0, 0)
    vec = lambda a: a.reshape(1, -1)
    return pl.pallas_call(
        functools.partial(_dattn_kernel, tq=tq, lam_init=lam_init),
        grid=(batch, N_DIFF_HEADS, nq),
        in_specs=[
            pl.BlockSpec((tq, hd), qmap),
            pl.BlockSpec((seq, hd), kvmap),
            pl.BlockSpec((seq, hd), kvmap),
            pl.BlockSpec((1, HEAD_DIM), par),
            pl.BlockSpec((1, HEAD_DIM), par),
            pl.BlockSpec((1, HEAD_DIM), par),
            pl.BlockSpec((1, HEAD_DIM), par),
            pl.BlockSpec((1, hd), par),
        ],
        out_specs=pl.BlockSpec((tq, hd), qmap),
        out_shape=jax.ShapeDtypeStruct((T, D_DIFF), BF16),
        scratch_shapes=[
            pltpu.VMEM((2 * tq, hd), BF16),
            pltpu.VMEM((2 * tq, LANES), F32),
            pltpu.VMEM((2 * tq, LANES), F32),
            pltpu.VMEM((2 * tq, hd), F32),
            pltpu.VMEM((2 * tq, tq), BF16),
            pltpu.VMEM((2 * tq, tq), BF16),
            pltpu.VMEM((2 * tq, tq), BF16),
            pltpu.VMEM((2 * tq, tq), BF16),
            pltpu.VMEM((2 * tq, LANES), F32),
            pltpu.VMEM((2 * tq, LANES), F32),
        ],
        compiler_params=_cparams(("parallel", "parallel", "arbitrary")),
        name="diff_attn",
    )(q, k, v, vec(lq1), vec(lk1), vec(lq2), vec(lk2), vec(subln_g))


def _proj_ln_kernel(yr_ref, yd_ref, w_ref, x_ref, g_ref, b_ref, o_ref):
    mix = jnp.dot(yr_ref[...], w_ref[:D_RWKV, :], preferred_element_type=F32)
    mix = mix + jnp.dot(yd_ref[...], w_ref[D_RWKV:, :], preferred_element_type=F32)
    o_ref[...] = _layer_norm(DEEPNORM_ALPHA * x_ref[...] + mix, g_ref[...], b_ref[...])


def _proj_ln(yr, yd, w_out, x2, g, b, *, tm):
    T = x2.shape[0]
    row = lambda i: (i, 0)
    full = lambda i: (0, 0)
    return pl.pallas_call(
        _proj_ln_kernel,
        grid=(T // tm,),
        in_specs=[
            pl.BlockSpec((tm, D_RWKV), row),
            pl.BlockSpec((tm, D_DIFF), row),
            pl.BlockSpec((D_RWKV + D_DIFF, D_MODEL), full),
            pl.BlockSpec((tm, D_MODEL), row),
            pl.BlockSpec((1, D_MODEL), full),
            pl.BlockSpec((1, D_MODEL), full),
        ],
        out_specs=pl.BlockSpec((tm, D_MODEL), row),
        out_shape=jax.ShapeDtypeStruct((T, D_MODEL), F32),
        compiler_params=_cparams(("parallel",)),
        name="ab_out_ln",
    )(yr, yd, w_out, x2, g.reshape(1, -1), b.reshape(1, -1))


def _gmlp_kernel(x_ref, win_ref, bin_ref, lng_ref, lnb_ref, ws_ref, bs_ref, wout_ref, g_ref, b_ref,
                 o_ref, gated_scr, *, tm):
    x = x_ref[...]
    xb = x.astype(BF16)

    def gelu(h):
        return 0.5 * h * (1.0 + lax.erf(h * (0.5 ** 0.5)))

    hv = jnp.dot(xb, win_ref[:, D_GMLP:], preferred_element_type=F32) + bin_ref[:, D_GMLP:]
    hu = jnp.dot(xb, win_ref[:, :D_GMLP], preferred_element_type=F32) + bin_ref[:, :D_GMLP]
    v = _layer_norm(gelu(hv), lng_ref[...], lnb_ref[...]).astype(BF16)
    u = gelu(hu)
    row = lax.broadcasted_iota(jnp.int32, (CHUNK, CHUNK), 0)
    col = lax.broadcasted_iota(jnp.int32, (CHUNK, CHUNK), 1)
    gw = D_GMLP // GMLP_GROUPS
    n_chunks = tm // CHUNK
    for gi in range(GMLP_GROUPS):
        cs = slice(gi * gw, (gi + 1) * gw)
        ws = jnp.where(row >= col, ws_ref[gi], 0.0).astype(BF16)
        v_side = jnp.concatenate([v[c * CHUNK:(c + 1) * CHUNK, cs] for c in range(n_chunks)], axis=1)
        mixed = jnp.dot(ws, v_side, preferred_element_type=F32)
        for c in range(n_chunks):
            rs = slice(c * CHUNK, (c + 1) * CHUNK)
            gated_scr[rs, cs] = (u[rs, cs] * (mixed[:, c * gw:(c + 1) * gw] + bs_ref[gi])).astype(BF16)
    mix = jnp.dot(gated_scr[...], wout_ref[...], preferred_element_type=F32)
    o_ref[...] = _layer_norm(DEEPNORM_ALPHA * x + mix, g_ref[...], b_ref[...])


def _gmlp(x2, w_in, b_in, ln_g, ln_b, w_s, b_s, w_out, g, b, *, tm):
    T = x2.shape[0]
    row = lambda i: (i, 0)
    full = lambda i: (0, 0)
    full3 = lambda i: (0, 0, 0)
    vec = lambda a: a.reshape(1, -1)
    gw = D_GMLP // GMLP_GROUPS
    bs_b = jnp.broadcast_to(b_s[:, :, None], (GMLP_GROUPS, CHUNK, gw))
    return pl.pallas_call(
        functools.partial(_gmlp_kernel, tm=tm),
        grid=(T // tm,),
        in_specs=[
            pl.BlockSpec((tm, D_MODEL), row),
            pl.BlockSpec((D_MODEL, 2 * D_GMLP), full),
            pl.BlockSpec((1, 2 * D_GMLP), full),
            pl.BlockSpec((1, D_GMLP), full),
            pl.BlockSpec((1, D_GMLP), full),
            pl.BlockSpec((GMLP_GROUPS, CHUNK, CHUNK), full3),
            pl.BlockSpec((GMLP_GROUPS, CHUNK, gw), full3),
            pl.BlockSpec((D_GMLP, D_MODEL), full),
            pl.BlockSpec((1, D_MODEL), full),
            pl.BlockSpec((1, D_MODEL), full),
        ],
        out_specs=pl.BlockSpec((tm, D_MODEL), row),
        out_shape=jax.ShapeDtypeStruct((T, D_MODEL), F32),
        scratch_shapes=[pltpu.VMEM((tm, D_GMLP), BF16)],
        compiler_params=_cparams(("parallel",)),
        name="gmlp",
    )(x2, w_in, vec(b_in), vec(ln_g), vec(ln_b), w_s, bs_b, w_out, vec(g), vec(b))


def _ffn_kernel(x_ref, xh_ref, wup_ref, cw_ref, cb_ref, wd_ref, g_ref, b_ref, o_ref,
                *, blocks_per_seq, tf, n_slabs):
    x = x_ref[...]
    tm = x.shape[0]
    xe = jnp.concatenate([x, xh_ref[...]], axis=0).astype(BF16)
    xb = xe[:tm]
    seq_start = pl.program_id(0) % blocks_per_seq == 0

    def up(j):
        cols = slice(j * tf, (j + 1) * tf)
        ge = jnp.dot(xe, wup_ref[:, cols], preferred_element_type=F32)
        val = jnp.dot(xb, wup_ref[:, D_FF + j * tf:D_FF + (j + 1) * tf], preferred_element_type=F32)
        return ge[:tm], val, jnp.where(seq_start, 0.0, ge[tm:])

    acc = None
    nxt = up(0)
    for j in range(n_slabs):
        gate, val, gh = nxt
        if j + 1 < n_slabs:
            nxt = up(j + 1)
        cols = slice(j * tf, (j + 1) * tf)
        cw = cw_ref[:, cols]
        conv = (cb_ref[:, cols] + cw[0:1, :] * _shift_rows(gate, 2, gh)
                + cw[1:2, :] * _shift_rows(gate, 1, gh) + cw[2:3, :] * gate)
        hid = conv * jax.nn.sigmoid(conv) * val
        down = jnp.dot(hid.astype(BF16), wd_ref[j], preferred_element_type=F32)
        acc = down if acc is None else acc + down
    o_ref[...] = _layer_norm(DEEPNORM_ALPHA * x + acc, g_ref[...], b_ref[...])


def _ffn(x2, w_up, conv_w, conv_b, w_down, g, b, *, seq, tm, tf):
    T = x2.shape[0]
    n_slabs = D_FF // tf
    bps = seq // tm
    wup = w_up.astype(BF16)
    cb = conv_b.reshape(1, D_FF)
    wd = w_down.reshape(n_slabs, tf, D_MODEL).astype(BF16)
    row = lambda i: (i, 0)
    full = lambda i: (0, 0)
    full3 = lambda i: (0, 0, 0)
    halo = lambda i: (jnp.maximum(i * (tm // HALO_ROWS) - 1, 0), 0)
    once = pl.Buffered(1)
    return pl.pallas_call(
        functools.partial(_ffn_kernel, blocks_per_seq=bps, tf=tf, n_slabs=n_slabs),
        grid=(T // tm,),
        in_specs=[
            pl.BlockSpec((tm, D_MODEL), row),
            pl.BlockSpec((HALO_ROWS, D_MODEL), halo),
            pl.BlockSpec((D_MODEL, 2 * D_FF), full, pipeline_mode=once),
            pl.BlockSpec((3, D_FF), full, pipeline_mode=once),
            pl.BlockSpec((1, D_FF), full, pipeline_mode=once),
            pl.BlockSpec((n_slabs, tf, D_MODEL), full3, pipeline_mode=once),
            pl.BlockSpec((1, D_MODEL), full),
            pl.BlockSpec((1, D_MODEL), full),
        ],
        out_specs=pl.BlockSpec((tm, D_MODEL), row),
        out_shape=jax.ShapeDtypeStruct((T, D_MODEL), F32),
        compiler_params=_cparams(("parallel",)),
        name="conv_ffn",
    )(x2, x2, wup, conv_w, cb, wd, g.reshape(1, -1), b.reshape(1, -1))


def _rope_tables(seq):
    half = ROPE_DIM // 2
    inv_freq = ROPE_THETA ** (-jnp.arange(0, ROPE_DIM, 2, dtype=F32) / ROPE_DIM)
    ang = jnp.arange(seq, dtype=F32)[:, None] * inv_freq[None, :]
    cos, sin = jnp.cos(ang), jnp.sin(ang)
    ones = jnp.ones((seq, HEAD_DIM - ROPE_DIM), F32)
    zeros = jnp.zeros((seq, HEAD_DIM - ROPE_DIM), F32)
    zh = jnp.zeros((seq, half), F32)
    c64 = jnp.concatenate([cos, cos, ones], axis=1)
    s1_64 = jnp.concatenate([zh, sin, zeros], axis=1)
    s2_64 = jnp.concatenate([-sin, zh, zeros], axis=1)
    two = lambda t: jnp.concatenate([t, t], axis=1)
    return two(c64), two(s1_64), two(s2_64)


def kernel(x, ab_w_in, ab_shift_mu, ab_w0, ab_w2, ab_a0, ab_a2, ab_g2, ab_k_k, ab_k_a, ab_r_k, ab_lnx_g, ab_lnx_b, ab_lam_q1, ab_lam_k1, ab_lam_q2, ab_lam_k2, ab_subln_g, ab_w_out, c_w_in, c_b_in, c_ln_g, c_ln_b, c_w_s, c_b_s, c_w_out, ln1_g, ln1_b, ffn_w_up, ffn_conv_w, ffn_conv_b, ffn_w_down, ln2_g, ln2_b):
    batch, seq, _ = x.shape
    x2 = x.reshape(batch * seq, D_MODEL)
    rc, rs1, rs2 = _rope_tables(seq)
    tm_in = min(512, seq)
    tm_ffn = min(512, seq)
    tm_gmlp = min(512, seq)
    tq = min(512, seq)
    wkv_rows = min(512, seq)
    for i in range(DEPTH):
        j = i // 2
        if i % 2 == 0:
            (r, k, v, lw, kn, a, g, qd, kd, vd) = _ab_in(
                x2, ab_w_in[j].astype(BF16), ab_shift_mu[j], ab_w0[j], ab_w2[j], ab_a0[j],
                ab_a2[j].astype(BF16), ab_g2[j].astype(BF16), ab_k_k[j], ab_k_a[j], rc, rs1, rs2,
                seq=seq, tm=tm_in)
            y_r = _wkv(r, k, v, lw, kn, a, g, ab_r_k[j], ab_lnx_g[j], ab_lnx_b[j],
                       batch=batch, seq=seq, rows=wkv_rows)
            lam_init = 0.8 - 0.6 * math.exp(-0.3 * i)
            y_d = _dattn(qd, kd, vd, ab_lam_q1[j], ab_lam_k1[j], ab_lam_q2[j], ab_lam_k2[j],
                         ab_subln_g[j], batch=batch, seq=seq, tq=tq, lam_init=lam_init)
            x2 = _proj_ln(y_r, y_d, ab_w_out[j].astype(BF16), x2, ln1_g[i], ln1_b[i], tm=tm_in)
        else:
            x2 = _gmlp(x2, c_w_in[j].astype(BF16), c_b_in[j], c_ln_g[j], c_ln_b[j], c_w_s[j],
                       c_b_s[j], c_w_out[j].astype(BF16), ln1_g[i], ln1_b[i], tm=tm_gmlp)
        x2 = _ffn(x2, ffn_w_up[i], ffn_conv_w[i], ffn_conv_b[i], ffn_w_down[i], ln2_g[i], ln2_b[i],
                  seq=seq, tm=tm_ffn, tf=256)
    return x2.reshape(batch, seq, D_MODEL)
```

```python
import functools
import math

import jax
import jax.numpy as jnp
from jax import lax
from jax.experimental import pallas as pl
from jax.experimental.pallas import tpu as pltpu

F32 = jnp.float32
BF16 = jnp.bfloat16
HIGHEST = lax.Precision.HIGHEST

D_MODEL = 1024
HEAD_DIM = 64
N_RWKV_HEADS = 8
D_RWKV = N_RWKV_HEADS * HEAD_DIM
N_DIFF_HEADS = 4
D_DIFF = N_DIFF_HEADS * 2 * HEAD_DIM
DECAY_LORA = 64
AAA_LORA = 64
GATE_LORA = 128
RWKV_COLS = 3 * D_RWKV + DECAY_LORA + AAA_LORA + GATE_LORA
AB_COLS = RWKV_COLS + 3 * D_DIFF
RWKV_GN_EPS = 64e-5
ROPE_THETA = 500000.0
ROPE_DIM = HEAD_DIM // 4
CHUNK = 128
GMLP_GROUPS = 8
D_GMLP = D_MODEL
D_FF = 2816
DEPTH = 4
DEEPNORM_ALPHA = (2 * DEPTH) ** 0.25

LANES = 128
HALO_ROWS = 16
WKV_CHUNK = 64
VMEM_LIMIT = 56 * 1024 * 1024
LOG2E = math.log2(math.e)


def _cparams(sem):
    return pltpu.CompilerParams(dimension_semantics=sem, vmem_limit_bytes=VMEM_LIMIT)


def _dot(a, b):
    return jnp.dot(a.astype(BF16), b.astype(BF16), preferred_element_type=F32)


def _dot_nt(a, b):
    return lax.dot_general(a.astype(BF16), b.astype(BF16), (((1,), (1,)), ((), ())),
                           preferred_element_type=F32)


def _layer_norm(z, g, b, eps=1e-5):
    mu = jnp.mean(z, axis=-1, keepdims=True)
    d = z - mu
    var = jnp.mean(d * d, axis=-1, keepdims=True)
    return d * lax.rsqrt(var + eps) * g + b


def _shift_rows(t, k, halo):
    rolled = pltpu.roll(t, k, 0)
    row = lax.broadcasted_iota(jnp.int32, t.shape, 0)
    out = rolled
    for j in range(k):
        out = jnp.where(row == j, halo[HALO_ROWS - k + j:HALO_ROWS - k + j + 1, :], out)
    return out


def _ab_in_kernel(x_ref, xh_ref, w_ref, mu_ref, w0_ref, w2_ref, a0_ref, a2_ref, g2_ref, kk_ref,
                  ka_ref, rc_ref, rs1_ref, rs2_ref,
                  r_out, k_out, v_out, lw_out, kn_out, a_out, g_out, q_out, kd_out, vd_out,
                  *, blocks_per_seq):
    seq_start = pl.program_id(0) % blocks_per_seq == 0
    tm = x_ref.shape[0]
    xe = jnp.concatenate([x_ref[...], xh_ref[...]], axis=0).astype(BF16)
    xb = xe[:tm]

    def proj(c0, c1):
        return jnp.dot(xb, w_ref[:, c0:c1], preferred_element_type=F32)

    def proj_shifted(c0, c1):
        pe = jnp.dot(xe, w_ref[:, c0:c1], preferred_element_type=F32)
        p = pe[:tm]
        ph = jnp.where(seq_start, 0.0, pe[tm:])
        return p + mu_ref[:, c0:c1] * (_shift_rows(p, 1, ph) - p)

    o = 3 * D_RWKV
    pq = proj(RWKV_COLS, RWKV_COLS + D_DIFF)
    pk = proj(RWKV_COLS + D_DIFF, RWKV_COLS + 2 * D_DIFF)
    xl = proj_shifted(o, RWKV_COLS)
    k = proj_shifted(D_RWKV, 2 * D_RWKV)
    r = proj_shifted(0, D_RWKV)
    v = proj_shifted(2 * D_RWKV, o)
    vd_out[...] = proj(RWKV_COLS + 2 * D_DIFF, AB_COLS).astype(BF16)
    xw = xl[:, :DECAY_LORA]
    xa = xl[:, DECAY_LORA:DECAY_LORA + AAA_LORA]
    xg = xl[:, DECAY_LORA + AAA_LORA:]

    rc = jnp.concatenate([rc_ref[...]] * (D_DIFF // LANES), axis=1)
    rs1 = jnp.concatenate([rs1_ref[...]] * (D_DIFF // LANES), axis=1)
    rs2 = jnp.concatenate([rs2_ref[...]] * (D_DIFF // LANES), axis=1)
    half = ROPE_DIM // 2

    def rope(t):
        return t * rc + pltpu.roll(t, half, 1) * rs1 + pltpu.roll(t, D_DIFF - half, 1) * rs2

    q_out[...] = (rope(pq) * (HEAD_DIM ** -0.5 * LOG2E)).astype(BF16)
    kd_out[...] = rope(pk).astype(BF16)

    z = w0_ref[...] + jnp.dot(jnp.tanh(xw), w2_ref[...], precision=HIGHEST,
                              preferred_element_type=F32)
    softplus_neg = jnp.maximum(-z, 0.0) + jnp.log1p(jnp.exp(-jnp.abs(z)))
    w = -softplus_neg - 0.5
    lw_out[...] = -jnp.exp(w)
    a = jax.nn.sigmoid(a0_ref[...] + _dot(xa, a2_ref[...]))
    a_out[...] = a
    g_out[...] = _dot(jax.nn.sigmoid(xg), g2_ref[...])

    kx = k * kk_ref[...]
    sq = kx * kx
    lane = lax.broadcasted_iota(jnp.int32, (sq.shape[0], LANES), 1)
    lo = lane < HEAD_DIM
    for c in range(D_RWKV // LANES):
        blk = sq[:, c * LANES:(c + 1) * LANES]
        n_lo = jnp.sqrt(jnp.sum(jnp.where(lo, blk, 0.0), axis=-1, keepdims=True))
        n_hi = jnp.sqrt(jnp.sum(jnp.where(lo, 0.0, blk), axis=-1, keepdims=True))
        norm = jnp.maximum(jnp.where(lo, n_lo, n_hi), 1e-12)
        kn_out[:, c * LANES:(c + 1) * LANES] = kx[:, c * LANES:(c + 1) * LANES] / norm
    r_out[...] = r
    k_out[...] = k * (1.0 + (a - 1.0) * ka_ref[...])
    v_out[...] = v


def _ab_in(x2, w_in, mu, w0, w2, a0, a2, g2, k_k, k_a, rc, rs1, rs2, *, seq, tm):
    T = x2.shape[0]
    n = T // tm
    bps = seq // tm
    row = lambda i: (i, 0)
    full = lambda i: (0, 0)
    halo = lambda i: (jnp.maximum(i * (tm // HALO_ROWS) - 1, 0), 0)
    rope_map = lambda i: (i % bps, 0)
    vec = lambda a: a.reshape(1, -1)
    f32_out = jax.ShapeDtypeStruct((T, D_RWKV), F32)
    bf_out = jax.ShapeDtypeStruct((T, D_DIFF), BF16)
    out_spec = pl.BlockSpec((tm, D_RWKV), row)
    return pl.pallas_call(
        functools.partial(_ab_in_kernel, blocks_per_seq=bps),
        grid=(n,),
        in_specs=[
            pl.BlockSpec((tm, D_MODEL), row),
            pl.BlockSpec((HALO_ROWS, D_MODEL), halo),
            pl.BlockSpec((D_MODEL, AB_COLS), full),
            pl.BlockSpec((1, RWKV_COLS), full),
            pl.BlockSpec((1, D_RWKV), full),
            pl.BlockSpec((DECAY_LORA, D_RWKV), full),
            pl.BlockSpec((1, D_RWKV), full),
            pl.BlockSpec((AAA_LORA, D_RWKV), full),
            pl.BlockSpec((GATE_LORA, D_RWKV), full),
            pl.BlockSpec((1, D_RWKV), full),
            pl.BlockSpec((1, D_RWKV), full),
            pl.BlockSpec((tm, LANES), rope_map),
            pl.BlockSpec((tm, LANES), rope_map),
            pl.BlockSpec((tm, LANES), rope_map),
        ],
        out_specs=[out_spec] * 10,
        out_shape=[f32_out] * 7 + [bf_out] * 3,
        compiler_params=_cparams(("parallel",)),
        name="ab_in",
    )(x2, x2, w_in, vec(mu), vec(w0), w2, vec(a0), a2, g2, vec(k_k), vec(k_a), rc, rs1, rs2)


WKV_GROUP = 4
GW = WKV_GROUP * HEAD_DIM


def _wkv_masks():
    c = WKV_CHUNK
    row = lax.broadcasted_iota(jnp.int32, (c, GW), 0)
    col = lax.broadcasted_iota(jnp.int32, (c, GW), 1) % HEAD_DIM
    brow = lax.broadcasted_iota(jnp.int32, (GW, GW), 0) // HEAD_DIM
    bcol = lax.broadcasted_iota(jnp.int32, (GW, GW), 1) // HEAD_DIM
    bd = jnp.where(brow == bcol, 1.0, 0.0).astype(BF16)
    brow2 = lax.broadcasted_iota(jnp.int32, (2 * c * WKV_GROUP, GW), 0) // (2 * c)
    bcol2 = lax.broadcasted_iota(jnp.int32, (2 * c * WKV_GROUP, GW), 1) // HEAD_DIM
    bd2 = jnp.where(brow2 == bcol2, 1.0, 0.0).astype(BF16)
    strict_incl = jnp.concatenate([row > col, row >= col], axis=0)
    return dict(row=row, col=col, bd=bd, bd2=bd2, strict_incl=strict_incl)


def _chunk_cumsum(lw, n_chunks):
    n = 2 * WKV_CHUNK
    r = lax.broadcasted_iota(jnp.int32, (n, n), 0)
    c = lax.broadcasted_iota(jnp.int32, (n, n), 1)
    tri = jnp.where(jnp.logical_and(r // WKV_CHUNK == c // WKV_CHUNK, r >= c), 1.0, 0.0).astype(BF16)
    h1 = lw.astype(BF16)
    rem = lw - h1.astype(F32)
    h2 = rem.astype(BF16)
    h3 = (rem - h2.astype(F32)).astype(BF16)
    out = []
    for b in range(n_chunks // 2):
        rows = slice(b * n, (b + 1) * n)
        cum = jnp.dot(tri, h1[rows], preferred_element_type=F32)
        cum = cum + jnp.dot(tri, h2[rows], preferred_element_type=F32)
        out.append(cum + jnp.dot(tri, h3[rows], preferred_element_type=F32))
    return jnp.concatenate(out, axis=0)


def _bd(w, m):
    return jnp.concatenate([w.astype(BF16)] * WKV_GROUP, axis=0) * m["bd"]


def _mm(x, w_bd):
    return jnp.dot(x.astype(BF16), w_bd, preferred_element_type=F32)


def _mm_nt(x, w_bd):
    return lax.dot_general(x.astype(BF16), w_bd, (((1,), (1,)), ((), ())),
                           preferred_element_type=F32)


def _seg_sum(x, m):
    hi = x.astype(BF16)
    lo = (x - hi.astype(F32)).astype(BF16)
    s = jnp.dot(jnp.concatenate([hi, lo], axis=0), m["bd"], preferred_element_type=F32)
    return s[:x.shape[0]] + s[x.shape[0]:]


def _run_interleaved(*gens):
    gens = list(gens)
    while gens:
        for g in list(gens):
            try:
                next(g)
            except StopIteration:
                gens.remove(g)


def _unit_lower_inverse(n_mats, m):
    row, col = m["row"], m["col"]
    eye = jnp.where(row == col, 1.0, 0.0)
    same16 = (row // 16) == (col // 16)
    c = WKV_CHUNK
    pws = [jnp.where(same16, n, 0.0) for n in n_mats]
    ts = [eye + p for p in pws]
    pws = [_mm(p, _bd(p, m)) for p in pws]
    yield
    for lvl in range(3):
        pw_bds = [_bd(p, m) for p in pws]
        if lvl < 2:
            both = [_mm(jnp.concatenate([t, p], axis=0), b) for t, p, b in zip(ts, pws, pw_bds)]
            ts = [t + bo[:c] for t, bo in zip(ts, both)]
            pws = [bo[c:] for bo in both]
        else:
            ts = [t + _mm(t, b) for t, b in zip(ts, pw_bds)]
        yield
    blk = 16
    while blk < WKV_CHUNK:
        same_lo = (row // blk) == (col // blk)
        same_hi = (row // (2 * blk)) == (col // (2 * blk))
        sel = jnp.logical_and(same_hi, jnp.logical_not(same_lo))
        halves = [_mm(t, _bd(jnp.where(sel, n, 0.0), m)) for t, n in zip(ts, n_mats)]
        yield
        ts = [t + _mm(h, _bd(t, m)) for t, h in zip(ts, halves)]
        yield
        blk *= 2
    return ts


def _wkv_products(probs, m, res):
    c = WKV_CHUNK
    ts = yield from _unit_lower_inverse([p["a_ab"] for p in probs], m)
    vg_bds = [_bd(p["vg"], m) for p in probs]
    akv_yvs = [_mm(p["ak"], vb) for p, vb in zip(probs, vg_bds)]
    yield
    t_bfs = [t.astype(BF16) for t in ts]
    xas = [_mm(t, _bd(p["a0"], m)) for t, p in zip(t_bfs, probs)]
    yield
    res["xvs"] = [_mm(t, _bd(ay[:c], m)) for t, ay in zip(t_bfs, akv_yvs)]
    res["xrs"] = [jnp.concatenate([xa, p["r0"]], axis=0).astype(BF16) for xa, p in zip(xas, probs)]
    res["yvs"] = [ay[c:] for ay in akv_yvs]
    yield


def _wkv_recurrence(probs, res, decays, states, ys, m):
    c = WKV_CHUNK
    n_groups = len(states)
    for c0 in range(0, len(probs), n_groups):
        us = []
        for gi in range(n_groups):
            i = c0 + gi
            uy = _mm(res["xrs"][i], _bd(states[gi], m))
            us.append((uy[:c] + res["xvs"][i], uy[c:]))
        yield
        for gi in range(n_groups):
            i = c0 + gi
            p = probs[i]
            u, y0 = us[gi]
            ys.append(y0 + _mm(p["a_rb"], _bd(u, m)) + res["yvs"][i])
            uv = jnp.concatenate([u, p["vg"]], axis=0).astype(BF16)
            uv_bd = jnp.concatenate([uv] * WKV_GROUP, axis=0) * m["bd2"]
            states[gi] = states[gi] * decays[c0 + gi] + jnp.dot(
                p["bk_t"], uv_bd, preferred_element_type=F32)
        yield


def _wkv_prepare(r, k, v, lw, cum, kn, a, m):
    c = r.shape[0]
    mid = cum[c // 2 - 1:c // 2, :]
    end = cum[c - 1:c, :]
    e_abs = jnp.exp(cum)
    e_abs_prev = jnp.exp(cum - lw)
    e_mid = jnp.exp(-mid)
    e_neg = jnp.exp(mid - cum)
    e_end = jnp.exp(end - cum)
    g_end = jnp.exp(end)
    bvec = kn * a
    r0 = r * e_abs
    a0 = -kn * e_abs_prev
    rt = r0 * e_mid
    at = a0 * e_mid
    kt = k * e_neg
    bt = bvec * e_neg
    kh = k * e_end
    bh = bvec * e_end
    out = []
    for gi in range(N_RWKV_HEADS // WKV_GROUP):
        sl = slice(gi * GW, (gi + 1) * GW)
        lhs = jnp.concatenate([at[:, sl], rt[:, sl]], axis=0)
        ab = jnp.where(m["strict_incl"], _mm_nt(lhs, _bd(bt[:, sl], m)), 0.0)
        ak = jnp.where(m["strict_incl"], _mm_nt(lhs, _bd(kt[:, sl], m)), 0.0)
        bk_t = jnp.transpose(jnp.concatenate([bh[:, sl], kh[:, sl]], axis=0))
        bk_t = jnp.concatenate([bk_t[hh * HEAD_DIM:(hh + 1) * HEAD_DIM] for hh in range(WKV_GROUP)],
                               axis=1).astype(BF16)
        g_diag = jnp.where(m["row"] == m["col"], g_end[:, sl], 0.0)
        out.append(dict(a_ab=ab[:c], a_rb=ab[c:], ak=ak, vg=v[:, sl], a0=a0[:, sl], r0=r0[:, sl],
                        g_diag=g_diag, bk_t=bk_t))
    return out


def _wkv_kernel(r_ref, k_ref, v_ref, lw_ref, kn_ref, a_ref, g_ref, rk_ref, lg_ref, lb_ref,
                o_ref, s_scr, *, n_chunks):
    @pl.when(pl.program_id(1) == 0)
    def _():
        s_scr[...] = jnp.zeros_like(s_scr)

    n_groups = N_RWKV_HEADS // WKV_GROUP
    m = _wkv_masks()
    chunk_rows = [slice(ci * WKV_CHUNK, (ci + 1) * WKV_CHUNK) for ci in range(n_chunks)]

    c = WKV_CHUNK
    lw_all = lw_ref[...]
    cum_all = _chunk_cumsum(lw_all, n_chunks)
    probs = []
    for rows in chunk_rows:
        probs += _wkv_prepare(r_ref[rows, :], k_ref[rows, :], v_ref[rows, :], lw_all[rows, :],
                              cum_all[rows, :], kn_ref[rows, :], a_ref[rows, :], m)
    n_probs = len(probs)
    split = lambda stacked: [stacked[i * c:(i + 1) * c] for i in range(n_probs)]
    decays = split(_seg_sum(jnp.concatenate([p["g_diag"] for p in probs], axis=0), m))
    rkr = r_ref[...] * k_ref[...] * rk_ref[...]
    bonus_w = split(_seg_sum(jnp.concatenate(
        [rkr[rows, gi * GW:(gi + 1) * GW] for rows in chunk_rows for gi in range(n_groups)], axis=0), m))

    h = (n_chunks // 2) * n_groups
    res_a, res_b = {}, {}
    states = [s_scr[gi] for gi in range(n_groups)]
    ys = []
    _run_interleaved(_wkv_products(probs[:h], m, res_a))
    _run_interleaved(_wkv_recurrence(probs[:h], res_a, decays[:h], states, ys, m),
                     _wkv_products(probs[h:], m, res_b))
    _run_interleaved(_wkv_recurrence(probs[h:], res_b, decays[h:], states, ys, m))
    for gi in range(n_groups):
        s_scr[gi] = states[gi]

    y_all = jnp.concatenate(ys, axis=0)
    d_all = y_all - _seg_sum(y_all, m) * (1.0 / HEAD_DIM)
    inv_all = lax.rsqrt(_seg_sum(d_all * d_all, m) * (1.0 / HEAD_DIM) + RWKV_GN_EPS)
    for ci, rows in enumerate(chunk_rows):
        for gi in range(n_groups):
            i = ci * n_groups + gi
            sl = slice(gi * GW, (gi + 1) * GW)
            yn = d_all[i * c:(i + 1) * c] * inv_all[i * c:(i + 1) * c] * lg_ref[:, sl] + lb_ref[:, sl]
            o_ref[rows, sl] = ((yn + bonus_w[i] * probs[i]["vg"]) * g_ref[rows, sl]).astype(BF16)


def _wkv(r, k, v, lw, kn, a, g, r_k, lnx_g, lnx_b, *, batch, seq, rows):
    T = r.shape[0]
    nb = seq // rows
    blk = pl.BlockSpec((rows, D_RWKV), lambda b, i: (b * nb + i, 0))
    par = pl.BlockSpec((1, D_RWKV), lambda b, i: (0, 0))
    return pl.pallas_call(
        functools.partial(_wkv_kernel, n_chunks=rows // WKV_CHUNK),
        grid=(batch, nb),
        in_specs=[blk] * 7 + [par] * 3,
        out_specs=blk,
        out_shape=jax.ShapeDtypeStruct((T, D_RWKV), BF16),
        scratch_shapes=[pltpu.VMEM((N_RWKV_HEADS // WKV_GROUP, HEAD_DIM, GW), F32)],
        compiler_params=_cparams(("parallel", "arbitrary")),
        name="wkv7",
    )(r, k, v, lw, kn, a, g, r_k.reshape(1, -1), lnx_g.reshape(1, -1), lnx_b.reshape(1, -1))


def _dattn_kernel(q_ref, k_ref, v_ref, lq1_ref, lk1_ref, lq2_ref, lk2_ref, sg_ref, o_ref,
                  qs_scr, m_scr, l_scr, acc_scr, s0_scr, s1_scr, p0_scr, p1_scr, al0_scr, al1_scr,
                  *, tq, lam_init):
    qi = pl.program_id(2)
    q = q_ref[...]
    lane = lax.broadcasted_iota(jnp.int32, q.shape, 1)
    zero = jnp.zeros_like(q)
    qs_scr[:tq, :] = jnp.where(lane < HEAD_DIM, q, zero)
    qs_scr[tq:, :] = jnp.where(lane < HEAD_DIM, zero, q)
    m_scr[...] = jnp.full_like(m_scr, -jnp.inf)
    l_scr[...] = jnp.zeros_like(l_scr)
    acc_scr[...] = jnp.zeros_like(acc_scr)

    bufs = ((s0_scr, p0_scr, al0_scr), (s1_scr, p1_scr, al1_scr))

    def scores(j, par):
        start = pl.multiple_of(j * tq, tq)
        bufs[par][0][...] = _dot_nt(qs_scr[...], k_ref[pl.ds(start, tq), :])

    def softmax(par, masked):
        s_ref, p_ref, al_ref = bufs[par]
        s = s_ref[...]
        if masked:
            r_pos = lax.broadcasted_iota(jnp.int32, s.shape, 0) % tq
            c_pos = lax.broadcasted_iota(jnp.int32, s.shape, 1)
            s = jnp.where(c_pos <= r_pos, s, -jnp.inf)
        m_prev = m_scr[...]
        m_new = jnp.maximum(m_prev, jnp.max(s, axis=-1, keepdims=True))
        alpha = jnp.exp2(m_prev - m_new)
        p = jnp.exp2(s - jnp.concatenate([m_new] * (tq // LANES), axis=1))
        l_scr[...] = alpha * l_scr[...] + jnp.sum(p, axis=-1, keepdims=True)
        m_scr[...] = m_new
        p_ref[...] = p.astype(BF16)
        al_ref[...] = alpha

    def values(j, par):
        _, p_ref, al_ref = bufs[par]
        start = pl.multiple_of(j * tq, tq)
        acc_scr[...] = al_ref[...] * acc_scr[...] + jnp.dot(
            p_ref[...], v_ref[pl.ds(start, tq), :], preferred_element_type=F32)

    scores(0, 0)

    @pl.when(qi == 0)
    def _():
        softmax(0, True)
        values(0, 0)

    @pl.when(qi > 0)
    def _():
        scores(1, 1)
        softmax(0, False)

        def body(i, carry):
            t = 2 + 2 * i
            values(t - 2, 0)
            scores(t, 0)
            softmax(1, False)
            values(t - 1, 1)
            scores(t + 1, 1)
            softmax(0, False)
            return carry

        lax.fori_loop(0, (qi - 1) // 2, body, 0)

        @pl.when(qi % 2 == 1)
        def _():
            values(qi - 1, 0)
            softmax(1, True)
            values(qi, 1)

        @pl.when(qi % 2 == 0)
        def _():
            values(qi - 2, 0)
            scores(qi, 0)
            softmax(1, False)
            values(qi - 1, 1)
            softmax(0, True)
            values(qi, 0)

    lam = (jnp.exp(jnp.sum(lq1_ref[...] * lk1_ref[...], axis=-1, keepdims=True))
           - jnp.exp(jnp.sum(lq2_ref[...] * lk2_ref[...], axis=-1, keepdims=True)) + lam_init)
    on = acc_scr[...] / l_scr[...]
    o = on[:tq, :] - lam * on[tq:, :]
    o = o * lax.rsqrt(jnp.mean(o * o, axis=-1, keepdims=True) + 1e-5) * sg_ref[...]
    o_ref[...] = (o * (1.0 - lam_init)).astype(BF16)


def _dattn(q, k, v, lq1, lk1, lq2, lk2, subln_g, *, batch, seq, tq, lam_init):
    T = q.shape[0]
    nq = seq // tq
    hd = 2 * HEAD_DIM
    qmap = lambda b, h, i: (b * nq + i, h)
    kvmap = lambda b, h, i: (b, h)
    par = lambda b, h, i: (0, 0)
    vec = lambda a: a.reshape(1, -1)
    return pl.pallas_call(
        functools.partial(_dattn_kernel, tq=tq, lam_init=lam_init),
        grid=(batch, N_DIFF_HEADS, nq),
        in_specs=[
            pl.BlockSpec((tq, hd), qmap),
            pl.BlockSpec((seq, hd), kvmap),
            pl.BlockSpec((seq, hd), kvmap),
            pl.BlockSpec((1, HEAD_DIM), par),
            pl.BlockSpec((1, HEAD_DIM), par),
            pl.BlockSpec((1, HEAD_DIM), par),
            pl.BlockSpec((1, HEAD_DIM), par),
            pl.BlockSpec((1, hd), par),
        ],
        out_specs=pl.BlockSpec((tq, hd), qmap),
        out_shape=jax.ShapeDtypeStruct((T, D_DIFF), BF16),
        scratch_shapes=[
            pltpu.VMEM((2 * tq, hd), BF16),
            pltpu.VMEM((2 * tq, LANES), F32),
            pltpu.VMEM((2 * tq, LANES), F32),
            pltpu.VMEM((2 * tq, hd), F32),
            pltpu.VMEM((2 * tq, tq), F32),
            pltpu.VMEM((2 * tq, tq), F32),
            pltpu.VMEM((2 * tq, tq), BF16),
            pltpu.VMEM((2 * tq, tq), BF16),
            pltpu.VMEM((2 * tq, LANES), F32),
            pltpu.VMEM((2 * tq, LANES), F32),
        ],
        compiler_params=_cparams(("parallel", "parallel", "arbitrary")),
        name="diff_attn",
    )(q, k, v, vec(lq1), vec(lk1), vec(lq2), vec(lk2), vec(subln_g))


def _proj_ln_kernel(yr_ref, yd_ref, w_ref, x_ref, g_ref, b_ref, o_ref):
    mix = jnp.dot(yr_ref[...], w_ref[:D_RWKV, :], preferred_element_type=F32)
    mix = mix + jnp.dot(yd_ref[...], w_ref[D_RWKV:, :], preferred_element_type=F32)
    o_ref[...] = _layer_norm(DEEPNORM_ALPHA * x_ref[...] + mix, g_ref[...], b_ref[...])


def _proj_ln(yr, yd, w_out, x2, g, b, *, tm):
    T = x2.shape[0]
    row = lambda i: (i, 0)
    full = lambda i: (0, 0)
    return pl.pallas_call(
        _proj_ln_kernel,
        grid=(T // tm,),
        in_specs=[
            pl.BlockSpec((tm, D_RWKV), row),
            pl.BlockSpec((tm, D_DIFF), row),
            pl.BlockSpec((D_RWKV + D_DIFF, D_MODEL), full),
            pl.BlockSpec((tm, D_MODEL), row),
            pl.BlockSpec((1, D_MODEL), full),
            pl.BlockSpec((1, D_MODEL), full),
        ],
        out_specs=pl.BlockSpec((tm, D_MODEL), row),
        out_shape=jax.ShapeDtypeStruct((T, D_MODEL), F32),
        compiler_params=_cparams(("parallel",)),
        name="ab_out_ln",
    )(yr, yd, w_out, x2, g.reshape(1, -1), b.reshape(1, -1))


def _gmlp_kernel(x_ref, win_ref, bin_ref, lng_ref, lnb_ref, ws_ref, bs_ref, wout_ref, g_ref, b_ref,
                 o_ref, gated_scr, *, tm):
    x = x_ref[...]
    xb = x.astype(BF16)

    def gelu(h):
        return 0.5 * h * (1.0 + lax.erf(h * (0.5 ** 0.5)))

    hv = jnp.dot(xb, win_ref[:, D_GMLP:], preferred_element_type=F32) + bin_ref[:, D_GMLP:]
    hu = jnp.dot(xb, win_ref[:, :D_GMLP], preferred_element_type=F32) + bin_ref[:, :D_GMLP]
    v = _layer_norm(gelu(hv), lng_ref[...], lnb_ref[...]).astype(BF16)
    u = gelu(hu)
    row = lax.broadcasted_iota(jnp.int32, (CHUNK, CHUNK), 0)
    col = lax.broadcasted_iota(jnp.int32, (CHUNK, CHUNK), 1)
    gw = D_GMLP // GMLP_GROUPS
    n_chunks = tm // CHUNK
    for gi in range(GMLP_GROUPS):
        cs = slice(gi * gw, (gi + 1) * gw)
        ws = jnp.where(row >= col, ws_ref[gi], 0.0).astype(BF16)
        v_side = jnp.concatenate([v[c * CHUNK:(c + 1) * CHUNK, cs] for c in range(n_chunks)], axis=1)
        mixed = jnp.dot(ws, v_side, preferred_element_type=F32)
        for c in range(n_chunks):
            rs = slice(c * CHUNK, (c + 1) * CHUNK)
            gated_scr[rs, cs] = (u[rs, cs] * (mixed[:, c * gw:(c + 1) * gw] + bs_ref[gi])).astype(BF16)
    mix = jnp.dot(gated_scr[...], wout_ref[...], preferred_element_type=F32)
    o_ref[...] = _layer_norm(DEEPNORM_ALPHA * x + mix, g_ref[...], b_ref[...])


def _gmlp(x2, w_in, b_in, ln_g, ln_b, w_s, b_s, w_out, g, b, *, tm):
    T = x2.shape[0]
    row = lambda i: (i, 0)
    full = lambda i: (0, 0)
    full3 = lambda i: (0, 0, 0)
    vec = lambda a: a.reshape(1, -1)
    gw = D_GMLP // GMLP_GROUPS
    bs_b = jnp.broadcast_to(b_s[:, :, None], (GMLP_GROUPS, CHUNK, gw))
    return pl.pallas_call(
        functools.partial(_gmlp_kernel, tm=tm),
        grid=(T // tm,),
        in_specs=[
            pl.BlockSpec((tm, D_MODEL), row),
            pl.BlockSpec((D_MODEL, 2 * D_GMLP), full, pipeline_mode=pl.Buffered(1)),
            pl.BlockSpec((1, 2 * D_GMLP), full),
            pl.BlockSpec((1, D_GMLP), full),
            pl.BlockSpec((1, D_GMLP), full),
            pl.BlockSpec((GMLP_GROUPS, CHUNK, CHUNK), full3),
            pl.BlockSpec((GMLP_GROUPS, CHUNK, gw), full3),
            pl.BlockSpec((D_GMLP, D_MODEL), full, pipeline_mode=pl.Buffered(1)),
            pl.BlockSpec((1, D_MODEL), full),
            pl.BlockSpec((1, D_MODEL), full),
        ],
        out_specs=pl.BlockSpec((tm, D_MODEL), row),
        out_shape=jax.ShapeDtypeStruct((T, D_MODEL), F32),
        scratch_shapes=[pltpu.VMEM((tm, D_GMLP), BF16)],
        compiler_params=_cparams(("parallel",)),
        name="gmlp",
    )(x2, w_in, vec(b_in), vec(ln_g), vec(ln_b), w_s, bs_b, w_out, vec(g), vec(b))


def _ffn_kernel(x_ref, xh_ref, wup_ref, cw_ref, cb_ref, wd_ref, g_ref, b_ref, o_ref,
                *, blocks_per_seq, tf, n_slabs):
    x = x_ref[...]
    tm = x.shape[0]
    xe = jnp.concatenate([x, xh_ref[...]], axis=0).astype(BF16)
    xb = xe[:tm]
    seq_start = pl.program_id(0) % blocks_per_seq == 0

    def up(j):
        cols = slice(j * tf, (j + 1) * tf)
        ge = jnp.dot(xe, wup_ref[:, cols], preferred_element_type=F32)
        val = jnp.dot(xb, wup_ref[:, D_FF + j * tf:D_FF + (j + 1) * tf], preferred_element_type=F32)
        return ge[:tm], val, jnp.where(seq_start, 0.0, ge[tm:])

    acc = None
    nxt = up(0)
    for j in range(n_slabs):
        gate, val, gh = nxt
        if j + 1 < n_slabs:
            nxt = up(j + 1)
        cols = slice(j * tf, (j + 1) * tf)
        cw = cw_ref[:, cols]
        conv = (cb_ref[:, cols] + cw[0:1, :] * _shift_rows(gate, 2, gh)
                + cw[1:2, :] * _shift_rows(gate, 1, gh) + cw[2:3, :] * gate)
        hid = conv * jax.nn.sigmoid(conv) * val
        down = jnp.dot(hid.astype(BF16), wd_ref[j], preferred_element_type=F32)
        acc = down if acc is None else acc + down
    o_ref[...] = _layer_norm(DEEPNORM_ALPHA * x + acc, g_ref[...], b_ref[...])


def _ffn(x2, w_up, conv_w, conv_b, w_down, g, b, *, seq, tm, tf):
    T = x2.shape[0]
    n_slabs = D_FF // tf
    bps = seq // tm
    wup = w_up.astype(BF16)
    cb = conv_b.reshape(1, D_FF)
    wd = w_down.reshape(n_slabs, tf, D_MODEL).astype(BF16)
    row = lambda i: (i, 0)
    full = lambda i: (0, 0)
    full3 = lambda i: (0, 0, 0)
    halo = lambda i: (jnp.maximum(i * (tm // HALO_ROWS) - 1, 0), 0)
    once = pl.Buffered(1)
    return pl.pallas_call(
        functools.partial(_ffn_kernel, blocks_per_seq=bps, tf=tf, n_slabs=n_slabs),
        grid=(T // tm,),
        in_specs=[
            pl.BlockSpec((tm, D_MODEL), row),
            pl.BlockSpec((HALO_ROWS, D_MODEL), halo),
            pl.BlockSpec((D_MODEL, 2 * D_FF), full, pipeline_mode=once),
            pl.BlockSpec((3, D_FF), full, pipeline_mode=once),
            pl.BlockSpec((1, D_FF), full, pipeline_mode=once),
            pl.BlockSpec((n_slabs, tf, D_MODEL), full3, pipeline_mode=once),
            pl.BlockSpec((1, D_MODEL), full),
            pl.BlockSpec((1, D_MODEL), full),
        ],
        out_specs=pl.BlockSpec((tm, D_MODEL), row),
        out_shape=jax.ShapeDtypeStruct((T, D_MODEL), F32),
        compiler_params=_cparams(("parallel",)),
        name="conv_ffn",
    )(x2, x2, wup, conv_w, cb, wd, g.reshape(1, -1), b.reshape(1, -1))


def _rope_tables(seq):
    half = ROPE_DIM // 2
    inv_freq = ROPE_THETA ** (-jnp.arange(0, ROPE_DIM, 2, dtype=F32) / ROPE_DIM)
    ang = jnp.arange(seq, dtype=F32)[:, None] * inv_freq[None, :]
    cos, sin = jnp.cos(ang), jnp.sin(ang)
    ones = jnp.ones((seq, HEAD_DIM - ROPE_DIM), F32)
    zeros = jnp.zeros((seq, HEAD_DIM - ROPE_DIM), F32)
    zh = jnp.zeros((seq, half), F32)
    c64 = jnp.concatenate([cos, cos, ones], axis=1)
    s1_64 = jnp.concatenate([zh, sin, zeros], axis=1)
    s2_64 = jnp.concatenate([-sin, zh, zeros], axis=1)
    two = lambda t: jnp.concatenate([t, t], axis=1)
    return two(c64), two(s1_64), two(s2_64)


def kernel(x, ab_w_in, ab_shift_mu, ab_w0, ab_w2, ab_a0, ab_a2, ab_g2, ab_k_k, ab_k_a, ab_r_k, ab_lnx_g, ab_lnx_b, ab_lam_q1, ab_lam_k1, ab_lam_q2, ab_lam_k2, ab_subln_g, ab_w_out, c_w_in, c_b_in, c_ln_g, c_ln_b, c_w_s, c_b_s, c_w_out, ln1_g, ln1_b, ffn_w_up, ffn_conv_w, ffn_conv_b, ffn_w_down, ln2_g, ln2_b):
    batch, seq, _ = x.shape
    x2 = x.reshape(batch * seq, D_MODEL)
    rc, rs1, rs2 = _rope_tables(seq)
    tm_in = min(512, seq)
    tm_ffn = min(1024, seq)
    tm_gmlp = min(1024, seq)
    tq = min(512, seq)
    wkv_rows = min(512, seq)
    for i in range(DEPTH):
        j = i // 2
        if i % 2 == 0:
            (r, k, v, lw, kn, a, g, qd, kd, vd) = _ab_in(
                x2, ab_w_in[j].astype(BF16), ab_shift_mu[j], ab_w0[j], ab_w2[j], ab_a0[j],
                ab_a2[j].astype(BF16), ab_g2[j].astype(BF16), ab_k_k[j], ab_k_a[j], rc, rs1, rs2,
                seq=seq, tm=tm_in)
            y_r = _wkv(r, k, v, lw, kn, a, g, ab_r_k[j], ab_lnx_g[j], ab_lnx_b[j],
                       batch=batch, seq=seq, rows=wkv_rows)
            lam_init = 0.8 - 0.6 * math.exp(-0.3 * i)
            y_d = _dattn(qd, kd, vd, ab_lam_q1[j], ab_lam_k1[j], ab_lam_q2[j], ab_lam_k2[j],
                         ab_subln_g[j], batch=batch, seq=seq, tq=tq, lam_init=lam_init)
            x2 = _proj_ln(y_r, y_d, ab_w_out[j].astype(BF16), x2, ln1_g[i], ln1_b[i], tm=tm_in)
        else:
            x2 = _gmlp(x2, c_w_in[j].astype(BF16), c_b_in[j], c_ln_g[j], c_ln_b[j], c_w_s[j],
                       c_b_s[j], c_w_out[j].astype(BF16), ln1_g[i], ln1_b[i], tm=tm_gmlp)
        x2 = _ffn(x2, ffn_w_up[i], ffn_conv_w[i], ffn_conv_b[i], ffn_w_down[i], ln2_g[i], ln2_b[i],
                  seq=seq, tm=tm_ffn, tf=256)
    return x2.reshape(batch, seq, D_MODEL)
```

```python
import functools
import math

import jax
import jax.numpy as jnp
from jax import lax
from jax.experimental import pallas as pl
from jax.experimental.pallas import tpu as pltpu

F32 = jnp.float32
BF16 = jnp.bfloat16
HIGHEST = lax.Precision.HIGHEST

D_MODEL = 1024
HEAD_DIM = 64
N_RWKV_HEADS = 8
D_RWKV = N_RWKV_HEADS * HEAD_DIM
N_DIFF_HEADS = 4
D_DIFF = N_DIFF_HEADS * 2 * HEAD_DIM
DECAY_LORA = 64
AAA_LORA = 64
GATE_LORA = 128
RWKV_COLS = 3 * D_RWKV + DECAY_LORA + AAA_LORA + GATE_LORA
AB_COLS = RWKV_COLS + 3 * D_DIFF
RWKV_GN_EPS = 64e-5
ROPE_THETA = 500000.0
ROPE_DIM = HEAD_DIM // 4
CHUNK = 128
GMLP_GROUPS = 8
D_GMLP = D_MODEL
D_FF = 2816
DEPTH = 4
DEEPNORM_ALPHA = (2 * DEPTH) ** 0.25

LANES = 128
HALO_ROWS = 16
WKV_CHUNK = 64
VMEM_LIMIT = 56 * 1024 * 1024
LOG2E = math.log2(math.e)


def _cparams(sem):
    return pltpu.CompilerParams(dimension_semantics=sem, vmem_limit_bytes=VMEM_LIMIT)


def _dot(a, b):
    return jnp.dot(a.astype(BF16), b.astype(BF16), preferred_element_type=F32)


def _dot_nt(a, b):
    return lax.dot_general(a.astype(BF16), b.astype(BF16), (((1,), (1,)), ((), ())),
                           preferred_element_type=F32)


def _layer_norm(z, g, b, eps=1e-5):
    mu = jnp.mean(z, axis=-1, keepdims=True)
    d = z - mu
    var = jnp.mean(d * d, axis=-1, keepdims=True)
    return d * lax.rsqrt(var + eps) * g + b


def _shift_rows(t, k, halo):
    rolled = pltpu.roll(t, k, 0)
    row = lax.broadcasted_iota(jnp.int32, t.shape, 0)
    out = rolled
    for j in range(k):
        out = jnp.where(row == j, halo[HALO_ROWS - k + j:HALO_ROWS - k + j + 1, :], out)
    return out


def _ab_in_kernel(x_ref, xh_ref, w_ref, mu_ref, w0_ref, w2_ref, a0_ref, a2_ref, g2_ref, kk_ref,
                  ka_ref, rc_ref, rs1_ref, rs2_ref,
                  r_out, k_out, v_out, lw_out, kn_out, a_out, g_out, q_out, kd_out, vd_out,
                  *, blocks_per_seq):
    seq_start = pl.program_id(0) % blocks_per_seq == 0
    tm = x_ref.shape[0]
    xe = jnp.concatenate([x_ref[...], xh_ref[...]], axis=0).astype(BF16)
    xb = xe[:tm]

    def proj(c0, c1):
        return jnp.dot(xb, w_ref[:, c0:c1], preferred_element_type=F32)

    def proj_shifted(c0, c1):
        pe = jnp.dot(xe, w_ref[:, c0:c1], preferred_element_type=F32)
        p = pe[:tm]
        ph = jnp.where(seq_start, 0.0, pe[tm:])
        return p + mu_ref[:, c0:c1] * (_shift_rows(p, 1, ph) - p)

    o = 3 * D_RWKV
    pq = proj(RWKV_COLS, RWKV_COLS + D_DIFF)
    pk = proj(RWKV_COLS + D_DIFF, RWKV_COLS + 2 * D_DIFF)
    xl = proj_shifted(o, RWKV_COLS)
    k = proj_shifted(D_RWKV, 2 * D_RWKV)
    r = proj_shifted(0, D_RWKV)
    v = proj_shifted(2 * D_RWKV, o)
    vd_out[...] = proj(RWKV_COLS + 2 * D_DIFF, AB_COLS).astype(BF16)
    xw = xl[:, :DECAY_LORA]
    xa = xl[:, DECAY_LORA:DECAY_LORA + AAA_LORA]
    xg = xl[:, DECAY_LORA + AAA_LORA:]

    rc = jnp.concatenate([rc_ref[...]] * (D_DIFF // LANES), axis=1)
    rs1 = jnp.concatenate([rs1_ref[...]] * (D_DIFF // LANES), axis=1)
    rs2 = jnp.concatenate([rs2_ref[...]] * (D_DIFF // LANES), axis=1)
    half = ROPE_DIM // 2

    def rope(t):
        return t * rc + pltpu.roll(t, half, 1) * rs1 + pltpu.roll(t, D_DIFF - half, 1) * rs2

    q_out[...] = (rope(pq) * (HEAD_DIM ** -0.5 * LOG2E)).astype(BF16)
    kd_out[...] = rope(pk).astype(BF16)

    z = w0_ref[...] + jnp.dot(jnp.tanh(xw), w2_ref[...], precision=HIGHEST,
                              preferred_element_type=F32)
    softplus_neg = jnp.maximum(-z, 0.0) + jnp.log1p(jnp.exp(-jnp.abs(z)))
    w = -softplus_neg - 0.5
    lw_out[...] = -jnp.exp(w)
    a = jax.nn.sigmoid(a0_ref[...] + _dot(xa, a2_ref[...]))
    a_out[...] = a
    g_out[...] = _dot(jax.nn.sigmoid(xg), g2_ref[...])

    kx = k * kk_ref[...]
    sq = kx * kx
    lane = lax.broadcasted_iota(jnp.int32, (sq.shape[0], LANES), 1)
    lo = lane < HEAD_DIM
    for c in range(D_RWKV // LANES):
        blk = sq[:, c * LANES:(c + 1) * LANES]
        n_lo = jnp.sqrt(jnp.sum(jnp.where(lo, blk, 0.0), axis=-1, keepdims=True))
        n_hi = jnp.sqrt(jnp.sum(jnp.where(lo, 0.0, blk), axis=-1, keepdims=True))
        norm = jnp.maximum(jnp.where(lo, n_lo, n_hi), 1e-12)
        kn_out[:, c * LANES:(c + 1) * LANES] = kx[:, c * LANES:(c + 1) * LANES] / norm
    r_out[...] = r
    k_out[...] = k * (1.0 + (a - 1.0) * ka_ref[...])
    v_out[...] = v


def _ab_in(x2, w_in, mu, w0, w2, a0, a2, g2, k_k, k_a, rc, rs1, rs2, *, seq, tm):
    T = x2.shape[0]
    n = T // tm
    bps = seq // tm
    row = lambda i: (i, 0)
    full = lambda i: (0, 0)
    halo = lambda i: (jnp.maximum(i * (tm // HALO_ROWS) - 1, 0), 0)
    rope_map = lambda i: (i % bps, 0)
    vec = lambda a: a.reshape(1, -1)
    f32_out = jax.ShapeDtypeStruct((T, D_RWKV), F32)
    bf_out = jax.ShapeDtypeStruct((T, D_DIFF), BF16)
    out_spec = pl.BlockSpec((tm, D_RWKV), row)
    return pl.pallas_call(
        functools.partial(_ab_in_kernel, blocks_per_seq=bps),
        grid=(n,),
        in_specs=[
            pl.BlockSpec((tm, D_MODEL), row),
            pl.BlockSpec((HALO_ROWS, D_MODEL), halo),
            pl.BlockSpec((D_MODEL, AB_COLS), full),
            pl.BlockSpec((1, RWKV_COLS), full),
            pl.BlockSpec((1, D_RWKV), full),
            pl.BlockSpec((DECAY_LORA, D_RWKV), full),
            pl.BlockSpec((1, D_RWKV), full),
            pl.BlockSpec((AAA_LORA, D_RWKV), full),
            pl.BlockSpec((GATE_LORA, D_RWKV), full),
            pl.BlockSpec((1, D_RWKV), full),
            pl.BlockSpec((1, D_RWKV), full),
            pl.BlockSpec((tm, LANES), rope_map),
            pl.BlockSpec((tm, LANES), rope_map),
            pl.BlockSpec((tm, LANES), rope_map),
        ],
        out_specs=[out_spec] * 10,
        out_shape=[f32_out] * 7 + [bf_out] * 3,
        compiler_params=_cparams(("parallel",)),
        name="ab_in",
    )(x2, x2, w_in, vec(mu), vec(w0), w2, vec(a0), a2, g2, vec(k_k), vec(k_a), rc, rs1, rs2)


WKV_GROUP = 4
GW = WKV_GROUP * HEAD_DIM


def _wkv_masks():
    c = WKV_CHUNK
    row = lax.broadcasted_iota(jnp.int32, (c, GW), 0)
    col = lax.broadcasted_iota(jnp.int32, (c, GW), 1) % HEAD_DIM
    brow = lax.broadcasted_iota(jnp.int32, (GW, GW), 0) // HEAD_DIM
    bcol = lax.broadcasted_iota(jnp.int32, (GW, GW), 1) // HEAD_DIM
    bd = jnp.where(brow == bcol, 1.0, 0.0).astype(BF16)
    brow2 = lax.broadcasted_iota(jnp.int32, (2 * c * WKV_GROUP, GW), 0) // (2 * c)
    bcol2 = lax.broadcasted_iota(jnp.int32, (2 * c * WKV_GROUP, GW), 1) // HEAD_DIM
    bd2 = jnp.where(brow2 == bcol2, 1.0, 0.0).astype(BF16)
    strict_incl = jnp.concatenate([row > col, row >= col], axis=0)
    return dict(row=row, col=col, bd=bd, bd2=bd2, strict_incl=strict_incl)


def _chunk_cumsum(lw, n_chunks):
    n = 2 * WKV_CHUNK
    r = lax.broadcasted_iota(jnp.int32, (n, n), 0)
    c = lax.broadcasted_iota(jnp.int32, (n, n), 1)
    tri = jnp.where(jnp.logical_and(r // WKV_CHUNK == c // WKV_CHUNK, r >= c), 1.0, 0.0).astype(BF16)
    h1 = lw.astype(BF16)
    rem = lw - h1.astype(F32)
    h2 = rem.astype(BF16)
    h3 = (rem - h2.astype(F32)).astype(BF16)
    out = []
    for b in range(n_chunks // 2):
        rows = slice(b * n, (b + 1) * n)
        cum = jnp.dot(tri, h1[rows], preferred_element_type=F32)
        cum = cum + jnp.dot(tri, h2[rows], preferred_element_type=F32)
        out.append(cum + jnp.dot(tri, h3[rows], preferred_element_type=F32))
    return jnp.concatenate(out, axis=0)


def _bd(w, m):
    return jnp.concatenate([w.astype(BF16)] * WKV_GROUP, axis=0) * m["bd"]


def _mm(x, w_bd):
    return jnp.dot(x.astype(BF16), w_bd, preferred_element_type=F32)


def _mm_nt(x, w_bd):
    return lax.dot_general(x.astype(BF16), w_bd, (((1,), (1,)), ((), ())),
                           preferred_element_type=F32)


def _seg_sum(x, m):
    hi = x.astype(BF16)
    lo = (x - hi.astype(F32)).astype(BF16)
    s = jnp.dot(jnp.concatenate([hi, lo], axis=0), m["bd"], preferred_element_type=F32)
    return s[:x.shape[0]] + s[x.shape[0]:]


def _run_interleaved(*gens):
    gens = list(gens)
    while gens:
        for g in list(gens):
            try:
                next(g)
            except StopIteration:
                gens.remove(g)


def _unit_lower_inverse(n_mats, m):
    row, col = m["row"], m["col"]
    eye = jnp.where(row == col, 1.0, 0.0)
    same16 = (row // 16) == (col // 16)
    c = WKV_CHUNK
    pws = [jnp.where(same16, n, 0.0) for n in n_mats]
    ts = [eye + p for p in pws]
    pws = [_mm(p, _bd(p, m)) for p in pws]
    yield
    for lvl in range(3):
        pw_bds = [_bd(p, m) for p in pws]
        if lvl < 2:
            both = [_mm(jnp.concatenate([t, p], axis=0), b) for t, p, b in zip(ts, pws, pw_bds)]
            ts = [t + bo[:c] for t, bo in zip(ts, both)]
            pws = [bo[c:] for bo in both]
        else:
            ts = [t + _mm(t, b) for t, b in zip(ts, pw_bds)]
        yield
    blk = 16
    while blk < WKV_CHUNK:
        same_lo = (row // blk) == (col // blk)
        same_hi = (row // (2 * blk)) == (col // (2 * blk))
        sel = jnp.logical_and(same_hi, jnp.logical_not(same_lo))
        halves = [_mm(t, _bd(jnp.where(sel, n, 0.0), m)) for t, n in zip(ts, n_mats)]
        yield
        ts = [t + _mm(h, _bd(t, m)) for t, h in zip(ts, halves)]
        yield
        blk *= 2
    return ts


def _wkv_products(probs, m, res):
    c = WKV_CHUNK
    ts = yield from _unit_lower_inverse([p["a_ab"] for p in probs], m)
    vg_bds = [_bd(p["vg"], m) for p in probs]
    akv_yvs = [_mm(p["ak"], vb) for p, vb in zip(probs, vg_bds)]
    yield
    t_bfs = [t.astype(BF16) for t in ts]
    xas = [_mm(t, _bd(p["a0"], m)) for t, p in zip(t_bfs, probs)]
    yield
    res["xvs"] = [_mm(t, _bd(ay[:c], m)) for t, ay in zip(t_bfs, akv_yvs)]
    res["xrs"] = [jnp.concatenate([xa, p["r0"]], axis=0).astype(BF16) for xa, p in zip(xas, probs)]
    res["yvs"] = [ay[c:] for ay in akv_yvs]
    yield


def _wkv_recurrence(probs, res, decays, states, ys, m):
    c = WKV_CHUNK
    n_groups = len(states)
    for c0 in range(0, len(probs), n_groups):
        us = []
        for gi in range(n_groups):
            i = c0 + gi
            uy = _mm(res["xrs"][i], _bd(states[gi], m))
            us.append((uy[:c] + res["xvs"][i], uy[c:]))
        yield
        for gi in range(n_groups):
            i = c0 + gi
            p = probs[i]
            u, y0 = us[gi]
            ys.append(y0 + _mm(p["a_rb"], _bd(u, m)) + res["yvs"][i])
            uv = jnp.concatenate([u, p["vg"]], axis=0).astype(BF16)
            uv_bd = jnp.concatenate([uv] * WKV_GROUP, axis=0) * m["bd2"]
            states[gi] = states[gi] * decays[c0 + gi] + jnp.dot(
                p["bk_t"], uv_bd, preferred_element_type=F32)
        yield


def _wkv_prepare(r, k, v, lw, cum, kn, a, m):
    c = r.shape[0]
    mid = cum[c // 2 - 1:c // 2, :]
    end = cum[c - 1:c, :]
    e_abs = jnp.exp(cum)
    e_abs_prev = jnp.exp(cum - lw)
    e_mid = jnp.exp(-mid)
    e_neg = jnp.exp(mid - cum)
    e_end = jnp.exp(end - cum)
    g_end = jnp.exp(end)
    bvec = kn * a
    r0 = r * e_abs
    a0 = -kn * e_abs_prev
    rt = r0 * e_mid
    at = a0 * e_mid
    kt = k * e_neg
    bt = bvec * e_neg
    kh = k * e_end
    bh = bvec * e_end
    out = []
    for gi in range(N_RWKV_HEADS // WKV_GROUP):
        sl = slice(gi * GW, (gi + 1) * GW)
        lhs = jnp.concatenate([at[:, sl], rt[:, sl]], axis=0)
        ab = jnp.where(m["strict_incl"], _mm_nt(lhs, _bd(bt[:, sl], m)), 0.0)
        ak = jnp.where(m["strict_incl"], _mm_nt(lhs, _bd(kt[:, sl], m)), 0.0)
        bk_t = jnp.transpose(jnp.concatenate([bh[:, sl], kh[:, sl]], axis=0))
        bk_t = jnp.concatenate([bk_t[hh * HEAD_DIM:(hh + 1) * HEAD_DIM] for hh in range(WKV_GROUP)],
                               axis=1).astype(BF16)
        g_diag = jnp.where(m["row"] == m["col"], g_end[:, sl], 0.0)
        out.append(dict(a_ab=ab[:c], a_rb=ab[c:], ak=ak, vg=v[:, sl], a0=a0[:, sl], r0=r0[:, sl],
                        g_diag=g_diag, bk_t=bk_t))
    return out


def _wkv_kernel(r_ref, k_ref, v_ref, lw_ref, kn_ref, a_ref, g_ref, rk_ref, lg_ref, lb_ref,
                o_ref, s_scr, *, n_chunks):
    @pl.when(pl.program_id(1) == 0)
    def _():
        s_scr[...] = jnp.zeros_like(s_scr)

    n_groups = N_RWKV_HEADS // WKV_GROUP
    m = _wkv_masks()
    chunk_rows = [slice(ci * WKV_CHUNK, (ci + 1) * WKV_CHUNK) for ci in range(n_chunks)]

    c = WKV_CHUNK
    lw_all = lw_ref[...]
    cum_all = _chunk_cumsum(lw_all, n_chunks)
    probs = []
    for rows in chunk_rows:
        probs += _wkv_prepare(r_ref[rows, :], k_ref[rows, :], v_ref[rows, :], lw_all[rows, :],
                              cum_all[rows, :], kn_ref[rows, :], a_ref[rows, :], m)
    n_probs = len(probs)
    split = lambda stacked: [stacked[i * c:(i + 1) * c] for i in range(n_probs)]
    decays = split(_seg_sum(jnp.concatenate([p["g_diag"] for p in probs], axis=0), m))
    rkr = r_ref[...] * k_ref[...] * rk_ref[...]
    bonus_w = split(_seg_sum(jnp.concatenate(
        [rkr[rows, gi * GW:(gi + 1) * GW] for rows in chunk_rows for gi in range(n_groups)], axis=0), m))

    h = (n_chunks // 2) * n_groups
    res_a, res_b = {}, {}
    states = [s_scr[gi] for gi in range(n_groups)]
    ys = []
    _run_interleaved(_wkv_products(probs[:h], m, res_a))
    _run_interleaved(_wkv_recurrence(probs[:h], res_a, decays[:h], states, ys, m),
                     _wkv_products(probs[h:], m, res_b))
    _run_interleaved(_wkv_recurrence(probs[h:], res_b, decays[h:], states, ys, m))
    for gi in range(n_groups):
        s_scr[gi] = states[gi]

    y_all = jnp.concatenate(ys, axis=0)
    d_all = y_all - _seg_sum(y_all, m) * (1.0 / HEAD_DIM)
    inv_all = lax.rsqrt(_seg_sum(d_all * d_all, m) * (1.0 / HEAD_DIM) + RWKV_GN_EPS)
    for ci, rows in enumerate(chunk_rows):
        for gi in range(n_groups):
            i = ci * n_groups + gi
            sl = slice(gi * GW, (gi + 1) * GW)
            yn = d_all[i * c:(i + 1) * c] * inv_all[i * c:(i + 1) * c] * lg_ref[:, sl] + lb_ref[:, sl]
            o_ref[rows, sl] = ((yn + bonus_w[i] * probs[i]["vg"]) * g_ref[rows, sl]).astype(BF16)


def _wkv(r, k, v, lw, kn, a, g, r_k, lnx_g, lnx_b, *, batch, seq, rows):
    T = r.shape[0]
    nb = seq // rows
    blk = pl.BlockSpec((rows, D_RWKV), lambda b, i: (b * nb + i, 0))
    par = pl.BlockSpec((1, D_RWKV), lambda b, i: (0, 0))
    return pl.pallas_call(
        functools.partial(_wkv_kernel, n_chunks=rows // WKV_CHUNK),
        grid=(batch, nb),
        in_specs=[blk] * 7 + [par] * 3,
        out_specs=blk,
        out_shape=jax.ShapeDtypeStruct((T, D_RWKV), BF16),
        scratch_shapes=[pltpu.VMEM((N_RWKV_HEADS // WKV_GROUP, HEAD_DIM, GW), F32)],
        compiler_params=_cparams(("parallel", "arbitrary")),
        name="wkv7",
    )(r, k, v, lw, kn, a, g, r_k.reshape(1, -1), lnx_g.reshape(1, -1), lnx_b.reshape(1, -1))


def _dattn_kernel(q_ref, k_ref, v_ref, lq1_ref, lk1_ref, lq2_ref, lk2_ref, sg_ref, o_ref,
                  qs_scr, m_scr, l_scr, acc_scr, s0_scr, s1_scr, p0_scr, p1_scr, al0_scr, al1_scr,
                  *, tq, lam_init):
    qi = pl.program_id(2)
    q = q_ref[...]
    lane = lax.broadcasted_iota(jnp.int32, q.shape, 1)
    zero = jnp.zeros_like(q)
    qs_scr[:tq, :] = jnp.where(lane < HEAD_DIM, q, zero)
    qs_scr[tq:, :] = jnp.where(lane < HEAD_DIM, zero, q)
    m_scr[...] = jnp.full_like(m_scr, -jnp.inf)
    l_scr[...] = jnp.zeros_like(l_scr)
    acc_scr[...] = jnp.zeros_like(acc_scr)

    bufs = ((s0_scr, p0_scr, al0_scr), (s1_scr, p1_scr, al1_scr))

    def scores(j, par):
        start = pl.multiple_of(j * tq, tq)
        bufs[par][0][...] = _dot_nt(qs_scr[...], k_ref[pl.ds(start, tq), :])

    def softmax(par, masked):
        s_ref, p_ref, al_ref = bufs[par]
        s = s_ref[...]
        if masked:
            r_pos = lax.broadcasted_iota(jnp.int32, s.shape, 0) % tq
            c_pos = lax.broadcasted_iota(jnp.int32, s.shape, 1)
            s = jnp.where(c_pos <= r_pos, s, -jnp.inf)
        m_prev = m_scr[...]
        m_new = jnp.maximum(m_prev, jnp.max(s, axis=-1, keepdims=True))
        alpha = jnp.exp2(m_prev - m_new)
        p = jnp.exp2(s - jnp.concatenate([m_new] * (tq // LANES), axis=1))
        l_scr[...] = alpha * l_scr[...] + jnp.sum(p, axis=-1, keepdims=True)
        m_scr[...] = m_new
        p_ref[...] = p.astype(BF16)
        al_ref[...] = alpha

    def values(j, par):
        _, p_ref, al_ref = bufs[par]
        start = pl.multiple_of(j * tq, tq)
        acc_scr[...] = al_ref[...] * acc_scr[...] + jnp.dot(
            p_ref[...], v_ref[pl.ds(start, tq), :], preferred_element_type=F32)

    scores(0, 0)

    @pl.when(qi == 0)
    def _():
        softmax(0, True)
        values(0, 0)

    @pl.when(qi > 0)
    def _():
        scores(1, 1)
        softmax(0, False)

        def pair(t):
            values(t - 2, 0)
            scores(t, 0)
            softmax(1, False)
            values(t - 1, 1)
            scores(t + 1, 1)
            softmax(0, False)

        n_pairs = (qi - 1) // 2

        def body(i, carry):
            pair(2 + 4 * i)
            pair(4 + 4 * i)
            return carry

        lax.fori_loop(0, n_pairs // 2, body, 0)

        @pl.when(n_pairs % 2 == 1)
        def _():
            pair(2 * n_pairs)

        @pl.when(qi % 2 == 1)
        def _():
            values(qi - 1, 0)
            softmax(1, True)
            values(qi, 1)

        @pl.when(qi % 2 == 0)
        def _():
            values(qi - 2, 0)
            scores(qi, 0)
            softmax(1, False)
            values(qi - 1, 1)
            softmax(0, True)
            values(qi, 0)

    lam = (jnp.exp(jnp.sum(lq1_ref[...] * lk1_ref[...], axis=-1, keepdims=True))
           - jnp.exp(jnp.sum(lq2_ref[...] * lk2_ref[...], axis=-1, keepdims=True)) + lam_init)
    on = acc_scr[...] / l_scr[...]
    o = on[:tq, :] - lam * on[tq:, :]
    o = o * lax.rsqrt(jnp.mean(o * o, axis=-1, keepdims=True) + 1e-5) * sg_ref[...]
    o_ref[...] = (o * (1.0 - lam_init)).astype(BF16)


def _dattn(q, k, v, lq1, lk1, lq2, lk2, subln_g, *, batch, seq, tq, lam_init):
    T = q.shape[0]
    nq = seq // tq
    hd = 2 * HEAD_DIM
    qmap = lambda b, h, i: (b * nq + i, h)
    kvmap = lambda b, h, i: (b, h)
    par = lambda b, h, i: (0, 0)
    vec = lambda a: a.reshape(1, -1)
    return pl.pallas_call(
        functools.partial(_dattn_kernel, tq=tq, lam_init=lam_init),
        grid=(batch, N_DIFF_HEADS, nq),
        in_specs=[
            pl.BlockSpec((tq, hd), qmap),
            pl.BlockSpec((seq, hd), kvmap),
            pl.BlockSpec((seq, hd), kvmap),
            pl.BlockSpec((1, HEAD_DIM), par),
            pl.BlockSpec((1, HEAD_DIM), par),
            pl.BlockSpec((1, HEAD_DIM), par),
            pl.BlockSpec((1, HEAD_DIM), par),
            pl.BlockSpec((1, hd), par),
        ],
        out_specs=pl.BlockSpec((tq, hd), qmap),
        out_shape=jax.ShapeDtypeStruct((T, D_DIFF), BF16),
        scratch_shapes=[
            pltpu.VMEM((2 * tq, hd), BF16),
            pltpu.VMEM((2 * tq, LANES), F32),
            pltpu.VMEM((2 * tq, LANES), F32),
            pltpu.VMEM((2 * tq, hd), F32),
            pltpu.VMEM((2 * tq, tq), F32),
            pltpu.VMEM((2 * tq, tq), F32),
            pltpu.VMEM((2 * tq, tq), BF16),
            pltpu.VMEM((2 * tq, tq), BF16),
            pltpu.VMEM((2 * tq, LANES), F32),
            pltpu.VMEM((2 * tq, LANES), F32),
        ],
        compiler_params=_cparams(("parallel", "parallel", "arbitrary")),
        name="diff_attn",
    )(q, k, v, vec(lq1), vec(lk1), vec(lq2), vec(lk2), vec(subln_g))


def _proj_ln_kernel(yr_ref, yd_ref, w_ref, x_ref, g_ref, b_ref, o_ref):
    mix = jnp.dot(yr_ref[...], w_ref[:D_RWKV, :], preferred_element_type=F32)
    mix = mix + jnp.dot(yd_ref[...], w_ref[D_RWKV:, :], preferred_element_type=F32)
    o_ref[...] = _layer_norm(DEEPNORM_ALPHA * x_ref[...] + mix, g_ref[...], b_ref[...])


def _proj_ln(yr, yd, w_out, x2, g, b, *, tm):
    T = x2.shape[0]
    row = lambda i: (i, 0)
    full = lambda i: (0, 0)
    return pl.pallas_call(
        _proj_ln_kernel,
        grid=(T // tm,),
        in_specs=[
            pl.BlockSpec((tm, D_RWKV), row),
            pl.BlockSpec((tm, D_DIFF), row),
            pl.BlockSpec((D_RWKV + D_DIFF, D_MODEL), full),
            pl.BlockSpec((tm, D_MODEL), row),
            pl.BlockSpec((1, D_MODEL), full),
            pl.BlockSpec((1, D_MODEL), full),
        ],
        out_specs=pl.BlockSpec((tm, D_MODEL), row),
        out_shape=jax.ShapeDtypeStruct((T, D_MODEL), F32),
        compiler_params=_cparams(("parallel",)),
        name="ab_out_ln",
    )(yr, yd, w_out, x2, g.reshape(1, -1), b.reshape(1, -1))


def _gmlp_kernel(x_ref, win_ref, bin_ref, lng_ref, lnb_ref, ws_ref, bs_ref, wout_ref, g_ref, b_ref,
                 o_ref, gated_scr, *, tm):
    x = x_ref[...]
    xb = x.astype(BF16)

    def gelu(h):
        return 0.5 * h * (1.0 + lax.erf(h * (0.5 ** 0.5)))

    hv = jnp.dot(xb, win_ref[:, D_GMLP:], preferred_element_type=F32) + bin_ref[:, D_GMLP:]
    hu = jnp.dot(xb, win_ref[:, :D_GMLP], preferred_element_type=F32) + bin_ref[:, :D_GMLP]
    v = _layer_norm(gelu(hv), lng_ref[...], lnb_ref[...]).astype(BF16)
    u = gelu(hu)
    row = lax.broadcasted_iota(jnp.int32, (CHUNK, CHUNK), 0)
    col = lax.broadcasted_iota(jnp.int32, (CHUNK, CHUNK), 1)
    gw = D_GMLP // GMLP_GROUPS
    n_chunks = tm // CHUNK
    for gi in range(GMLP_GROUPS):
        cs = slice(gi * gw, (gi + 1) * gw)
        ws = jnp.where(row >= col, ws_ref[gi], 0.0).astype(BF16)
        v_side = jnp.concatenate([v[c * CHUNK:(c + 1) * CHUNK, cs] for c in range(n_chunks)], axis=1)
        mixed = jnp.dot(ws, v_side, preferred_element_type=F32)
        for c in range(n_chunks):
            rs = slice(c * CHUNK, (c + 1) * CHUNK)
            gated_scr[rs, cs] = (u[rs, cs] * (mixed[:, c * gw:(c + 1) * gw] + bs_ref[gi])).astype(BF16)
    mix = jnp.dot(gated_scr[...], wout_ref[...], preferred_element_type=F32)
    o_ref[...] = _layer_norm(DEEPNORM_ALPHA * x + mix, g_ref[...], b_ref[...])


def _gmlp(x2, w_in, b_in, ln_g, ln_b, w_s, b_s, w_out, g, b, *, tm):
    T = x2.shape[0]
    row = lambda i: (i, 0)
    full = lambda i: (0, 0)
    full3 = lambda i: (0, 0, 0)
    vec = lambda a: a.reshape(1, -1)
    gw = D_GMLP // GMLP_GROUPS
    bs_b = jnp.broadcast_to(b_s[:, :, None], (GMLP_GROUPS, CHUNK, gw))
    return pl.pallas_call(
        functools.partial(_gmlp_kernel, tm=tm),
        grid=(T // tm,),
        in_specs=[
            pl.BlockSpec((tm, D_MODEL), row),
            pl.BlockSpec((D_MODEL, 2 * D_GMLP), full, pipeline_mode=pl.Buffered(1)),
            pl.BlockSpec((1, 2 * D_GMLP), full),
            pl.BlockSpec((1, D_GMLP), full),
            pl.BlockSpec((1, D_GMLP), full),
            pl.BlockSpec((GMLP_GROUPS, CHUNK, CHUNK), full3),
            pl.BlockSpec((GMLP_GROUPS, CHUNK, gw), full3),
            pl.BlockSpec((D_GMLP, D_MODEL), full, pipeline_mode=pl.Buffered(1)),
            pl.BlockSpec((1, D_MODEL), full),
            pl.BlockSpec((1, D_MODEL), full),
        ],
        out_specs=pl.BlockSpec((tm, D_MODEL), row),
        out_shape=jax.ShapeDtypeStruct((T, D_MODEL), F32),
        scratch_shapes=[pltpu.VMEM((tm, D_GMLP), BF16)],
        compiler_params=_cparams(("parallel",)),
        name="gmlp",
    )(x2, w_in, vec(b_in), vec(ln_g), vec(ln_b), w_s, bs_b, w_out, vec(g), vec(b))


def _ffn_kernel(x_ref, xh_ref, wup_ref, cw_ref, cb_ref, wd_ref, g_ref, b_ref, o_ref,
                *, blocks_per_seq, tf, n_slabs):
    x = x_ref[...]
    tm = x.shape[0]
    xe = jnp.concatenate([x, xh_ref[...]], axis=0).astype(BF16)
    xb = xe[:tm]
    seq_start = pl.program_id(0) % blocks_per_seq == 0

    def up(j):
        cols = slice(j * tf, (j + 1) * tf)
        ge = jnp.dot(xe, wup_ref[:, cols], preferred_element_type=F32)
        val = jnp.dot(xb, wup_ref[:, D_FF + j * tf:D_FF + (j + 1) * tf], preferred_element_type=F32)
        return ge[:tm], val, jnp.where(seq_start, 0.0, ge[tm:])

    acc = None
    nxt = up(0)
    for j in range(n_slabs):
        gate, val, gh = nxt
        if j + 1 < n_slabs:
            nxt = up(j + 1)
        cols = slice(j * tf, (j + 1) * tf)
        cw = cw_ref[:, cols]
        conv = (cb_ref[:, cols] + cw[0:1, :] * _shift_rows(gate, 2, gh)
                + cw[1:2, :] * _shift_rows(gate, 1, gh) + cw[2:3, :] * gate)
        hid = conv * jax.nn.sigmoid(conv) * val
        down = jnp.dot(hid.astype(BF16), wd_ref[j], preferred_element_type=F32)
        acc = down if acc is None else acc + down
    o_ref[...] = _layer_norm(DEEPNORM_ALPHA * x + acc, g_ref[...], b_ref[...])


def _ffn(x2, w_up, conv_w, conv_b, w_down, g, b, *, seq, tm, tf):
    T = x2.shape[0]
    n_slabs = D_FF // tf
    bps = seq // tm
    wup = w_up.astype(BF16)
    cb = conv_b.reshape(1, D_FF)
    wd = w_down.reshape(n_slabs, tf, D_MODEL).astype(BF16)
    row = lambda i: (i, 0)
    full = lambda i: (0, 0)
    full3 = lambda i: (0, 0, 0)
    halo = lambda i: (jnp.maximum(i * (tm // HALO_ROWS) - 1, 0), 0)
    once = pl.Buffered(1)
    return pl.pallas_call(
        functools.partial(_ffn_kernel, blocks_per_seq=bps, tf=tf, n_slabs=n_slabs),
        grid=(T // tm,),
        in_specs=[
            pl.BlockSpec((tm, D_MODEL), row),
            pl.BlockSpec((HALO_ROWS, D_MODEL), halo),
            pl.BlockSpec((D_MODEL, 2 * D_FF), full, pipeline_mode=once),
            pl.BlockSpec((3, D_FF), full, pipeline_mode=once),
            pl.BlockSpec((1, D_FF), full, pipeline_mode=once),
            pl.BlockSpec((n_slabs, tf, D_MODEL), full3, pipeline_mode=once),
            pl.BlockSpec((1, D_MODEL), full),
            pl.BlockSpec((1, D_MODEL), full),
        ],
        out_specs=pl.BlockSpec((tm, D_MODEL), row),
        out_shape=jax.ShapeDtypeStruct((T, D_MODEL), F32),
        compiler_params=_cparams(("parallel",)),
        name="conv_ffn",
    )(x2, x2, wup, conv_w, cb, wd, g.reshape(1, -1), b.reshape(1, -1))


def _rope_tables(seq):
    half = ROPE_DIM // 2
    inv_freq = ROPE_THETA ** (-jnp.arange(0, ROPE_DIM, 2, dtype=F32) / ROPE_DIM)
    ang = jnp.arange(seq, dtype=F32)[:, None] * inv_freq[None, :]
    cos, sin = jnp.cos(ang), jnp.sin(ang)
    ones = jnp.ones((seq, HEAD_DIM - ROPE_DIM), F32)
    zeros = jnp.zeros((seq, HEAD_DIM - ROPE_DIM), F32)
    zh = jnp.zeros((seq, half), F32)
    c64 = jnp.concatenate([cos, cos, ones], axis=1)
    s1_64 = jnp.concatenate([zh, sin, zeros], axis=1)
    s2_64 = jnp.concatenate([-sin, zh, zeros], axis=1)
    two = lambda t: jnp.concatenate([t, t], axis=1)
    return two(c64), two(s1_64), two(s2_64)


def kernel(x, ab_w_in, ab_shift_mu, ab_w0, ab_w2, ab_a0, ab_a2, ab_g2, ab_k_k, ab_k_a, ab_r_k, ab_lnx_g, ab_lnx_b, ab_lam_q1, ab_lam_k1, ab_lam_q2, ab_lam_k2, ab_subln_g, ab_w_out, c_w_in, c_b_in, c_ln_g, c_ln_b, c_w_s, c_b_s, c_w_out, ln1_g, ln1_b, ffn_w_up, ffn_conv_w, ffn_conv_b, ffn_w_down, ln2_g, ln2_b):
    batch, seq, _ = x.shape
    x2 = x.reshape(batch * seq, D_MODEL)
    rc, rs1, rs2 = _rope_tables(seq)
    tm_in = min(512, seq)
    tm_ffn = min(1024, seq)
    tm_gmlp = min(1024, seq)
    tq = min(512, seq)
    wkv_rows = min(512, seq)
    for i in range(DEPTH):
        j = i // 2
        if i % 2 == 0:
            (r, k, v, lw, kn, a, g, qd, kd, vd) = _ab_in(
                x2, ab_w_in[j].astype(BF16), ab_shift_mu[j], ab_w0[j], ab_w2[j], ab_a0[j],
                ab_a2[j].astype(BF16), ab_g2[j].astype(BF16), ab_k_k[j], ab_k_a[j], rc, rs1, rs2,
                seq=seq, tm=tm_in)
            y_r = _wkv(r, k, v, lw, kn, a, g, ab_r_k[j], ab_lnx_g[j], ab_lnx_b[j],
                       batch=batch, seq=seq, rows=wkv_rows)
            lam_init = 0.8 - 0.6 * math.exp(-0.3 * i)
            y_d = _dattn(qd, kd, vd, ab_lam_q1[j], ab_lam_k1[j], ab_lam_q2[j], ab_lam_k2[j],
                         ab_subln_g[j], batch=batch, seq=seq, tq=tq, lam_init=lam_init)
            x2 = _proj_ln(y_r, y_d, ab_w_out[j].astype(BF16), x2, ln1_g[i], ln1_b[i], tm=tm_in)
        else:
            x2 = _gmlp(x2, c_w_in[j].astype(BF16), c_b_in[j], c_ln_g[j], c_ln_b[j], c_w_s[j],
                       c_b_s[j], c_w_out[j].astype(BF16), ln1_g[i], ln1_b[i], tm=tm_gmlp)
        x2 = _ffn(x2, ffn_w_up[i], ffn_conv_w[i], ffn_conv_b[i], ffn_w_down[i], ln2_g[i], ln2_b[i],
                  seq=seq, tm=tm_ffn, tf=256)
    return x2.reshape(batch, seq, D_MODEL)
```

```python
import functools
import math

import jax
import jax.numpy as jnp
from jax import lax
from jax.experimental import pallas as pl
from jax.experimental.pallas import tpu as pltpu

F32 = jnp.float32
BF16 = jnp.bfloat16
HIGHEST = lax.Precision.HIGHEST

D_MODEL = 1024
HEAD_DIM = 64
N_RWKV_HEADS = 8
D_RWKV = N_RWKV_HEADS * HEAD_DIM
N_DIFF_HEADS = 4
D_DIFF = N_DIFF_HEADS * 2 * HEAD_DIM
DECAY_LORA = 64
AAA_LORA = 64
GATE_LORA = 128
RWKV_COLS = 3 * D_RWKV + DECAY_LORA + AAA_LORA + GATE_LORA
AB_COLS = RWKV_COLS + 3 * D_DIFF
RWKV_GN_EPS = 64e-5
ROPE_THETA = 500000.0
ROPE_DIM = HEAD_DIM // 4
CHUNK = 128
GMLP_GROUPS = 8
D_GMLP = D_MODEL
D_FF = 2816
DEPTH = 4
DEEPNORM_ALPHA = (2 * DEPTH) ** 0.25

LANES = 128
HALO_ROWS = 16
WKV_CHUNK = 64
WKV_INV_BASE = 16
VMEM_LIMIT = 56 * 1024 * 1024
LOG2E = math.log2(math.e)


def _cparams(sem):
    return pltpu.CompilerParams(dimension_semantics=sem, vmem_limit_bytes=VMEM_LIMIT)


def _dot(a, b):
    return jnp.dot(a.astype(BF16), b.astype(BF16), preferred_element_type=F32)


def _dot_nt(a, b):
    return lax.dot_general(a.astype(BF16), b.astype(BF16), (((1,), (1,)), ((), ())),
                           preferred_element_type=F32)


def _layer_norm(z, g, b, eps=1e-5):
    mu = jnp.mean(z, axis=-1, keepdims=True)
    d = z - mu
    var = jnp.mean(d * d, axis=-1, keepdims=True)
    return d * lax.rsqrt(var + eps) * g + b


def _shift_rows(t, k, halo):
    rolled = pltpu.roll(t, k, 0)
    row = lax.broadcasted_iota(jnp.int32, t.shape, 0)
    out = rolled
    for j in range(k):
        out = jnp.where(row == j, halo[HALO_ROWS - k + j:HALO_ROWS - k + j + 1, :], out)
    return out


def _ab_in_kernel(x_ref, xh_ref, w_ref, mu_ref, w0_ref, w2_ref, a0_ref, a2_ref, g2_ref, kk_ref,
                  ka_ref, rc_ref, rs1_ref, rs2_ref,
                  r_out, k_out, v_out, lw_out, kn_out, a_out, g_out, q_out, kd_out, vd_out,
                  *, blocks_per_seq):
    seq_start = pl.program_id(0) % blocks_per_seq == 0
    tm = x_ref.shape[0]
    xe = jnp.concatenate([x_ref[...], xh_ref[...]], axis=0).astype(BF16)
    xb = xe[:tm]

    def proj(c0, c1):
        return jnp.dot(xb, w_ref[:, c0:c1], preferred_element_type=F32)

    def proj_shifted(c0, c1):
        pe = jnp.dot(xe, w_ref[:, c0:c1], preferred_element_type=F32)
        p = pe[:tm]
        ph = jnp.where(seq_start, 0.0, pe[tm:])
        return p + mu_ref[:, c0:c1] * (_shift_rows(p, 1, ph) - p)

    o = 3 * D_RWKV
    pq = proj(RWKV_COLS, RWKV_COLS + D_DIFF)
    pk = proj(RWKV_COLS + D_DIFF, RWKV_COLS + 2 * D_DIFF)
    xl = proj_shifted(o, RWKV_COLS)
    k = proj_shifted(D_RWKV, 2 * D_RWKV)
    r = proj_shifted(0, D_RWKV)
    v = proj_shifted(2 * D_RWKV, o)
    vd_out[...] = proj(RWKV_COLS + 2 * D_DIFF, AB_COLS).astype(BF16)
    xw = xl[:, :DECAY_LORA]
    xa = xl[:, DECAY_LORA:DECAY_LORA + AAA_LORA]
    xg = xl[:, DECAY_LORA + AAA_LORA:]

    rc = jnp.concatenate([rc_ref[...]] * (D_DIFF // LANES), axis=1)
    rs1 = jnp.concatenate([rs1_ref[...]] * (D_DIFF // LANES), axis=1)
    rs2 = jnp.concatenate([rs2_ref[...]] * (D_DIFF // LANES), axis=1)
    half = ROPE_DIM // 2

    def rope(t):
        return t * rc + pltpu.roll(t, half, 1) * rs1 + pltpu.roll(t, D_DIFF - half, 1) * rs2

    q_out[...] = (rope(pq) * (HEAD_DIM ** -0.5 * LOG2E)).astype(BF16)
    kd_out[...] = rope(pk).astype(BF16)

    z = w0_ref[...] + jnp.dot(jnp.tanh(xw), w2_ref[...], precision=HIGHEST,
                              preferred_element_type=F32)
    softplus_neg = jnp.maximum(-z, 0.0) + jnp.log1p(jnp.exp(-jnp.abs(z)))
    w = -softplus_neg - 0.5
    lw_out[...] = -jnp.exp(w)
    a = jax.nn.sigmoid(a0_ref[...] + _dot(xa, a2_ref[...]))
    a_out[...] = a
    g_out[...] = _dot(jax.nn.sigmoid(xg), g2_ref[...])

    kx = k * kk_ref[...]
    sq = kx * kx
    lane = lax.broadcasted_iota(jnp.int32, (sq.shape[0], LANES), 1)
    lo = lane < HEAD_DIM
    for c in range(D_RWKV // LANES):
        blk = sq[:, c * LANES:(c + 1) * LANES]
        n_lo = jnp.sqrt(jnp.sum(jnp.where(lo, blk, 0.0), axis=-1, keepdims=True))
        n_hi = jnp.sqrt(jnp.sum(jnp.where(lo, 0.0, blk), axis=-1, keepdims=True))
        norm = jnp.maximum(jnp.where(lo, n_lo, n_hi), 1e-12)
        kn_out[:, c * LANES:(c + 1) * LANES] = kx[:, c * LANES:(c + 1) * LANES] / norm
    r_out[...] = r
    k_out[...] = k * (1.0 + (a - 1.0) * ka_ref[...])
    v_out[...] = v


def _ab_in(x2, w_in, mu, w0, w2, a0, a2, g2, k_k, k_a, rc, rs1, rs2, *, seq, tm):
    T = x2.shape[0]
    n = T // tm
    bps = seq // tm
    row = lambda i: (i, 0)
    full = lambda i: (0, 0)
    halo = lambda i: (jnp.maximum(i * (tm // HALO_ROWS) - 1, 0), 0)
    rope_map = lambda i: (i % bps, 0)
    vec = lambda a: a.reshape(1, -1)
    f32_out = jax.ShapeDtypeStruct((T, D_RWKV), F32)
    bf_out = jax.ShapeDtypeStruct((T, D_DIFF), BF16)
    out_spec = pl.BlockSpec((tm, D_RWKV), row)
    return pl.pallas_call(
        functools.partial(_ab_in_kernel, blocks_per_seq=bps),
        grid=(n,),
        in_specs=[
            pl.BlockSpec((tm, D_MODEL), row),
            pl.BlockSpec((HALO_ROWS, D_MODEL), halo),
            pl.BlockSpec((D_MODEL, AB_COLS), full),
            pl.BlockSpec((1, RWKV_COLS), full),
            pl.BlockSpec((1, D_RWKV), full),
            pl.BlockSpec((DECAY_LORA, D_RWKV), full),
            pl.BlockSpec((1, D_RWKV), full),
            pl.BlockSpec((AAA_LORA, D_RWKV), full),
            pl.BlockSpec((GATE_LORA, D_RWKV), full),
            pl.BlockSpec((1, D_RWKV), full),
            pl.BlockSpec((1, D_RWKV), full),
            pl.BlockSpec((tm, LANES), rope_map),
            pl.BlockSpec((tm, LANES), rope_map),
            pl.BlockSpec((tm, LANES), rope_map),
        ],
        out_specs=[out_spec] * 10,
        out_shape=[f32_out] * 7 + [bf_out] * 3,
        compiler_params=_cparams(("parallel",)),
        name="ab_in",
    )(x2, x2, w_in, vec(mu), vec(w0), w2, vec(a0), a2, g2, vec(k_k), vec(k_a), rc, rs1, rs2)


WKV_GROUP = 4
GW = WKV_GROUP * HEAD_DIM


def _wkv_masks():
    c = WKV_CHUNK
    row = lax.broadcasted_iota(jnp.int32, (c, GW), 0)
    col = lax.broadcasted_iota(jnp.int32, (c, GW), 1) % HEAD_DIM
    brow = lax.broadcasted_iota(jnp.int32, (GW, GW), 0) // HEAD_DIM
    bcol = lax.broadcasted_iota(jnp.int32, (GW, GW), 1) // HEAD_DIM
    bd = jnp.where(brow == bcol, 1.0, 0.0).astype(BF16)
    brow2 = lax.broadcasted_iota(jnp.int32, (2 * c * WKV_GROUP, GW), 0) // (2 * c)
    bcol2 = lax.broadcasted_iota(jnp.int32, (2 * c * WKV_GROUP, GW), 1) // HEAD_DIM
    bd2 = jnp.where(brow2 == bcol2, 1.0, 0.0).astype(BF16)
    strict_incl = jnp.concatenate([row > col, row >= col], axis=0)
    return dict(row=row, col=col, bd=bd, bd2=bd2, strict_incl=strict_incl)


def _chunk_cumsum(lw, n_chunks):
    n = 2 * WKV_CHUNK
    r = lax.broadcasted_iota(jnp.int32, (n, n), 0)
    c = lax.broadcasted_iota(jnp.int32, (n, n), 1)
    tri = jnp.where(jnp.logical_and(r // WKV_CHUNK == c // WKV_CHUNK, r >= c), 1.0, 0.0).astype(BF16)
    h1 = lw.astype(BF16)
    rem = lw - h1.astype(F32)
    h2 = rem.astype(BF16)
    h3 = (rem - h2.astype(F32)).astype(BF16)
    out = []
    for b in range(n_chunks // 2):
        rows = slice(b * n, (b + 1) * n)
        cum = jnp.dot(tri, h1[rows], preferred_element_type=F32)
        cum = cum + jnp.dot(tri, h2[rows], preferred_element_type=F32)
        out.append(cum + jnp.dot(tri, h3[rows], preferred_element_type=F32))
    return jnp.concatenate(out, axis=0)


def _bd(w, m):
    return jnp.concatenate([w.astype(BF16)] * WKV_GROUP, axis=0) * m["bd"]


def _mm(x, w_bd):
    return jnp.dot(x.astype(BF16), w_bd, preferred_element_type=F32)


def _mm_nt(x, w_bd):
    return lax.dot_general(x.astype(BF16), w_bd, (((1,), (1,)), ((), ())),
                           preferred_element_type=F32)


def _seg_sum(x, m):
    hi = x.astype(BF16)
    lo = (x - hi.astype(F32)).astype(BF16)
    s = jnp.dot(jnp.concatenate([hi, lo], axis=0), m["bd"], preferred_element_type=F32)
    return s[:x.shape[0]] + s[x.shape[0]:]


def _run_interleaved(*gens):
    gens = list(gens)
    while gens:
        for g in list(gens):
            try:
                next(g)
            except StopIteration:
                gens.remove(g)


def _unit_lower_inverse(n_mats, m):
    row, col = m["row"], m["col"]
    eye = jnp.where(row == col, 1.0, 0.0)
    same_base = (row // WKV_INV_BASE) == (col // WKV_INV_BASE)
    c = WKV_CHUNK
    pws = [jnp.where(same_base, n, 0.0) for n in n_mats]
    ts = [eye + p for p in pws]
    pws = [_mm(p, _bd(p, m)) for p in pws]
    yield
    n_levels = WKV_INV_BASE.bit_length() - 2
    for lvl in range(n_levels):
        pw_bds = [_bd(p, m) for p in pws]
        if lvl < n_levels - 1:
            both = [_mm(jnp.concatenate([t, p], axis=0), b) for t, p, b in zip(ts, pws, pw_bds)]
            ts = [t + bo[:c] for t, bo in zip(ts, both)]
            pws = [bo[c:] for bo in both]
        else:
            ts = [t + _mm(t, b) for t, b in zip(ts, pw_bds)]
        yield
    blk = WKV_INV_BASE
    while blk < WKV_CHUNK:
        same_lo = (row // blk) == (col // blk)
        same_hi = (row // (2 * blk)) == (col // (2 * blk))
        sel = jnp.logical_and(same_hi, jnp.logical_not(same_lo))
        halves = [_mm(t, _bd(jnp.where(sel, n, 0.0), m)) for t, n in zip(ts, n_mats)]
        yield
        ts = [t + _mm(h, _bd(t, m)) for t, h in zip(ts, halves)]
        yield
        blk *= 2
    return ts


def _wkv_products(probs, m, res):
    c = WKV_CHUNK
    ts = yield from _unit_lower_inverse([p["a_ab"] for p in probs], m)
    vg_bds = [_bd(p["vg"], m) for p in probs]
    akv_yvs = [_mm(p["ak"], vb) for p, vb in zip(probs, vg_bds)]
    yield
    t_bfs = [t.astype(BF16) for t in ts]
    xas = [_mm(t, _bd(p["a0"], m)) for t, p in zip(t_bfs, probs)]
    yield
    res["xvs"] = [_mm(t, _bd(ay[:c], m)) for t, ay in zip(t_bfs, akv_yvs)]
    res["xrs"] = [jnp.concatenate([xa, p["r0"]], axis=0).astype(BF16) for xa, p in zip(xas, probs)]
    res["yvs"] = [ay[c:] for ay in akv_yvs]
    yield


def _wkv_recurrence(probs, res, decays, states, ys, m):
    c = WKV_CHUNK
    n_groups = len(states)
    for c0 in range(0, len(probs), n_groups):
        us = []
        for gi in range(n_groups):
            i = c0 + gi
            uy = _mm(res["xrs"][i], _bd(states[gi], m))
            us.append((uy[:c] + res["xvs"][i], uy[c:]))
        yield
        for gi in range(n_groups):
            i = c0 + gi
            p = probs[i]
            u, y0 = us[gi]
            ys.append(y0 + _mm(p["a_rb"], _bd(u, m)) + res["yvs"][i])
            uv = jnp.concatenate([u, p["vg"]], axis=0).astype(BF16)
            uv_bd = jnp.concatenate([uv] * WKV_GROUP, axis=0) * m["bd2"]
            states[gi] = states[gi] * decays[c0 + gi] + jnp.dot(
                p["bk_t"], uv_bd, preferred_element_type=F32)
        yield


def _wkv_prepare(r, k, v, lw, cum, kn, a, m):
    c = r.shape[0]
    mid = cum[c // 2 - 1:c // 2, :]
    end = cum[c - 1:c, :]
    e_abs = jnp.exp(cum)
    e_abs_prev = jnp.exp(cum - lw)
    e_mid = jnp.exp(-mid)
    e_neg = jnp.exp(mid - cum)
    e_end = jnp.exp(end - cum)
    g_end = jnp.exp(end)
    bvec = kn * a
    r0 = r * e_abs
    a0 = -kn * e_abs_prev
    rt = r0 * e_mid
    at = a0 * e_mid
    kt = k * e_neg
    bt = bvec * e_neg
    kh = k * e_end
    bh = bvec * e_end
    out = []
    for gi in range(N_RWKV_HEADS // WKV_GROUP):
        sl = slice(gi * GW, (gi + 1) * GW)
        lhs = jnp.concatenate([at[:, sl], rt[:, sl]], axis=0)
        ab = jnp.where(m["strict_incl"], _mm_nt(lhs, _bd(bt[:, sl], m)), 0.0)
        ak = jnp.where(m["strict_incl"], _mm_nt(lhs, _bd(kt[:, sl], m)), 0.0)
        bk_t = jnp.transpose(jnp.concatenate([bh[:, sl], kh[:, sl]], axis=0))
        bk_t = jnp.concatenate([bk_t[hh * HEAD_DIM:(hh + 1) * HEAD_DIM] for hh in range(WKV_GROUP)],
                               axis=1).astype(BF16)
        g_diag = jnp.where(m["row"] == m["col"], g_end[:, sl], 0.0)
        out.append(dict(a_ab=ab[:c], a_rb=ab[c:], ak=ak, vg=v[:, sl], a0=a0[:, sl], r0=r0[:, sl],
                        g_diag=g_diag, bk_t=bk_t))
    return out


def _wkv_kernel(r_ref, k_ref, v_ref, lw_ref, kn_ref, a_ref, g_ref, rk_ref, lg_ref, lb_ref,
                o_ref, s_scr, *, n_chunks):
    @pl.when(pl.program_id(1) == 0)
    def _():
        s_scr[...] = jnp.zeros_like(s_scr)

    n_groups = N_RWKV_HEADS // WKV_GROUP
    m = _wkv_masks()
    chunk_rows = [slice(ci * WKV_CHUNK, (ci + 1) * WKV_CHUNK) for ci in range(n_chunks)]

    c = WKV_CHUNK
    lw_all = lw_ref[...]
    cum_all = _chunk_cumsum(lw_all, n_chunks)
    probs = []
    for rows in chunk_rows:
        probs += _wkv_prepare(r_ref[rows, :], k_ref[rows, :], v_ref[rows, :], lw_all[rows, :],
                              cum_all[rows, :], kn_ref[rows, :], a_ref[rows, :], m)
    n_probs = len(probs)
    split = lambda stacked: [stacked[i * c:(i + 1) * c] for i in range(n_probs)]
    decays = split(_seg_sum(jnp.concatenate([p["g_diag"] for p in probs], axis=0), m))
    rkr = r_ref[...] * k_ref[...] * rk_ref[...]
    bonus_w = split(_seg_sum(jnp.concatenate(
        [rkr[rows, gi * GW:(gi + 1) * GW] for rows in chunk_rows for gi in range(n_groups)], axis=0), m))

    h = (n_chunks // 2) * n_groups
    res_a, res_b = {}, {}
    states = [s_scr[gi] for gi in range(n_groups)]
    ys = []
    _run_interleaved(_wkv_products(probs[:h], m, res_a))
    _run_interleaved(_wkv_recurrence(probs[:h], res_a, decays[:h], states, ys, m),
                     _wkv_products(probs[h:], m, res_b))
    _run_interleaved(_wkv_recurrence(probs[h:], res_b, decays[h:], states, ys, m))
    for gi in range(n_groups):
        s_scr[gi] = states[gi]

    y_all = jnp.concatenate(ys, axis=0)
    d_all = y_all - _seg_sum(y_all, m) * (1.0 / HEAD_DIM)
    inv_all = lax.rsqrt(_seg_sum(d_all * d_all, m) * (1.0 / HEAD_DIM) + RWKV_GN_EPS)
    for ci, rows in enumerate(chunk_rows):
        for gi in range(n_groups):
            i = ci * n_groups + gi
            sl = slice(gi * GW, (gi + 1) * GW)
            yn = d_all[i * c:(i + 1) * c] * inv_all[i * c:(i + 1) * c] * lg_ref[:, sl] + lb_ref[:, sl]
            o_ref[rows, sl] = ((yn + bonus_w[i] * probs[i]["vg"]) * g_ref[rows, sl]).astype(BF16)


def _wkv(r, k, v, lw, kn, a, g, r_k, lnx_g, lnx_b, *, batch, seq, rows):
    T = r.shape[0]
    nb = seq // rows
    blk = pl.BlockSpec((rows, D_RWKV), lambda b, i: (b * nb + i, 0))
    par = pl.BlockSpec((1, D_RWKV), lambda b, i: (0, 0))
    return pl.pallas_call(
        functools.partial(_wkv_kernel, n_chunks=rows // WKV_CHUNK),
        grid=(batch, nb),
        in_specs=[blk] * 7 + [par] * 3,
        out_specs=blk,
        out_shape=jax.ShapeDtypeStruct((T, D_RWKV), BF16),
        scratch_shapes=[pltpu.VMEM((N_RWKV_HEADS // WKV_GROUP, HEAD_DIM, GW), F32)],
        compiler_params=_cparams(("parallel", "arbitrary")),
        name="wkv7",
    )(r, k, v, lw, kn, a, g, r_k.reshape(1, -1), lnx_g.reshape(1, -1), lnx_b.reshape(1, -1))


def _dattn_kernel(q_ref, k_ref, v_ref, lq1_ref, lk1_ref, lq2_ref, lk2_ref, sg_ref, o_ref,
                  qs_scr, m_scr, l_scr, acc_scr, s0_scr, s1_scr, p0_scr, p1_scr, al0_scr, al1_scr,
                  *, tq, lam_init):
    qi = pl.program_id(2)
    q = q_ref[...]
    lane = lax.broadcasted_iota(jnp.int32, q.shape, 1)
    zero = jnp.zeros_like(q)
    qs_scr[:tq, :] = jnp.where(lane < HEAD_DIM, q, zero)
    qs_scr[tq:, :] = jnp.where(lane < HEAD_DIM, zero, q)
    m_scr[...] = jnp.full_like(m_scr, -jnp.inf)
    l_scr[...] = jnp.zeros_like(l_scr)
    acc_scr[...] = jnp.zeros_like(acc_scr)

    bufs = ((s0_scr, p0_scr, al0_scr), (s1_scr, p1_scr, al1_scr))

    def scores(j, par):
        start = pl.multiple_of(j * tq, tq)
        bufs[par][0][...] = _dot_nt(qs_scr[...], k_ref[pl.ds(start, tq), :])

    def softmax(par, masked):
        s_ref, p_ref, al_ref = bufs[par]
        s = s_ref[...]
        if masked:
            r_pos = lax.broadcasted_iota(jnp.int32, s.shape, 0) % tq
            c_pos = lax.broadcasted_iota(jnp.int32, s.shape, 1)
            s = jnp.where(c_pos <= r_pos, s, -jnp.inf)
        m_prev = m_scr[...]
        m_new = jnp.maximum(m_prev, jnp.max(s, axis=-1, keepdims=True))
        alpha = jnp.exp2(m_prev - m_new)
        p = jnp.exp2(s - jnp.concatenate([m_new] * (tq // LANES), axis=1))
        l_scr[...] = alpha * l_scr[...] + jnp.sum(p, axis=-1, keepdims=True)
        m_scr[...] = m_new
        p_ref[...] = p.astype(BF16)
        al_ref[...] = alpha

    def values(j, par):
        _, p_ref, al_ref = bufs[par]
        start = pl.multiple_of(j * tq, tq)
        acc_scr[...] = al_ref[...] * acc_scr[...] + jnp.dot(
            p_ref[...], v_ref[pl.ds(start, tq), :], preferred_element_type=F32)

    scores(0, 0)

    @pl.when(qi == 0)
    def _():
        softmax(0, True)
        values(0, 0)

    @pl.when(qi > 0)
    def _():
        scores(1, 1)
        softmax(0, False)

        def pair(t):
            values(t - 2, 0)
            scores(t, 0)
            softmax(1, False)
            values(t - 1, 1)
            scores(t + 1, 1)
            softmax(0, False)

        n_pairs = (qi - 1) // 2

        def body(i, carry):
            pair(2 + 4 * i)
            pair(4 + 4 * i)
            return carry

        lax.fori_loop(0, n_pairs // 2, body, 0)

        @pl.when(n_pairs % 2 == 1)
        def _():
            pair(2 * n_pairs)

        @pl.when(qi % 2 == 1)
        def _():
            values(qi - 1, 0)
            softmax(1, True)
            values(qi, 1)

        @pl.when(qi % 2 == 0)
        def _():
            values(qi - 2, 0)
            scores(qi, 0)
            softmax(1, False)
            values(qi - 1, 1)
            softmax(0, True)
            values(qi, 0)

    lam = (jnp.exp(jnp.sum(lq1_ref[...] * lk1_ref[...], axis=-1, keepdims=True))
           - jnp.exp(jnp.sum(lq2_ref[...] * lk2_ref[...], axis=-1, keepdims=True)) + lam_init)
    on = acc_scr[...] / l_scr[...]
    o = on[:tq, :] - lam * on[tq:, :]
    o = o * lax.rsqrt(jnp.mean(o * o, axis=-1, keepdims=True) + 1e-5) * sg_ref[...]
    o_ref[...] = (o * (1.0 - lam_init)).astype(BF16)


def _dattn(q, k, v, lq1, lk1, lq2, lk2, subln_g, *, batch, seq, tq, lam_init):
    T = q.shape[0]
    nq = seq // tq
    hd = 2 * HEAD_DIM
    qmap = lambda b, h, i: (b * nq + i, h)
    kvmap = lambda b, h, i: (b, h)
    par = lambda b, h, i: (0, 0)
    vec = lambda a: a.reshape(1, -1)
    return pl.pallas_call(
        functools.partial(_dattn_kernel, tq=tq, lam_init=lam_init),
        grid=(batch, N_DIFF_HEADS, nq),
        in_specs=[
            pl.BlockSpec((tq, hd), qmap),
            pl.BlockSpec((seq, hd), kvmap),
            pl.BlockSpec((seq, hd), kvmap),
            pl.BlockSpec((1, HEAD_DIM), par),
            pl.BlockSpec((1, HEAD_DIM), par),
            pl.BlockSpec((1, HEAD_DIM), par),
            pl.BlockSpec((1, HEAD_DIM), par),
            pl.BlockSpec((1, hd), par),
        ],
        out_specs=pl.BlockSpec((tq, hd), qmap),
        out_shape=jax.ShapeDtypeStruct((T, D_DIFF), BF16),
        scratch_shapes=[
            pltpu.VMEM((2 * tq, hd), BF16),
            pltpu.VMEM((2 * tq, LANES), F32),
            pltpu.VMEM((2 * tq, LANES), F32),
            pltpu.VMEM((2 * tq, hd), F32),
            pltpu.VMEM((2 * tq, tq), F32),
            pltpu.VMEM((2 * tq, tq), F32),
            pltpu.VMEM((2 * tq, tq), BF16),
            pltpu.VMEM((2 * tq, tq), BF16),
            pltpu.VMEM((2 * tq, LANES), F32),
            pltpu.VMEM((2 * tq, LANES), F32),
        ],
        compiler_params=_cparams(("parallel", "parallel", "arbitrary")),
        name="diff_attn",
    )(q, k, v, vec(lq1), vec(lk1), vec(lq2), vec(lk2), vec(subln_g))


def _proj_ln_kernel(yr_ref, yd_ref, w_ref, x_ref, g_ref, b_ref, o_ref):
    mix = jnp.dot(yr_ref[...], w_ref[:D_RWKV, :], preferred_element_type=F32)
    mix = mix + jnp.dot(yd_ref[...], w_ref[D_RWKV:, :], preferred_element_type=F32)
    o_ref[...] = _layer_norm(DEEPNORM_ALPHA * x_ref[...] + mix, g_ref[...], b_ref[...])


def _proj_ln(yr, yd, w_out, x2, g, b, *, tm):
    T = x2.shape[0]
    row = lambda i: (i, 0)
    full = lambda i: (0, 0)
    return pl.pallas_call(
        _proj_ln_kernel,
        grid=(T // tm,),
        in_specs=[
            pl.BlockSpec((tm, D_RWKV), row),
            pl.BlockSpec((tm, D_DIFF), row),
            pl.BlockSpec((D_RWKV + D_DIFF, D_MODEL), full),
            pl.BlockSpec((tm, D_MODEL), row),
            pl.BlockSpec((1, D_MODEL), full),
            pl.BlockSpec((1, D_MODEL), full),
        ],
        out_specs=pl.BlockSpec((tm, D_MODEL), row),
        out_shape=jax.ShapeDtypeStruct((T, D_MODEL), F32),
        compiler_params=_cparams(("parallel",)),
        name="ab_out_ln",
    )(yr, yd, w_out, x2, g.reshape(1, -1), b.reshape(1, -1))


def _gmlp_kernel(x_ref, win_ref, bin_ref, lng_ref, lnb_ref, ws_ref, bs_ref, wout_ref, g_ref, b_ref,
                 o_ref, gated_scr, *, tm):
    x = x_ref[...]
    xb = x.astype(BF16)

    def gelu(h):
        return 0.5 * h * (1.0 + lax.erf(h * (0.5 ** 0.5)))

    hv = jnp.dot(xb, win_ref[:, D_GMLP:], preferred_element_type=F32) + bin_ref[:, D_GMLP:]
    hu = jnp.dot(xb, win_ref[:, :D_GMLP], preferred_element_type=F32) + bin_ref[:, :D_GMLP]
    v = _layer_norm(gelu(hv), lng_ref[...], lnb_ref[...]).astype(BF16)
    u = gelu(hu)
    row = lax.broadcasted_iota(jnp.int32, (CHUNK, CHUNK), 0)
    col = lax.broadcasted_iota(jnp.int32, (CHUNK, CHUNK), 1)
    gw = D_GMLP // GMLP_GROUPS
    n_chunks = tm // CHUNK
    for gi in range(GMLP_GROUPS):
        cs = slice(gi * gw, (gi + 1) * gw)
        ws = jnp.where(row >= col, ws_ref[gi], 0.0).astype(BF16)
        v_side = jnp.concatenate([v[c * CHUNK:(c + 1) * CHUNK, cs] for c in range(n_chunks)], axis=1)
        mixed = jnp.dot(ws, v_side, preferred_element_type=F32)
        for c in range(n_chunks):
            rs = slice(c * CHUNK, (c + 1) * CHUNK)
            gated_scr[rs, cs] = (u[rs, cs] * (mixed[:, c * gw:(c + 1) * gw] + bs_ref[gi])).astype(BF16)
    mix = jnp.dot(gated_scr[...], wout_ref[...], preferred_element_type=F32)
    o_ref[...] = _layer_norm(DEEPNORM_ALPHA * x + mix, g_ref[...], b_ref[...])


def _gmlp(x2, w_in, b_in, ln_g, ln_b, w_s, b_s, w_out, g, b, *, tm):
    T = x2.shape[0]
    row = lambda i: (i, 0)
    full = lambda i: (0, 0)
    full3 = lambda i: (0, 0, 0)
    vec = lambda a: a.reshape(1, -1)
    gw = D_GMLP // GMLP_GROUPS
    bs_b = jnp.broadcast_to(b_s[:, :, None], (GMLP_GROUPS, CHUNK, gw))
    return pl.pallas_call(
        functools.partial(_gmlp_kernel, tm=tm),
        grid=(T // tm,),
        in_specs=[
            pl.BlockSpec((tm, D_MODEL), row),
            pl.BlockSpec((D_MODEL, 2 * D_GMLP), full, pipeline_mode=pl.Buffered(1)),
            pl.BlockSpec((1, 2 * D_GMLP), full),
            pl.BlockSpec((1, D_GMLP), full),
            pl.BlockSpec((1, D_GMLP), full),
            pl.BlockSpec((GMLP_GROUPS, CHUNK, CHUNK), full3),
            pl.BlockSpec((GMLP_GROUPS, CHUNK, gw), full3),
            pl.BlockSpec((D_GMLP, D_MODEL), full, pipeline_mode=pl.Buffered(1)),
            pl.BlockSpec((1, D_MODEL), full),
            pl.BlockSpec((1, D_MODEL), full),
        ],
        out_specs=pl.BlockSpec((tm, D_MODEL), row),
        out_shape=jax.ShapeDtypeStruct((T, D_MODEL), F32),
        scratch_shapes=[pltpu.VMEM((tm, D_GMLP), BF16)],
        compiler_params=_cparams(("parallel",)),
        name="gmlp",
    )(x2, w_in, vec(b_in), vec(ln_g), vec(ln_b), w_s, bs_b, w_out, vec(g), vec(b))


def _ffn_kernel(x_ref, xh_ref, wup_ref, cw_ref, cb_ref, wd_ref, g_ref, b_ref, o_ref,
                *, blocks_per_seq, tf, n_slabs):
    x = x_ref[...]
    tm = x.shape[0]
    xe = jnp.concatenate([x, xh_ref[...]], axis=0).astype(BF16)
    xb = xe[:tm]
    seq_start = pl.program_id(0) % blocks_per_seq == 0

    def up(j):
        cols = slice(j * tf, (j + 1) * tf)
        ge = jnp.dot(xe, wup_ref[:, cols], preferred_element_type=F32)
        val = jnp.dot(xb, wup_ref[:, D_FF + j * tf:D_FF + (j + 1) * tf], preferred_element_type=F32)
        return ge[:tm], val, jnp.where(seq_start, 0.0, ge[tm:])

    acc = None
    nxt = up(0)
    for j in range(n_slabs):
        gate, val, gh = nxt
        if j + 1 < n_slabs:
            nxt = up(j + 1)
        cols = slice(j * tf, (j + 1) * tf)
        cw = cw_ref[:, cols]
        conv = (cb_ref[:, cols] + cw[0:1, :] * _shift_rows(gate, 2, gh)
                + cw[1:2, :] * _shift_rows(gate, 1, gh) + cw[2:3, :] * gate)
        hid = conv * jax.nn.sigmoid(conv) * val
        down = jnp.dot(hid.astype(BF16), wd_ref[j], preferred_element_type=F32)
        acc = down if acc is None else acc + down
    o_ref[...] = _layer_norm(DEEPNORM_ALPHA * x + acc, g_ref[...], b_ref[...])


def _ffn(x2, w_up, conv_w, conv_b, w_down, g, b, *, seq, tm, tf):
    T = x2.shape[0]
    n_slabs = D_FF // tf
    bps = seq // tm
    wup = w_up.astype(BF16)
    cb = conv_b.reshape(1, D_FF)
    wd = w_down.reshape(n_slabs, tf, D_MODEL).astype(BF16)
    row = lambda i: (i, 0)
    full = lambda i: (0, 0)
    full3 = lambda i: (0, 0, 0)
    halo = lambda i: (jnp.maximum(i * (tm // HALO_ROWS) - 1, 0), 0)
    once = pl.Buffered(1)
    return pl.pallas_call(
        functools.partial(_ffn_kernel, blocks_per_seq=bps, tf=tf, n_slabs=n_slabs),
        grid=(T // tm,),
        in_specs=[
            pl.BlockSpec((tm, D_MODEL), row),
            pl.BlockSpec((HALO_ROWS, D_MODEL), halo),
            pl.BlockSpec((D_MODEL, 2 * D_FF), full, pipeline_mode=once),
            pl.BlockSpec((3, D_FF), full, pipeline_mode=once),
            pl.BlockSpec((1, D_FF), full, pipeline_mode=once),
            pl.BlockSpec((n_slabs, tf, D_MODEL), full3, pipeline_mode=once),
            pl.BlockSpec((1, D_MODEL), full),
            pl.BlockSpec((1, D_MODEL), full),
        ],
        out_specs=pl.BlockSpec((tm, D_MODEL), row),
        out_shape=jax.ShapeDtypeStruct((T, D_MODEL), F32),
        compiler_params=_cparams(("parallel",)),
        name="conv_ffn",
    )(x2, x2, wup, conv_w, cb, wd, g.reshape(1, -1), b.reshape(1, -1))


def _rope_tables(seq):
    half = ROPE_DIM // 2
    inv_freq = ROPE_THETA ** (-jnp.arange(0, ROPE_DIM, 2, dtype=F32) / ROPE_DIM)
    ang = jnp.arange(seq, dtype=F32)[:, None] * inv_freq[None, :]
    cos, sin = jnp.cos(ang), jnp.sin(ang)
    ones = jnp.ones((seq, HEAD_DIM - ROPE_DIM), F32)
    zeros = jnp.zeros((seq, HEAD_DIM - ROPE_DIM), F32)
    zh = jnp.zeros((seq, half), F32)
    c64 = jnp.concatenate([cos, cos, ones], axis=1)
    s1_64 = jnp.concatenate([zh, sin, zeros], axis=1)
    s2_64 = jnp.concatenate([-sin, zh, zeros], axis=1)
    two = lambda t: jnp.concatenate([t, t], axis=1)
    return two(c64), two(s1_64), two(s2_64)


def kernel(x, ab_w_in, ab_shift_mu, ab_w0, ab_w2, ab_a0, ab_a2, ab_g2, ab_k_k, ab_k_a, ab_r_k, ab_lnx_g, ab_lnx_b, ab_lam_q1, ab_lam_k1, ab_lam_q2, ab_lam_k2, ab_subln_g, ab_w_out, c_w_in, c_b_in, c_ln_g, c_ln_b, c_w_s, c_b_s, c_w_out, ln1_g, ln1_b, ffn_w_up, ffn_conv_w, ffn_conv_b, ffn_w_down, ln2_g, ln2_b):
    batch, seq, d_model = x.shape
    assert d_model == D_MODEL and x.dtype == F32
    x2 = x.reshape(batch * seq, D_MODEL)
    rc, rs1, rs2 = _rope_tables(seq)
    tm_in = min(512, seq)
    tm_ffn = min(1024, seq)
    tm_gmlp = min(1024, seq)
    tq = min(512, seq)
    wkv_rows = min(512, seq)
    for rows in (tm_in, tm_ffn, tm_gmlp, tq, wkv_rows):
        assert seq % rows == 0 and rows % (2 * WKV_CHUNK) == 0, (seq, rows)
    for i in range(DEPTH):
        j = i // 2
        if i % 2 == 0:
            (r, k, v, lw, kn, a, g, qd, kd, vd) = _ab_in(
                x2, ab_w_in[j].astype(BF16), ab_shift_mu[j], ab_w0[j], ab_w2[j], ab_a0[j],
                ab_a2[j].astype(BF16), ab_g2[j].astype(BF16), ab_k_k[j], ab_k_a[j], rc, rs1, rs2,
                seq=seq, tm=tm_in)
            y_r = _wkv(r, k, v, lw, kn, a, g, ab_r_k[j], ab_lnx_g[j], ab_lnx_b[j],
                       batch=batch, seq=seq, rows=wkv_rows)
            lam_init = 0.8 - 0.6 * math.exp(-0.3 * i)
            y_d = _dattn(qd, kd, vd, ab_lam_q1[j], ab_lam_k1[j], ab_lam_q2[j], ab_lam_k2[j],
                         ab_subln_g[j], batch=batch, seq=seq, tq=tq, lam_init=lam_init)
            x2 = _proj_ln(y_r, y_d, ab_w_out[j].astype(BF16), x2, ln1_g[i], ln1_b[i], tm=tm_in)
        else:
            x2 = _gmlp(x2, c_w_in[j].astype(BF16), c_b_in[j], c_ln_g[j], c_ln_b[j], c_w_s[j],
                       c_b_s[j], c_w_out[j].astype(BF16), ln1_g[i], ln1_b[i], tm=tm_gmlp)
        x2 = _ffn(x2, ffn_w_up[i], ffn_conv_w[i], ffn_conv_b[i], ffn_w_down[i], ln2_g[i], ln2_b[i],
                  seq=seq, tm=tm_ffn, tf=256)
    return x2.reshape(batch, seq, D_MODEL)
```

```python
import functools
import math

import jax
import jax.numpy as jnp
from jax import lax
from jax.experimental import pallas as pl
from jax.experimental.pallas import tpu as pltpu

F32 = jnp.float32
BF16 = jnp.bfloat16
HIGHEST = lax.Precision.HIGHEST

D_MODEL = 1024
HEAD_DIM = 64
N_RWKV_HEADS = 8
D_RWKV = N_RWKV_HEADS * HEAD_DIM
N_DIFF_HEADS = 4
D_DIFF = N_DIFF_HEADS * 2 * HEAD_DIM
DECAY_LORA = 64
AAA_LORA = 64
GATE_LORA = 128
RWKV_COLS = 3 * D_RWKV + DECAY_LORA + AAA_LORA + GATE_LORA
AB_COLS = RWKV_COLS + 3 * D_DIFF
RWKV_GN_EPS = 64e-5
ROPE_THETA = 500000.0
ROPE_DIM = HEAD_DIM // 4
CHUNK = 128
GMLP_GROUPS = 8
D_GMLP = D_MODEL
D_FF = 2816
DEPTH = 4
DEEPNORM_ALPHA = (2 * DEPTH) ** 0.25

LANES = 128
HALO_ROWS = 16
WKV_CHUNK = 64
WKV_INV_BASE = 16
VMEM_LIMIT = 56 * 1024 * 1024
LOG2E = math.log2(math.e)


def _cparams(sem):
    return pltpu.CompilerParams(dimension_semantics=sem, vmem_limit_bytes=VMEM_LIMIT)


def _dot(a, b):
    return jnp.dot(a.astype(BF16), b.astype(BF16), preferred_element_type=F32)


def _dot_nt(a, b):
    return lax.dot_general(a.astype(BF16), b.astype(BF16), (((1,), (1,)), ((), ())),
                           preferred_element_type=F32)


def _layer_norm(z, g, b, eps=1e-5):
    mu = jnp.mean(z, axis=-1, keepdims=True)
    d = z - mu
    var = jnp.mean(d * d, axis=-1, keepdims=True)
    return d * lax.rsqrt(var + eps) * g + b


def _shift_rows(t, k, halo):
    rolled = pltpu.roll(t, k, 0)
    row = lax.broadcasted_iota(jnp.int32, t.shape, 0)
    out = rolled
    for j in range(k):
        out = jnp.where(row == j, halo[HALO_ROWS - k + j:HALO_ROWS - k + j + 1, :], out)
    return out


def _ab_in_kernel(x_ref, xh_ref, w_ref, mu_ref, w0_ref, w2_ref, a0_ref, a2_ref, g2_ref, kk_ref,
                  ka_ref, rc_ref, rs1_ref, rs2_ref,
                  r_out, k_out, v_out, lw_out, kn_out, a_out, g_out, q_out, kd_out, vd_out,
                  *, blocks_per_seq):
    seq_start = pl.program_id(0) % blocks_per_seq == 0
    tm = x_ref.shape[0]
    xe = jnp.concatenate([x_ref[...], xh_ref[...]], axis=0).astype(BF16)
    xb = xe[:tm]

    def proj(c0, c1):
        return jnp.dot(xb, w_ref[:, c0:c1], preferred_element_type=F32)

    def proj_shifted(c0, c1):
        pe = jnp.dot(xe, w_ref[:, c0:c1], preferred_element_type=F32)
        p = pe[:tm]
        ph = jnp.where(seq_start, 0.0, pe[tm:])
        return p + mu_ref[:, c0:c1] * (_shift_rows(p, 1, ph) - p)

    o = 3 * D_RWKV
    pq = proj(RWKV_COLS, RWKV_COLS + D_DIFF)
    pk = proj(RWKV_COLS + D_DIFF, RWKV_COLS + 2 * D_DIFF)
    xl = proj_shifted(o, RWKV_COLS)
    k = proj_shifted(D_RWKV, 2 * D_RWKV)
    r = proj_shifted(0, D_RWKV)
    v = proj_shifted(2 * D_RWKV, o)
    vd_out[...] = proj(RWKV_COLS + 2 * D_DIFF, AB_COLS).astype(BF16)
    xw = xl[:, :DECAY_LORA]
    xa = xl[:, DECAY_LORA:DECAY_LORA + AAA_LORA]
    xg = xl[:, DECAY_LORA + AAA_LORA:]

    rc = jnp.concatenate([rc_ref[...]] * (D_DIFF // LANES), axis=1)
    rs1 = jnp.concatenate([rs1_ref[...]] * (D_DIFF // LANES), axis=1)
    rs2 = jnp.concatenate([rs2_ref[...]] * (D_DIFF // LANES), axis=1)
    half = ROPE_DIM // 2

    def rope(t):
        return t * rc + pltpu.roll(t, half, 1) * rs1 + pltpu.roll(t, D_DIFF - half, 1) * rs2

    q_out[...] = (rope(pq) * (HEAD_DIM ** -0.5 * LOG2E)).astype(BF16)
    kd_out[...] = rope(pk).astype(BF16)

    z = w0_ref[...] + jnp.dot(jnp.tanh(xw), w2_ref[...], precision=HIGHEST,
                              preferred_element_type=F32)
    softplus_neg = jnp.maximum(-z, 0.0) + jnp.log1p(jnp.exp(-jnp.abs(z)))
    w = -softplus_neg - 0.5
    lw_out[...] = -jnp.exp(w)
    a = jax.nn.sigmoid(a0_ref[...] + _dot(xa, a2_ref[...]))
    a_out[...] = a
    g_out[...] = _dot(jax.nn.sigmoid(xg), g2_ref[...])

    kx = k * kk_ref[...]
    sq = kx * kx
    lane = lax.broadcasted_iota(jnp.int32, (sq.shape[0], LANES), 1)
    lo = lane < HEAD_DIM
    for c in range(D_RWKV // LANES):
        blk = sq[:, c * LANES:(c + 1) * LANES]
        n_lo = jnp.sqrt(jnp.sum(jnp.where(lo, blk, 0.0), axis=-1, keepdims=True))
        n_hi = jnp.sqrt(jnp.sum(jnp.where(lo, 0.0, blk), axis=-1, keepdims=True))
        norm = jnp.maximum(jnp.where(lo, n_lo, n_hi), 1e-12)
        kn_out[:, c * LANES:(c + 1) * LANES] = kx[:, c * LANES:(c + 1) * LANES] / norm
    r_out[...] = r
    k_out[...] = k * (1.0 + (a - 1.0) * ka_ref[...])
    v_out[...] = v


def _ab_in(x2, w_in, mu, w0, w2, a0, a2, g2, k_k, k_a, rc, rs1, rs2, *, seq, tm):
    T = x2.shape[0]
    n = T // tm
    bps = seq // tm
    row = lambda i: (i, 0)
    full = lambda i: (0, 0)
    halo = lambda i: (jnp.maximum(i * (tm // HALO_ROWS) - 1, 0), 0)
    rope_map = lambda i: (i % bps, 0)
    vec = lambda a: a.reshape(1, -1)
    f32_out = jax.ShapeDtypeStruct((T, D_RWKV), F32)
    bf_out = jax.ShapeDtypeStruct((T, D_DIFF), BF16)
    out_spec = pl.BlockSpec((tm, D_RWKV), row)
    return pl.pallas_call(
        functools.partial(_ab_in_kernel, blocks_per_seq=bps),
        grid=(n,),
        in_specs=[
            pl.BlockSpec((tm, D_MODEL), row),
            pl.BlockSpec((HALO_ROWS, D_MODEL), halo),
            pl.BlockSpec((D_MODEL, AB_COLS), full),
            pl.BlockSpec((1, RWKV_COLS), full),
            pl.BlockSpec((1, D_RWKV), full),
            pl.BlockSpec((DECAY_LORA, D_RWKV), full),
            pl.BlockSpec((1, D_RWKV), full),
            pl.BlockSpec((AAA_LORA, D_RWKV), full),
            pl.BlockSpec((GATE_LORA, D_RWKV), full),
            pl.BlockSpec((1, D_RWKV), full),
            pl.BlockSpec((1, D_RWKV), full),
            pl.BlockSpec((tm, LANES), rope_map),
            pl.BlockSpec((tm, LANES), rope_map),
            pl.BlockSpec((tm, LANES), rope_map),
        ],
        out_specs=[out_spec] * 10,
        out_shape=[f32_out] * 7 + [bf_out] * 3,
        compiler_params=_cparams(("parallel",)),
        name="ab_in",
    )(x2, x2, w_in, vec(mu), vec(w0), w2, vec(a0), a2, g2, vec(k_k), vec(k_a), rc, rs1, rs2)


WKV_GROUP = 4
GW = WKV_GROUP * HEAD_DIM


def _wkv_masks():
    c = WKV_CHUNK
    row = lax.broadcasted_iota(jnp.int32, (c, GW), 0)
    col = lax.broadcasted_iota(jnp.int32, (c, GW), 1) % HEAD_DIM
    brow = lax.broadcasted_iota(jnp.int32, (GW, GW), 0) // HEAD_DIM
    bcol = lax.broadcasted_iota(jnp.int32, (GW, GW), 1) // HEAD_DIM
    bd = jnp.where(brow == bcol, 1.0, 0.0).astype(BF16)
    brow2 = lax.broadcasted_iota(jnp.int32, (2 * c * WKV_GROUP, GW), 0) // (2 * c)
    bcol2 = lax.broadcasted_iota(jnp.int32, (2 * c * WKV_GROUP, GW), 1) // HEAD_DIM
    bd2 = jnp.where(brow2 == bcol2, 1.0, 0.0).astype(BF16)
    strict_incl = jnp.concatenate([row > col, row >= col], axis=0)
    return dict(row=row, col=col, bd=bd, bd2=bd2, strict_incl=strict_incl)


def _chunk_cumsum(lw, n_chunks):
    n = 2 * WKV_CHUNK
    r = lax.broadcasted_iota(jnp.int32, (n, n), 0)
    c = lax.broadcasted_iota(jnp.int32, (n, n), 1)
    tri = jnp.where(jnp.logical_and(r // WKV_CHUNK == c // WKV_CHUNK, r >= c), 1.0, 0.0).astype(BF16)
    h1 = lw.astype(BF16)
    rem = lw - h1.astype(F32)
    h2 = rem.astype(BF16)
    h3 = (rem - h2.astype(F32)).astype(BF16)
    out = []
    for b in range(n_chunks // 2):
        rows = slice(b * n, (b + 1) * n)
        cum = jnp.dot(tri, h1[rows], preferred_element_type=F32)
        cum = cum + jnp.dot(tri, h2[rows], preferred_element_type=F32)
        out.append(cum + jnp.dot(tri, h3[rows], preferred_element_type=F32))
    return jnp.concatenate(out, axis=0)


def _bd(w, m):
    return jnp.concatenate([w.astype(BF16)] * WKV_GROUP, axis=0) * m["bd"]


def _mm(x, w_bd):
    return jnp.dot(x.astype(BF16), w_bd, preferred_element_type=F32)


def _mm_nt(x, w_bd):
    return lax.dot_general(x.astype(BF16), w_bd, (((1,), (1,)), ((), ())),
                           preferred_element_type=F32)


def _seg_sum(x, m):
    hi = x.astype(BF16)
    lo = (x - hi.astype(F32)).astype(BF16)
    s = jnp.dot(jnp.concatenate([hi, lo], axis=0), m["bd"], preferred_element_type=F32)
    return s[:x.shape[0]] + s[x.shape[0]:]


def _run_interleaved(*gens):
    gens = list(gens)
    while gens:
        for g in list(gens):
            try:
                next(g)
            except StopIteration:
                gens.remove(g)


def _unit_lower_inverse(n_mats, m):
    row, col = m["row"], m["col"]
    eye = jnp.where(row == col, 1.0, 0.0)
    same_base = (row // WKV_INV_BASE) == (col // WKV_INV_BASE)
    c = WKV_CHUNK
    pws = [jnp.where(same_base, n, 0.0) for n in n_mats]
    ts = [eye + p for p in pws]
    pws = [_mm(p, _bd(p, m)) for p in pws]
    yield
    n_levels = WKV_INV_BASE.bit_length() - 2
    for lvl in range(n_levels):
        pw_bds = [_bd(p, m) for p in pws]
        if lvl < n_levels - 1:
            both = [_mm(jnp.concatenate([t, p], axis=0), b) for t, p, b in zip(ts, pws, pw_bds)]
            ts = [t + bo[:c] for t, bo in zip(ts, both)]
            pws = [bo[c:] for bo in both]
        else:
            ts = [t + _mm(t, b) for t, b in zip(ts, pw_bds)]
        yield
    blk = WKV_INV_BASE
    while blk < WKV_CHUNK:
        same_lo = (row // blk) == (col // blk)
        same_hi = (row // (2 * blk)) == (col // (2 * blk))
        sel = jnp.logical_and(same_hi, jnp.logical_not(same_lo))
        halves = [_mm(t, _bd(jnp.where(sel, n, 0.0), m)) for t, n in zip(ts, n_mats)]
        yield
        ts = [t + _mm(h, _bd(t, m)) for t, h in zip(ts, halves)]
        yield
        blk *= 2
    return ts


def _wkv_products(probs, m, res):
    c = WKV_CHUNK
    ts = yield from _unit_lower_inverse([p["a_ab"] for p in probs], m)
    vg_bds = [_bd(p["vg"], m) for p in probs]
    akv_yvs = [_mm(p["ak"], vb) for p, vb in zip(probs, vg_bds)]
    yield
    t_bfs = [t.astype(BF16) for t in ts]
    xas = [_mm(t, _bd(p["a0"], m)) for t, p in zip(t_bfs, probs)]
    yield
    res["xvs"] = [_mm(t, _bd(ay[:c], m)) for t, ay in zip(t_bfs, akv_yvs)]
    res["xrs"] = [jnp.concatenate([xa, p["r0"]], axis=0).astype(BF16) for xa, p in zip(xas, probs)]
    res["yvs"] = [ay[c:] for ay in akv_yvs]
    yield


def _wkv_recurrence(probs, res, decays, states, ys, m):
    c = WKV_CHUNK
    n_groups = len(states)
    for c0 in range(0, len(probs), n_groups):
        us = []
        for gi in range(n_groups):
            i = c0 + gi
            uy = _mm(res["xrs"][i], _bd(states[gi], m))
            us.append((uy[:c] + res["xvs"][i], uy[c:]))
        yield
        for gi in range(n_groups):
            i = c0 + gi
            p = probs[i]
            u, y0 = us[gi]
            ys.append(y0 + _mm(p["a_rb"], _bd(u, m)) + res["yvs"][i])
            uv = jnp.concatenate([u, p["vg"]], axis=0).astype(BF16)
            uv_bd = jnp.concatenate([uv] * WKV_GROUP, axis=0) * m["bd2"]
            states[gi] = states[gi] * decays[c0 + gi] + jnp.dot(
                p["bk_t"], uv_bd, preferred_element_type=F32)
        yield


def _wkv_prepare(r, k, v, lw, cum, kn, a, m):
    c = r.shape[0]
    mid = cum[c // 2 - 1:c // 2, :]
    end = cum[c - 1:c, :]
    e_abs = jnp.exp(cum)
    e_abs_prev = jnp.exp(cum - lw)
    e_mid = jnp.exp(-mid)
    e_neg = jnp.exp(mid - cum)
    e_end = jnp.exp(end - cum)
    g_end = jnp.exp(end)
    bvec = kn * a
    r0 = r * e_abs
    a0 = -kn * e_abs_prev
    rt = r0 * e_mid
    at = a0 * e_mid
    kt = k * e_neg
    bt = bvec * e_neg
    kh = k * e_end
    bh = bvec * e_end
    out = []
    for gi in range(N_RWKV_HEADS // WKV_GROUP):
        sl = slice(gi * GW, (gi + 1) * GW)
        lhs = jnp.concatenate([at[:, sl], rt[:, sl]], axis=0)
        ab = jnp.where(m["strict_incl"], _mm_nt(lhs, _bd(bt[:, sl], m)), 0.0)
        ak = jnp.where(m["strict_incl"], _mm_nt(lhs, _bd(kt[:, sl], m)), 0.0)
        bk_t = jnp.transpose(jnp.concatenate([bh[:, sl], kh[:, sl]], axis=0))
        bk_t = jnp.concatenate([bk_t[hh * HEAD_DIM:(hh + 1) * HEAD_DIM] for hh in range(WKV_GROUP)],
                               axis=1).astype(BF16)
        g_diag = jnp.where(m["row"] == m["col"], g_end[:, sl], 0.0)
        out.append(dict(a_ab=ab[:c], a_rb=ab[c:], ak=ak, vg=v[:, sl], a0=a0[:, sl], r0=r0[:, sl],
                        g_diag=g_diag, bk_t=bk_t))
    return out


def _wkv_kernel(r_ref, k_ref, v_ref, lw_ref, kn_ref, a_ref, g_ref, rk_ref, lg_ref, lb_ref,
                o_ref, s_scr, *, n_chunks):
    @pl.when(pl.program_id(1) == 0)
    def _():
        s_scr[...] = jnp.zeros_like(s_scr)

    n_groups = N_RWKV_HEADS // WKV_GROUP
    m = _wkv_masks()
    chunk_rows = [slice(ci * WKV_CHUNK, (ci + 1) * WKV_CHUNK) for ci in range(n_chunks)]

    c = WKV_CHUNK
    lw_all = lw_ref[...]
    cum_all = _chunk_cumsum(lw_all, n_chunks)
    probs = []
    for rows in chunk_rows:
        probs += _wkv_prepare(r_ref[rows, :], k_ref[rows, :], v_ref[rows, :], lw_all[rows, :],
                              cum_all[rows, :], kn_ref[rows, :], a_ref[rows, :], m)
    n_probs = len(probs)
    split = lambda stacked: [stacked[i * c:(i + 1) * c] for i in range(n_probs)]
    decays = split(_seg_sum(jnp.concatenate([p["g_diag"] for p in probs], axis=0), m))
    rkr = r_ref[...] * k_ref[...] * rk_ref[...]
    bonus_w = split(_seg_sum(jnp.concatenate(
        [rkr[rows, gi * GW:(gi + 1) * GW] for rows in chunk_rows for gi in range(n_groups)], axis=0), m))

    h = (n_chunks // 2) * n_groups
    res_a, res_b = {}, {}
    states = [s_scr[gi] for gi in range(n_groups)]
    ys = []
    _run_interleaved(_wkv_products(probs[:h], m, res_a))
    _run_interleaved(_wkv_recurrence(probs[:h], res_a, decays[:h], states, ys, m),
                     _wkv_products(probs[h:], m, res_b))
    _run_interleaved(_wkv_recurrence(probs[h:], res_b, decays[h:], states, ys, m))
    for gi in range(n_groups):
        s_scr[gi] = states[gi]

    y_all = jnp.concatenate(ys, axis=0)
    d_all = y_all - _seg_sum(y_all, m) * (1.0 / HEAD_DIM)
    inv_all = lax.rsqrt(_seg_sum(d_all * d_all, m) * (1.0 / HEAD_DIM) + RWKV_GN_EPS)
    for ci, rows in enumerate(chunk_rows):
        for gi in range(n_groups):
            i = ci * n_groups + gi
            sl = slice(gi * GW, (gi + 1) * GW)
            yn = d_all[i * c:(i + 1) * c] * inv_all[i * c:(i + 1) * c] * lg_ref[:, sl] + lb_ref[:, sl]
            o_ref[rows, sl] = ((yn + bonus_w[i] * probs[i]["vg"]) * g_ref[rows, sl]).astype(BF16)


def _wkv(r, k, v, lw, kn, a, g, r_k, lnx_g, lnx_b, *, batch, seq, rows):
    T = r.shape[0]
    nb = seq // rows
    blk = pl.BlockSpec((rows, D_RWKV), lambda b, i: (b * nb + i, 0))
    par = pl.BlockSpec((1, D_RWKV), lambda b, i: (0, 0))
    return pl.pallas_call(
        functools.partial(_wkv_kernel, n_chunks=rows // WKV_CHUNK),
        grid=(batch, nb),
        in_specs=[blk] * 7 + [par] * 3,
        out_specs=blk,
        out_shape=jax.ShapeDtypeStruct((T, D_RWKV), BF16),
        scratch_shapes=[pltpu.VMEM((N_RWKV_HEADS // WKV_GROUP, HEAD_DIM, GW), F32)],
        compiler_params=_cparams(("parallel", "arbitrary")),
        name="wkv7",
    )(r, k, v, lw, kn, a, g, r_k.reshape(1, -1), lnx_g.reshape(1, -1), lnx_b.reshape(1, -1))


def _dattn_kernel(q_ref, k_ref, v_ref, lq1_ref, lk1_ref, lq2_ref, lk2_ref, sg_ref, o_ref,
                  qs_scr, m_scr, l_scr, acc_scr, s0_scr, s1_scr, p0_scr, p1_scr, al0_scr, al1_scr,
                  *, tq, lam_init):
    qi = pl.program_id(2)
    q = q_ref[...]
    lane = lax.broadcasted_iota(jnp.int32, q.shape, 1)
    zero = jnp.zeros_like(q)
    qs_scr[:tq, :] = jnp.where(lane < HEAD_DIM, q, zero)
    qs_scr[tq:, :] = jnp.where(lane < HEAD_DIM, zero, q)
    m_scr[...] = jnp.full_like(m_scr, -jnp.inf)
    l_scr[...] = jnp.zeros_like(l_scr)
    acc_scr[...] = jnp.zeros_like(acc_scr)

    bufs = ((s0_scr, p0_scr, al0_scr), (s1_scr, p1_scr, al1_scr))

    def scores(j, par):
        start = pl.multiple_of(j * tq, tq)
        bufs[par][0][...] = _dot_nt(qs_scr[...], k_ref[pl.ds(start, tq), :])

    def softmax(par, masked):
        s_ref, p_ref, al_ref = bufs[par]
        s = s_ref[...]
        if masked:
            r_pos = lax.broadcasted_iota(jnp.int32, s.shape, 0) % tq
            c_pos = lax.broadcasted_iota(jnp.int32, s.shape, 1)
            s = jnp.where(c_pos <= r_pos, s, -jnp.inf)
        m_prev = m_scr[...]
        m_new = jnp.maximum(m_prev, jnp.max(s, axis=-1, keepdims=True))
        alpha = jnp.exp2(m_prev - m_new)
        p = jnp.exp2(s - jnp.concatenate([m_new] * (tq // LANES), axis=1))
        l_scr[...] = alpha * l_scr[...] + jnp.sum(p, axis=-1, keepdims=True)
        m_scr[...] = m_new
        p_ref[...] = p.astype(BF16)
        al_ref[...] = alpha

    def values(j, par):
        _, p_ref, al_ref = bufs[par]
        start = pl.multiple_of(j * tq, tq)
        acc_scr[...] = al_ref[...] * acc_scr[...] + jnp.dot(
            p_ref[...], v_ref[pl.ds(start, tq), :], preferred_element_type=F32)

    @pl.when(qi == 0)
    def _():
        scores(0, 0)
        softmax(0, True)
        values(0, 0)

    @pl.when(qi > 0)
    def _():
        scores(0, 0)
        scores(1, 1)
        softmax(0, False)

        def pair(t):
            values(t - 2, 0)
            scores(t, 0)
            softmax(1, False)
            values(t - 1, 1)
            scores(t + 1, 1)
            softmax(0, False)

        n_pairs = (qi - 1) // 2

        def body(i, carry):
            pair(2 + 4 * i)
            pair(4 + 4 * i)
            return carry

        lax.fori_loop(0, n_pairs // 2, body, 0)

        @pl.when(n_pairs % 2 == 1)
        def _():
            pair(2 * n_pairs)

        @pl.when(qi % 2 == 1)
        def _():
            values(qi - 1, 0)
            softmax(1, True)
            values(qi, 1)

        @pl.when(qi % 2 == 0)
        def _():
            values(qi - 2, 0)
            scores(qi, 0)
            softmax(1, False)
            values(qi - 1, 1)
            softmax(0, True)
            values(qi, 0)

    lam = (jnp.exp(jnp.sum(lq1_ref[...] * lk1_ref[...], axis=-1, keepdims=True))
           - jnp.exp(jnp.sum(lq2_ref[...] * lk2_ref[...], axis=-1, keepdims=True)) + lam_init)
    on = acc_scr[...] / l_scr[...]
    o = on[:tq, :] - lam * on[tq:, :]
    o = o * lax.rsqrt(jnp.mean(o * o, axis=-1, keepdims=True) + 1e-5) * sg_ref[...]
    o_ref[...] = (o * (1.0 - lam_init)).astype(BF16)


def _dattn(q, k, v, lq1, lk1, lq2, lk2, subln_g, *, batch, seq, tq, lam_init):
    T = q.shape[0]
    nq = seq // tq
    hd = 2 * HEAD_DIM
    qmap = lambda b, h, i: (b * nq + i, h)
    kvmap = lambda b, h, i: (b, h)
    par = lambda b, h, i: (0, 0)
    vec = lambda a: a.reshape(1, -1)
    return pl.pallas_call(
        functools.partial(_dattn_kernel, tq=tq, lam_init=lam_init),
        grid=(batch, N_DIFF_HEADS, nq),
        in_specs=[
            pl.BlockSpec((tq, hd), qmap),
            pl.BlockSpec((seq, hd), kvmap),
            pl.BlockSpec((seq, hd), kvmap),
            pl.BlockSpec((1, HEAD_DIM), par),
            pl.BlockSpec((1, HEAD_DIM), par),
            pl.BlockSpec((1, HEAD_DIM), par),
            pl.BlockSpec((1, HEAD_DIM), par),
            pl.BlockSpec((1, hd), par),
        ],
        out_specs=pl.BlockSpec((tq, hd), qmap),
        out_shape=jax.ShapeDtypeStruct((T, D_DIFF), BF16),
        scratch_shapes=[
            pltpu.VMEM((2 * tq, hd), BF16),
            pltpu.VMEM((2 * tq, LANES), F32),
            pltpu.VMEM((2 * tq, LANES), F32),
            pltpu.VMEM((2 * tq, hd), F32),
            pltpu.VMEM((2 * tq, tq), F32),
            pltpu.VMEM((2 * tq, tq), F32),
            pltpu.VMEM((2 * tq, tq), BF16),
            pltpu.VMEM((2 * tq, tq), BF16),
            pltpu.VMEM((2 * tq, LANES), F32),
            pltpu.VMEM((2 * tq, LANES), F32),
        ],
        compiler_params=_cparams(("parallel", "parallel", "arbitrary")),
        name="diff_attn",
    )(q, k, v, vec(lq1), vec(lk1), vec(lq2), vec(lk2), vec(subln_g))


def _proj_ln_kernel(yr_ref, yd_ref, w_ref, x_ref, g_ref, b_ref, o_ref):
    mix = jnp.dot(yr_ref[...], w_ref[:D_RWKV, :], preferred_element_type=F32)
    mix = mix + jnp.dot(yd_ref[...], w_ref[D_RWKV:, :], preferred_element_type=F32)
    o_ref[...] = _layer_norm(DEEPNORM_ALPHA * x_ref[...] + mix, g_ref[...], b_ref[...])


def _proj_ln(yr, yd, w_out, x2, g, b, *, tm):
    T = x2.shape[0]
    row = lambda i: (i, 0)
    full = lambda i: (0, 0)
    return pl.pallas_call(
        _proj_ln_kernel,
        grid=(T // tm,),
        in_specs=[
            pl.BlockSpec((tm, D_RWKV), row),
            pl.BlockSpec((tm, D_DIFF), row),
            pl.BlockSpec((D_RWKV + D_DIFF, D_MODEL), full),
            pl.BlockSpec((tm, D_MODEL), row),
            pl.BlockSpec((1, D_MODEL), full),
            pl.BlockSpec((1, D_MODEL), full),
        ],
        out_specs=pl.BlockSpec((tm, D_MODEL), row),
        out_shape=jax.ShapeDtypeStruct((T, D_MODEL), F32),
        compiler_params=_cparams(("parallel",)),
        name="ab_out_ln",
    )(yr, yd, w_out, x2, g.reshape(1, -1), b.reshape(1, -1))


def _gmlp_kernel(x_ref, win_ref, bin_ref, lng_ref, lnb_ref, ws_ref, bs_ref, wout_ref, g_ref, b_ref,
                 o_ref, gated_scr, *, tm):
    x = x_ref[...]
    xb = x.astype(BF16)

    def gelu(h):
        return 0.5 * h * (1.0 + lax.erf(h * (0.5 ** 0.5)))

    hv = jnp.dot(xb, win_ref[:, D_GMLP:], preferred_element_type=F32) + bin_ref[:, D_GMLP:]
    hu = jnp.dot(xb, win_ref[:, :D_GMLP], preferred_element_type=F32) + bin_ref[:, :D_GMLP]
    v = _layer_norm(gelu(hv), lng_ref[...], lnb_ref[...]).astype(BF16)
    u = gelu(hu)
    row = lax.broadcasted_iota(jnp.int32, (CHUNK, CHUNK), 0)
    col = lax.broadcasted_iota(jnp.int32, (CHUNK, CHUNK), 1)
    gw = D_GMLP // GMLP_GROUPS
    n_chunks = tm // CHUNK
    for gi in range(GMLP_GROUPS):
        cs = slice(gi * gw, (gi + 1) * gw)
        ws = jnp.where(row >= col, ws_ref[gi], 0.0).astype(BF16)
        v_side = jnp.concatenate([v[c * CHUNK:(c + 1) * CHUNK, cs] for c in range(n_chunks)], axis=1)
        mixed = jnp.dot(ws, v_side, preferred_element_type=F32)
        for c in range(n_chunks):
            rs = slice(c * CHUNK, (c + 1) * CHUNK)
            gated_scr[rs, cs] = (u[rs, cs] * (mixed[:, c * gw:(c + 1) * gw] + bs_ref[gi])).astype(BF16)
    mix = jnp.dot(gated_scr[...], wout_ref[...], preferred_element_type=F32)
    o_ref[...] = _layer_norm(DEEPNORM_ALPHA * x + mix, g_ref[...], b_ref[...])


def _gmlp(x2, w_in, b_in, ln_g, ln_b, w_s, b_s, w_out, g, b, *, tm):
    T = x2.shape[0]
    row = lambda i: (i, 0)
    full = lambda i: (0, 0)
    full3 = lambda i: (0, 0, 0)
    vec = lambda a: a.reshape(1, -1)
    gw = D_GMLP // GMLP_GROUPS
    bs_b = jnp.broadcast_to(b_s[:, :, None], (GMLP_GROUPS, CHUNK, gw))
    return pl.pallas_call(
        functools.partial(_gmlp_kernel, tm=tm),
        grid=(T // tm,),
        in_specs=[
            pl.BlockSpec((tm, D_MODEL), row),
            pl.BlockSpec((D_MODEL, 2 * D_GMLP), full, pipeline_mode=pl.Buffered(1)),
            pl.BlockSpec((1, 2 * D_GMLP), full),
            pl.BlockSpec((1, D_GMLP), full),
            pl.BlockSpec((1, D_GMLP), full),
            pl.BlockSpec((GMLP_GROUPS, CHUNK, CHUNK), full3),
            pl.BlockSpec((GMLP_GROUPS, CHUNK, gw), full3),
            pl.BlockSpec((D_GMLP, D_MODEL), full, pipeline_mode=pl.Buffered(1)),
            pl.BlockSpec((1, D_MODEL), full),
            pl.BlockSpec((1, D_MODEL), full),
        ],
        out_specs=pl.BlockSpec((tm, D_MODEL), row),
        out_shape=jax.ShapeDtypeStruct((T, D_MODEL), F32),
        scratch_shapes=[pltpu.VMEM((tm, D_GMLP), BF16)],
        compiler_params=_cparams(("parallel",)),
        name="gmlp",
    )(x2, w_in, vec(b_in), vec(ln_g), vec(ln_b), w_s, bs_b, w_out, vec(g), vec(b))


def _ffn_kernel(x_ref, xh_ref, wup_ref, cw_ref, cb_ref, wd_ref, g_ref, b_ref, o_ref,
                *, blocks_per_seq, tf, n_slabs):
    x = x_ref[...]
    tm = x.shape[0]
    xe = jnp.concatenate([x, xh_ref[...]], axis=0).astype(BF16)
    xb = xe[:tm]
    seq_start = pl.program_id(0) % blocks_per_seq == 0

    def up(j):
        cols = slice(j * tf, (j + 1) * tf)
        ge = jnp.dot(xe, wup_ref[:, cols], preferred_element_type=F32)
        val = jnp.dot(xb, wup_ref[:, D_FF + j * tf:D_FF + (j + 1) * tf], preferred_element_type=F32)
        return ge[:tm], val, jnp.where(seq_start, 0.0, ge[tm:])

    acc = None
    nxt = up(0)
    for j in range(n_slabs):
        gate, val, gh = nxt
        if j + 1 < n_slabs:
            nxt = up(j + 1)
        cols = slice(j * tf, (j + 1) * tf)
        cw = cw_ref[:, cols]
        conv = (cb_ref[:, cols] + cw[0:1, :] * _shift_rows(gate, 2, gh)
                + cw[1:2, :] * _shift_rows(gate, 1, gh) + cw[2:3, :] * gate)
        hid = conv * jax.nn.sigmoid(conv) * val
        down = jnp.dot(hid.astype(BF16), wd_ref[j], preferred_element_type=F32)
        acc = down if acc is None else acc + down
    o_ref[...] = _layer_norm(DEEPNORM_ALPHA * x + acc, g_ref[...], b_ref[...])


def _ffn(x2, w_up, conv_w, conv_b, w_down, g, b, *, seq, tm, tf):
    T = x2.shape[0]
    n_slabs = D_FF // tf
    bps = seq // tm
    wup = w_up.astype(BF16)
    cb = conv_b.reshape(1, D_FF)
    wd = w_down.reshape(n_slabs, tf, D_MODEL).astype(BF16)
    row = lambda i: (i, 0)
    full = lambda i: (0, 0)
    full3 = lambda i: (0, 0, 0)
    halo = lambda i: (jnp.maximum(i * (tm // HALO_ROWS) - 1, 0), 0)
    once = pl.Buffered(1)
    return pl.pallas_call(
        functools.partial(_ffn_kernel, blocks_per_seq=bps, tf=tf, n_slabs=n_slabs),
        grid=(T // tm,),
        in_specs=[
            pl.BlockSpec((tm, D_MODEL), row),
            pl.BlockSpec((HALO_ROWS, D_MODEL), halo),
            pl.BlockSpec((D_MODEL, 2 * D_FF), full, pipeline_mode=once),
            pl.BlockSpec((3, D_FF), full, pipeline_mode=once),
            pl.BlockSpec((1, D_FF), full, pipeline_mode=once),
            pl.BlockSpec((n_slabs, tf, D_MODEL), full3, pipeline_mode=once),
            pl.BlockSpec((1, D_MODEL), full),
            pl.BlockSpec((1, D_MODEL), full),
        ],
        out_specs=pl.BlockSpec((tm, D_MODEL), row),
        out_shape=jax.ShapeDtypeStruct((T, D_MODEL), F32),
        compiler_params=_cparams(("parallel",)),
        name="conv_ffn",
    )(x2, x2, wup, conv_w, cb, wd, g.reshape(1, -1), b.reshape(1, -1))


def _rope_tables(seq):
    half = ROPE_DIM // 2
    inv_freq = ROPE_THETA ** (-jnp.arange(0, ROPE_DIM, 2, dtype=F32) / ROPE_DIM)
    ang = jnp.arange(seq, dtype=F32)[:, None] * inv_freq[None, :]
    cos, sin = jnp.cos(ang), jnp.sin(ang)
    pos = jnp.arange(LANES) % HEAD_DIM
    spread = (jnp.arange(half)[:, None] == (pos % half)[None, :]).astype(F32)
    cos_l = jnp.dot(cos, spread, precision=HIGHEST)
    sin_l = jnp.dot(sin, spread, precision=HIGHEST)
    c = jnp.where(pos < ROPE_DIM, cos_l, 1.0)
    s1 = jnp.where((pos >= half) & (pos < ROPE_DIM), sin_l, 0.0)
    s2 = jnp.where(pos < half, -sin_l, 0.0)
    return c, s1, s2


def kernel(x, ab_w_in, ab_shift_mu, ab_w0, ab_w2, ab_a0, ab_a2, ab_g2, ab_k_k, ab_k_a, ab_r_k, ab_lnx_g, ab_lnx_b, ab_lam_q1, ab_lam_k1, ab_lam_q2, ab_lam_k2, ab_subln_g, ab_w_out, c_w_in, c_b_in, c_ln_g, c_ln_b, c_w_s, c_b_s, c_w_out, ln1_g, ln1_b, ffn_w_up, ffn_conv_w, ffn_conv_b, ffn_w_down, ln2_g, ln2_b):
    batch, seq, d_model = x.shape
    assert d_model == D_MODEL and x.dtype == F32
    x2 = x.reshape(batch * seq, D_MODEL)
    rc, rs1, rs2 = _rope_tables(seq)
    tm_in = min(512, seq)
    tm_ffn = min(1024, seq)
    tm_gmlp = min(1024, seq)
    tq = min(512, seq)
    wkv_rows = min(512, seq)
    for rows in (tm_in, tm_ffn, tm_gmlp, tq, wkv_rows):
        assert seq % rows == 0 and rows % (2 * WKV_CHUNK) == 0, (seq, rows)
    for i in range(DEPTH):
        j = i // 2
        if i % 2 == 0:
            (r, k, v, lw, kn, a, g, qd, kd, vd) = _ab_in(
                x2, ab_w_in[j].astype(BF16), ab_shift_mu[j], ab_w0[j], ab_w2[j], ab_a0[j],
                ab_a2[j].astype(BF16), ab_g2[j].astype(BF16), ab_k_k[j], ab_k_a[j], rc, rs1, rs2,
                seq=seq, tm=tm_in)
            y_r = _wkv(r, k, v, lw, kn, a, g, ab_r_k[j], ab_lnx_g[j], ab_lnx_b[j],
                       batch=batch, seq=seq, rows=wkv_rows)
            lam_init = 0.8 - 0.6 * math.exp(-0.3 * i)
            y_d = _dattn(qd, kd, vd, ab_lam_q1[j], ab_lam_k1[j], ab_lam_q2[j], ab_lam_k2[j],
                         ab_subln_g[j], batch=batch, seq=seq, tq=tq, lam_init=lam_init)
            x2 = _proj_ln(y_r, y_d, ab_w_out[j].astype(BF16), x2, ln1_g[i], ln1_b[i], tm=tm_in)
        else:
            x2 = _gmlp(x2, c_w_in[j].astype(BF16), c_b_in[j], c_ln_g[j], c_ln_b[j], c_w_s[j],
                       c_b_s[j], c_w_out[j].astype(BF16), ln1_g[i], ln1_b[i], tm=tm_gmlp)
        x2 = _ffn(x2, ffn_w_up[i], ffn_conv_w[i], ffn_conv_b[i], ffn_w_down[i], ln2_g[i], ln2_b[i],
                  seq=seq, tm=tm_ffn, tf=256)
    return x2.reshape(batch, seq, D_MODEL)
```

```python
import functools
import math

import jax
import jax.numpy as jnp
from jax import lax
from jax.experimental import pallas as pl
from jax.experimental.pallas import tpu as pltpu

F32 = jnp.float32
BF16 = jnp.bfloat16
HIGHEST = lax.Precision.HIGHEST

D_MODEL = 1024
HEAD_DIM = 64
N_RWKV_HEADS = 8
D_RWKV = N_RWKV_HEADS * HEAD_DIM
N_DIFF_HEADS = 4
D_DIFF = N_DIFF_HEADS * 2 * HEAD_DIM
DECAY_LORA = 64
AAA_LORA = 64
GATE_LORA = 128
RWKV_COLS = 3 * D_RWKV + DECAY_LORA + AAA_LORA + GATE_LORA
AB_COLS = RWKV_COLS + 3 * D_DIFF
RWKV_GN_EPS = 64e-5
ROPE_THETA = 500000.0
ROPE_DIM = HEAD_DIM // 4
CHUNK = 128
GMLP_GROUPS = 8
D_GMLP = D_MODEL
D_FF = 2816
DEPTH = 4
DEEPNORM_ALPHA = (2 * DEPTH) ** 0.25

LANES = 128
HALO_ROWS = 16
WKV_CHUNK = 64
WKV_INV_BASE = 16
VMEM_LIMIT = 56 * 1024 * 1024
LOG2E = math.log2(math.e)


def _cparams(sem):
    return pltpu.CompilerParams(dimension_semantics=sem, vmem_limit_bytes=VMEM_LIMIT)


def _dot(a, b):
    return jnp.dot(a.astype(BF16), b.astype(BF16), preferred_element_type=F32)


def _dot_nt(a, b):
    return lax.dot_general(a.astype(BF16), b.astype(BF16), (((1,), (1,)), ((), ())),
                           preferred_element_type=F32)


def _layer_norm(z, g, b, eps=1e-5):
    mu = jnp.mean(z, axis=-1, keepdims=True)
    d = z - mu
    var = jnp.mean(d * d, axis=-1, keepdims=True)
    return d * lax.rsqrt(var + eps) * g + b


def _shift_rows(t, k, halo):
    rolled = pltpu.roll(t, k, 0)
    row = lax.broadcasted_iota(jnp.int32, t.shape, 0)
    out = rolled
    for j in range(k):
        out = jnp.where(row == j, halo[HALO_ROWS - k + j:HALO_ROWS - k + j + 1, :], out)
    return out


def _ab_in_kernel(x_ref, xh_ref, w_ref, mu_ref, w0_ref, w2_ref, a0_ref, a2_ref, g2_ref, kk_ref,
                  ka_ref, rc_ref, rs1_ref, rs2_ref,
                  r_out, k_out, v_out, lw_out, kn_out, a_out, g_out, q_out, kd_out, vd_out,
                  *, blocks_per_seq):
    seq_start = pl.program_id(0) % blocks_per_seq == 0
    tm = x_ref.shape[0]
    xe = jnp.concatenate([x_ref[...], xh_ref[...]], axis=0).astype(BF16)
    xb = xe[:tm]

    def proj(c0, c1):
        return jnp.dot(xb, w_ref[:, c0:c1], preferred_element_type=F32)

    def proj_shifted(c0, c1):
        pe = jnp.dot(xe, w_ref[:, c0:c1], preferred_element_type=F32)
        p = pe[:tm]
        ph = jnp.where(seq_start, 0.0, pe[tm:])
        return p + mu_ref[:, c0:c1] * (_shift_rows(p, 1, ph) - p)

    o = 3 * D_RWKV
    pq = proj(RWKV_COLS, RWKV_COLS + D_DIFF)
    pk = proj(RWKV_COLS + D_DIFF, RWKV_COLS + 2 * D_DIFF)
    xl = proj_shifted(o, RWKV_COLS)
    k = proj_shifted(D_RWKV, 2 * D_RWKV)
    r = proj_shifted(0, D_RWKV)
    v = proj_shifted(2 * D_RWKV, o)
    vd_out[...] = proj(RWKV_COLS + 2 * D_DIFF, AB_COLS).astype(BF16)
    xw = xl[:, :DECAY_LORA]
    xa = xl[:, DECAY_LORA:DECAY_LORA + AAA_LORA]
    xg = xl[:, DECAY_LORA + AAA_LORA:]

    rc = jnp.concatenate([rc_ref[...]] * (D_DIFF // LANES), axis=1)
    rs1 = jnp.concatenate([rs1_ref[...]] * (D_DIFF // LANES), axis=1)
    rs2 = jnp.concatenate([rs2_ref[...]] * (D_DIFF // LANES), axis=1)
    half = ROPE_DIM // 2

    def rope(t):
        return t * rc + pltpu.roll(t, half, 1) * rs1 + pltpu.roll(t, D_DIFF - half, 1) * rs2

    q_out[...] = (rope(pq) * (HEAD_DIM ** -0.5 * LOG2E)).astype(BF16)
    kd_out[...] = rope(pk).astype(BF16)

    z = w0_ref[...] + jnp.dot(jnp.tanh(xw), w2_ref[...], precision=HIGHEST,
                              preferred_element_type=F32)
    softplus_neg = jnp.maximum(-z, 0.0) + jnp.log1p(jnp.exp(-jnp.abs(z)))
    w = -softplus_neg - 0.5
    lw_out[...] = -jnp.exp(w)
    a = jax.nn.sigmoid(a0_ref[...] + _dot(xa, a2_ref[...]))
    a_out[...] = a
    g_out[...] = _dot(jax.nn.sigmoid(xg), g2_ref[...])

    kx = k * kk_ref[...]
    sq = kx * kx
    lane = lax.broadcasted_iota(jnp.int32, (sq.shape[0], LANES), 1)
    lo = lane < HEAD_DIM
    for c in range(D_RWKV // LANES):
        blk = sq[:, c * LANES:(c + 1) * LANES]
        n_lo = jnp.sqrt(jnp.sum(jnp.where(lo, blk, 0.0), axis=-1, keepdims=True))
        n_hi = jnp.sqrt(jnp.sum(jnp.where(lo, 0.0, blk), axis=-1, keepdims=True))
        norm = jnp.maximum(jnp.where(lo, n_lo, n_hi), 1e-12)
        kn_out[:, c * LANES:(c + 1) * LANES] = kx[:, c * LANES:(c + 1) * LANES] / norm
    r_out[...] = r
    k_out[...] = k * (1.0 + (a - 1.0) * ka_ref[...])
    v_out[...] = v


def _ab_in(x2, w_in, mu, w0, w2, a0, a2, g2, k_k, k_a, rc, rs1, rs2, *, seq, tm):
    T = x2.shape[0]
    n = T // tm
    bps = seq // tm
    row = lambda i: (i, 0)
    full = lambda i: (0, 0)
    halo = lambda i: (jnp.maximum(i * (tm // HALO_ROWS) - 1, 0), 0)
    rope_map = lambda i: (i % bps, 0)
    vec = lambda a: a.reshape(1, -1)
    f32_out = jax.ShapeDtypeStruct((T, D_RWKV), F32)
    bf_out = jax.ShapeDtypeStruct((T, D_DIFF), BF16)
    out_spec = pl.BlockSpec((tm, D_RWKV), row)
    return pl.pallas_call(
        functools.partial(_ab_in_kernel, blocks_per_seq=bps),
        grid=(n,),
        in_specs=[
            pl.BlockSpec((tm, D_MODEL), row),
            pl.BlockSpec((HALO_ROWS, D_MODEL), halo),
            pl.BlockSpec((D_MODEL, AB_COLS), full),
            pl.BlockSpec((1, RWKV_COLS), full),
            pl.BlockSpec((1, D_RWKV), full),
            pl.BlockSpec((DECAY_LORA, D_RWKV), full),
            pl.BlockSpec((1, D_RWKV), full),
            pl.BlockSpec((AAA_LORA, D_RWKV), full),
            pl.BlockSpec((GATE_LORA, D_RWKV), full),
            pl.BlockSpec((1, D_RWKV), full),
            pl.BlockSpec((1, D_RWKV), full),
            pl.BlockSpec((tm, LANES), rope_map),
            pl.BlockSpec((tm, LANES), rope_map),
            pl.BlockSpec((tm, LANES), rope_map),
        ],
        out_specs=[out_spec] * 10,
        out_shape=[f32_out] * 7 + [bf_out] * 3,
        compiler_params=_cparams(("parallel",)),
        name="ab_in",
    )(x2, x2, w_in, vec(mu), vec(w0), w2, vec(a0), a2, g2, vec(k_k), vec(k_a), rc, rs1, rs2)


WKV_GROUP = 4
GW = WKV_GROUP * HEAD_DIM


def _wkv_masks():
    c = WKV_CHUNK
    row = lax.broadcasted_iota(jnp.int32, (c, GW), 0)
    col = lax.broadcasted_iota(jnp.int32, (c, GW), 1) % HEAD_DIM
    brow = lax.broadcasted_iota(jnp.int32, (GW, GW), 0) // HEAD_DIM
    bcol = lax.broadcasted_iota(jnp.int32, (GW, GW), 1) // HEAD_DIM
    bd = jnp.where(brow == bcol, 1.0, 0.0).astype(BF16)
    brow2 = lax.broadcasted_iota(jnp.int32, (2 * c * WKV_GROUP, GW), 0) // (2 * c)
    bcol2 = lax.broadcasted_iota(jnp.int32, (2 * c * WKV_GROUP, GW), 1) // HEAD_DIM
    bd2 = jnp.where(brow2 == bcol2, 1.0, 0.0).astype(BF16)
    strict_incl = jnp.concatenate([row > col, row >= col], axis=0)
    return dict(row=row, col=col, bd=bd, bd2=bd2, strict_incl=strict_incl)


def _chunk_cumsum(lw, n_chunks):
    n = 2 * WKV_CHUNK
    r = lax.broadcasted_iota(jnp.int32, (n, n), 0)
    c = lax.broadcasted_iota(jnp.int32, (n, n), 1)
    tri = jnp.where(jnp.logical_and(r // WKV_CHUNK == c // WKV_CHUNK, r >= c), 1.0, 0.0).astype(BF16)
    h1 = lw.astype(BF16)
    rem = lw - h1.astype(F32)
    h2 = rem.astype(BF16)
    h3 = (rem - h2.astype(F32)).astype(BF16)
    out = []
    for b in range(n_chunks // 2):
        rows = slice(b * n, (b + 1) * n)
        cum = jnp.dot(tri, h1[rows], preferred_element_type=F32)
        cum = cum + jnp.dot(tri, h2[rows], preferred_element_type=F32)
        out.append(cum + jnp.dot(tri, h3[rows], preferred_element_type=F32))
    return jnp.concatenate(out, axis=0)


def _bd(w, m):
    return jnp.concatenate([w.astype(BF16)] * WKV_GROUP, axis=0) * m["bd"]


def _mm(x, w_bd):
    return jnp.dot(x.astype(BF16), w_bd, preferred_element_type=F32)


def _mm_nt(x, w_bd):
    return lax.dot_general(x.astype(BF16), w_bd, (((1,), (1,)), ((), ())),
                           preferred_element_type=F32)


def _seg_sum(x, m):
    hi = x.astype(BF16)
    lo = (x - hi.astype(F32)).astype(BF16)
    s = jnp.dot(jnp.concatenate([hi, lo], axis=0), m["bd"], preferred_element_type=F32)
    return s[:x.shape[0]] + s[x.shape[0]:]


def _run_interleaved(*gens):
    gens = list(gens)
    while gens:
        for g in list(gens):
            try:
                next(g)
            except StopIteration:
                gens.remove(g)


def _unit_lower_inverse(n_mats, m):
    row, col = m["row"], m["col"]
    eye = jnp.where(row == col, 1.0, 0.0)
    same_base = (row // WKV_INV_BASE) == (col // WKV_INV_BASE)
    c = WKV_CHUNK
    pws = [jnp.where(same_base, n, 0.0) for n in n_mats]
    ts = [eye + p for p in pws]
    pws = [_mm(p, _bd(p, m)) for p in pws]
    yield
    n_levels = WKV_INV_BASE.bit_length() - 2
    for lvl in range(n_levels):
        pw_bds = [_bd(p, m) for p in pws]
        if lvl < n_levels - 1:
            both = [_mm(jnp.concatenate([t, p], axis=0), b) for t, p, b in zip(ts, pws, pw_bds)]
            ts = [t + bo[:c] for t, bo in zip(ts, both)]
            pws = [bo[c:] for bo in both]
        else:
            ts = [t + _mm(t, b) for t, b in zip(ts, pw_bds)]
        yield
    blk = WKV_INV_BASE
    while blk < WKV_CHUNK:
        same_lo = (row // blk) == (col // blk)
        same_hi = (row // (2 * blk)) == (col // (2 * blk))
        sel = jnp.logical_and(same_hi, jnp.logical_not(same_lo))
        halves = [_mm(t, _bd(jnp.where(sel, n, 0.0), m)) for t, n in zip(ts, n_mats)]
        yield
        ts = [t + _mm(h, _bd(t, m)) for t, h in zip(ts, halves)]
        yield
        blk *= 2
    return ts


def _wkv_products(probs, m, res):
    c = WKV_CHUNK
    ts = yield from _unit_lower_inverse([p["a_ab"] for p in probs], m)
    vg_bds = [_bd(p["vg"], m) for p in probs]
    akv_yvs = [_mm(p["ak"], vb) for p, vb in zip(probs, vg_bds)]
    yield
    t_bfs = [t.astype(BF16) for t in ts]
    xas = [_mm(t, _bd(p["a0"], m)) for t, p in zip(t_bfs, probs)]
    yield
    res["xvs"] = [_mm(t, _bd(ay[:c], m)) for t, ay in zip(t_bfs, akv_yvs)]
    res["xrs"] = [jnp.concatenate([xa, p["r0"]], axis=0).astype(BF16) for xa, p in zip(xas, probs)]
    res["yvs"] = [ay[c:] for ay in akv_yvs]
    yield


def _wkv_recurrence(probs, res, decays, states, ys, m):
    c = WKV_CHUNK
    n_groups = len(states)
    for c0 in range(0, len(probs), n_groups):
        us = []
        for gi in range(n_groups):
            i = c0 + gi
            uy = _mm(res["xrs"][i], _bd(states[gi], m))
            us.append((uy[:c] + res["xvs"][i], uy[c:]))
        yield
        for gi in range(n_groups):
            i = c0 + gi
            p = probs[i]
            u, y0 = us[gi]
            ys.append(y0 + _mm(p["a_rb"], _bd(u, m)) + res["yvs"][i])
            uv = jnp.concatenate([u, p["vg"]], axis=0).astype(BF16)
            uv_bd = jnp.concatenate([uv] * WKV_GROUP, axis=0) * m["bd2"]
            states[gi] = states[gi] * decays[c0 + gi] + jnp.dot(
                p["bk_t"], uv_bd, preferred_element_type=F32)
        yield


def _wkv_prepare(r, k, v, lw, cum, kn, a, m):
    c = r.shape[0]
    mid = cum[c // 2 - 1:c // 2, :]
    end = cum[c - 1:c, :]
    e_abs = jnp.exp(cum)
    e_abs_prev = jnp.exp(cum - lw)
    e_mid = jnp.exp(-mid)
    e_neg = jnp.exp(mid - cum)
    e_end = jnp.exp(end - cum)
    g_end = jnp.exp(end)
    bvec = kn * a
    r0 = r * e_abs
    a0 = -kn * e_abs_prev
    rt = r0 * e_mid
    at = a0 * e_mid
    kt = k * e_neg
    bt = bvec * e_neg
    kh = k * e_end
    bh = bvec * e_end
    out = []
    for gi in range(N_RWKV_HEADS // WKV_GROUP):
        sl = slice(gi * GW, (gi + 1) * GW)
        lhs = jnp.concatenate([at[:, sl], rt[:, sl]], axis=0)
        ab = jnp.where(m["strict_incl"], _mm_nt(lhs, _bd(bt[:, sl], m)), 0.0)
        ak = jnp.where(m["strict_incl"], _mm_nt(lhs, _bd(kt[:, sl], m)), 0.0)
        bk_t = jnp.transpose(jnp.concatenate([bh[:, sl], kh[:, sl]], axis=0))
        bk_t = jnp.concatenate([bk_t[hh * HEAD_DIM:(hh + 1) * HEAD_DIM] for hh in range(WKV_GROUP)],
                               axis=1).astype(BF16)
        g_diag = jnp.where(m["row"] == m["col"], g_end[:, sl], 0.0)
        out.append(dict(a_ab=ab[:c], a_rb=ab[c:], ak=ak, vg=v[:, sl], a0=a0[:, sl], r0=r0[:, sl],
                        g_diag=g_diag, bk_t=bk_t))
    return out


def _wkv_kernel(r_ref, k_ref, v_ref, lw_ref, kn_ref, a_ref, g_ref, rk_ref, lg_ref, lb_ref,
                o_ref, s_scr, *, n_chunks):
    @pl.when(pl.program_id(1) == 0)
    def _():
        s_scr[...] = jnp.zeros_like(s_scr)

    n_groups = N_RWKV_HEADS // WKV_GROUP
    m = _wkv_masks()
    chunk_rows = [slice(ci * WKV_CHUNK, (ci + 1) * WKV_CHUNK) for ci in range(n_chunks)]

    c = WKV_CHUNK
    lw_all = lw_ref[...]
    cum_all = _chunk_cumsum(lw_all, n_chunks)
    probs = []
    for rows in chunk_rows:
        probs += _wkv_prepare(r_ref[rows, :], k_ref[rows, :], v_ref[rows, :], lw_all[rows, :],
                              cum_all[rows, :], kn_ref[rows, :], a_ref[rows, :], m)
    n_probs = len(probs)
    split = lambda stacked: [stacked[i * c:(i + 1) * c] for i in range(n_probs)]
    decays = split(_seg_sum(jnp.concatenate([p["g_diag"] for p in probs], axis=0), m))
    rkr = r_ref[...] * k_ref[...] * rk_ref[...]
    bonus_w = split(_seg_sum(jnp.concatenate(
        [rkr[rows, gi * GW:(gi + 1) * GW] for rows in chunk_rows for gi in range(n_groups)], axis=0), m))

    ys = []

    def outputs(i0, i1):
        y_all = jnp.concatenate(ys[i0:i1], axis=0)
        d_all = y_all - _seg_sum(y_all, m) * (1.0 / HEAD_DIM)
        yield
        inv_all = lax.rsqrt(_seg_sum(d_all * d_all, m) * (1.0 / HEAD_DIM) + RWKV_GN_EPS)
        yield
        for i in range(i0, i1):
            rows = chunk_rows[i // n_groups]
            sl = slice((i % n_groups) * GW, (i % n_groups + 1) * GW)
            at = slice((i - i0) * c, (i - i0 + 1) * c)
            yn = d_all[at] * inv_all[at] * lg_ref[:, sl] + lb_ref[:, sl]
            o_ref[rows, sl] = ((yn + bonus_w[i] * probs[i]["vg"]) * g_ref[rows, sl]).astype(BF16)
            yield

    h = (n_chunks // 2) * n_groups
    res_a, res_b = {}, {}
    states = [s_scr[gi] for gi in range(n_groups)]
    _run_interleaved(_wkv_products(probs[:h], m, res_a))
    _run_interleaved(_wkv_recurrence(probs[:h], res_a, decays[:h], states, ys, m),
                     _wkv_products(probs[h:], m, res_b))
    _run_interleaved(_wkv_recurrence(probs[h:], res_b, decays[h:], states, ys, m),
                     outputs(0, h))
    for gi in range(n_groups):
        s_scr[gi] = states[gi]
    _run_interleaved(outputs(h, n_probs))


def _wkv(r, k, v, lw, kn, a, g, r_k, lnx_g, lnx_b, *, batch, seq, rows):
    T = r.shape[0]
    nb = seq // rows
    blk = pl.BlockSpec((rows, D_RWKV), lambda b, i: (b * nb + i, 0))
    par = pl.BlockSpec((1, D_RWKV), lambda b, i: (0, 0))
    return pl.pallas_call(
        functools.partial(_wkv_kernel, n_chunks=rows // WKV_CHUNK),
        grid=(batch, nb),
        in_specs=[blk] * 7 + [par] * 3,
        out_specs=blk,
        out_shape=jax.ShapeDtypeStruct((T, D_RWKV), BF16),
        scratch_shapes=[pltpu.VMEM((N_RWKV_HEADS // WKV_GROUP, HEAD_DIM, GW), F32)],
        compiler_params=_cparams(("parallel", "arbitrary")),
        name="wkv7",
    )(r, k, v, lw, kn, a, g, r_k.reshape(1, -1), lnx_g.reshape(1, -1), lnx_b.reshape(1, -1))


def _dattn_kernel(q_ref, k_ref, v_ref, lq1_ref, lk1_ref, lq2_ref, lk2_ref, sg_ref, o_ref,
                  qs_scr, m_scr, l_scr, acc_scr, s0_scr, s1_scr, p0_scr, p1_scr, al0_scr, al1_scr,
                  *, tq, lam_init):
    qi = pl.program_id(2)
    q = q_ref[...]
    lane = lax.broadcasted_iota(jnp.int32, q.shape, 1)
    zero = jnp.zeros_like(q)
    qs_scr[:tq, :] = jnp.where(lane < HEAD_DIM, q, zero)
    qs_scr[tq:, :] = jnp.where(lane < HEAD_DIM, zero, q)
    m_scr[...] = jnp.full_like(m_scr, -jnp.inf)
    l_scr[...] = jnp.zeros_like(l_scr)
    acc_scr[...] = jnp.zeros_like(acc_scr)

    bufs = ((s0_scr, p0_scr, al0_scr), (s1_scr, p1_scr, al1_scr))

    def scores(j, par):
        start = pl.multiple_of(j * tq, tq)
        bufs[par][0][...] = _dot_nt(qs_scr[...], k_ref[pl.ds(start, tq), :])

    def softmax(par, masked):
        s_ref, p_ref, al_ref = bufs[par]
        s = s_ref[...]
        if masked:
            r_pos = lax.broadcasted_iota(jnp.int32, s.shape, 0) % tq
            c_pos = lax.broadcasted_iota(jnp.int32, s.shape, 1)
            s = jnp.where(c_pos <= r_pos, s, -jnp.inf)
        m_prev = m_scr[...]
        m_new = jnp.maximum(m_prev, jnp.max(s, axis=-1, keepdims=True))
        alpha = jnp.exp2(m_prev - m_new)
        p = jnp.exp2(s - jnp.concatenate([m_new] * (tq // LANES), axis=1))
        l_scr[...] = alpha * l_scr[...] + jnp.sum(p, axis=-1, keepdims=True)
        m_scr[...] = m_new
        p_ref[...] = p.astype(BF16)
        al_ref[...] = alpha

    def values(j, par):
        _, p_ref, al_ref = bufs[par]
        start = pl.multiple_of(j * tq, tq)
        acc_scr[...] = al_ref[...] * acc_scr[...] + jnp.dot(
            p_ref[...], v_ref[pl.ds(start, tq), :], preferred_element_type=F32)

    @pl.when(qi == 0)
    def _():
        scores(0, 0)
        softmax(0, True)
        values(0, 0)

    @pl.when(qi > 0)
    def _():
        scores(0, 0)
        scores(1, 1)
        softmax(0, False)

        def pair(t):
            values(t - 2, 0)
            scores(t, 0)
            softmax(1, False)
            values(t - 1, 1)
            scores(t + 1, 1)
            softmax(0, False)

        n_pairs = (qi - 1) // 2

        def body(i, carry):
            pair(2 + 4 * i)
            pair(4 + 4 * i)
            return carry

        lax.fori_loop(0, n_pairs // 2, body, 0)

        @pl.when(n_pairs % 2 == 1)
        def _():
            pair(2 * n_pairs)

        @pl.when(qi % 2 == 1)
        def _():
            values(qi - 1, 0)
            softmax(1, True)
            values(qi, 1)

        @pl.when(qi % 2 == 0)
        def _():
            values(qi - 2, 0)
            scores(qi, 0)
            softmax(1, False)
            values(qi - 1, 1)
            softmax(0, True)
            values(qi, 0)

    lam = (jnp.exp(jnp.sum(lq1_ref[...] * lk1_ref[...], axis=-1, keepdims=True))
           - jnp.exp(jnp.sum(lq2_ref[...] * lk2_ref[...], axis=-1, keepdims=True)) + lam_init)
    on = acc_scr[...] / l_scr[...]
    o = on[:tq, :] - lam * on[tq:, :]
    o = o * lax.rsqrt(jnp.mean(o * o, axis=-1, keepdims=True) + 1e-5) * sg_ref[...]
    o_ref[...] = (o * (1.0 - lam_init)).astype(BF16)


def _dattn(q, k, v, lq1, lk1, lq2, lk2, subln_g, *, batch, seq, tq, lam_init):
    T = q.shape[0]
    nq = seq // tq
    hd = 2 * HEAD_DIM
    qmap = lambda b, h, i: (b * nq + i, h)
    kvmap = lambda b, h, i: (b, h)
    par = lambda b, h, i: (0, 0)
    vec = lambda a: a.reshape(1, -1)
    return pl.pallas_call(
        functools.partial(_dattn_kernel, tq=tq, lam_init=lam_init),
        grid=(batch, N_DIFF_HEADS, nq),
        in_specs=[
            pl.BlockSpec((tq, hd), qmap),
            pl.BlockSpec((seq, hd), kvmap),
            pl.BlockSpec((seq, hd), kvmap),
            pl.BlockSpec((1, HEAD_DIM), par),
            pl.BlockSpec((1, HEAD_DIM), par),
            pl.BlockSpec((1, HEAD_DIM), par),
            pl.BlockSpec((1, HEAD_DIM), par),
            pl.BlockSpec((1, hd), par),
        ],
        out_specs=pl.BlockSpec((tq, hd), qmap),
        out_shape=jax.ShapeDtypeStruct((T, D_DIFF), BF16),
        scratch_shapes=[
            pltpu.VMEM((2 * tq, hd), BF16),
            pltpu.VMEM((2 * tq, LANES), F32),
            pltpu.VMEM((2 * tq, LANES), F32),
            pltpu.VMEM((2 * tq, hd), F32),
            pltpu.VMEM((2 * tq, tq), F32),
            pltpu.VMEM((2 * tq, tq), F32),
            pltpu.VMEM((2 * tq, tq), BF16),
            pltpu.VMEM((2 * tq, tq), BF16),
            pltpu.VMEM((2 * tq, LANES), F32),
            pltpu.VMEM((2 * tq, LANES), F32),
        ],
        compiler_params=_cparams(("parallel", "parallel", "arbitrary")),
        name="diff_attn",
    )(q, k, v, vec(lq1), vec(lk1), vec(lq2), vec(lk2), vec(subln_g))


def _proj_ln_kernel(yr_ref, yd_ref, w_ref, x_ref, g_ref, b_ref, o_ref):
    mix = jnp.dot(yr_ref[...], w_ref[:D_RWKV, :], preferred_element_type=F32)
    mix = mix + jnp.dot(yd_ref[...], w_ref[D_RWKV:, :], preferred_element_type=F32)
    o_ref[...] = _layer_norm(DEEPNORM_ALPHA * x_ref[...] + mix, g_ref[...], b_ref[...])


def _proj_ln(yr, yd, w_out, x2, g, b, *, tm):
    T = x2.shape[0]
    row = lambda i: (i, 0)
    full = lambda i: (0, 0)
    return pl.pallas_call(
        _proj_ln_kernel,
        grid=(T // tm,),
        in_specs=[
            pl.BlockSpec((tm, D_RWKV), row),
            pl.BlockSpec((tm, D_DIFF), row),
            pl.BlockSpec((D_RWKV + D_DIFF, D_MODEL), full),
            pl.BlockSpec((tm, D_MODEL), row),
            pl.BlockSpec((1, D_MODEL), full),
            pl.BlockSpec((1, D_MODEL), full),
        ],
        out_specs=pl.BlockSpec((tm, D_MODEL), row),
        out_shape=jax.ShapeDtypeStruct((T, D_MODEL), F32),
        compiler_params=_cparams(("parallel",)),
        name="ab_out_ln",
    )(yr, yd, w_out, x2, g.reshape(1, -1), b.reshape(1, -1))


def _gmlp_kernel(x_ref, win_ref, bin_ref, lng_ref, lnb_ref, ws_ref, bs_ref, wout_ref, g_ref, b_ref,
                 o_ref, gated_scr, *, tm):
    x = x_ref[...]
    xb = x.astype(BF16)

    def gelu(h):
        return 0.5 * h * (1.0 + lax.erf(h * (0.5 ** 0.5)))

    hv = jnp.dot(xb, win_ref[:, D_GMLP:], preferred_element_type=F32) + bin_ref[:, D_GMLP:]
    hu = jnp.dot(xb, win_ref[:, :D_GMLP], preferred_element_type=F32) + bin_ref[:, :D_GMLP]
    v = _layer_norm(gelu(hv), lng_ref[...], lnb_ref[...]).astype(BF16)
    u = gelu(hu)
    row = lax.broadcasted_iota(jnp.int32, (CHUNK, CHUNK), 0)
    col = lax.broadcasted_iota(jnp.int32, (CHUNK, CHUNK), 1)
    gw = D_GMLP // GMLP_GROUPS
    n_chunks = tm // CHUNK
    for gi in range(GMLP_GROUPS):
        cs = slice(gi * gw, (gi + 1) * gw)
        ws = jnp.where(row >= col, ws_ref[gi], 0.0).astype(BF16)
        v_side = jnp.concatenate([v[c * CHUNK:(c + 1) * CHUNK, cs] for c in range(n_chunks)], axis=1)
        mixed = jnp.dot(ws, v_side, preferred_element_type=F32)
        for c in range(n_chunks):
            rs = slice(c * CHUNK, (c + 1) * CHUNK)
            gated_scr[rs, cs] = (u[rs, cs] * (mixed[:, c * gw:(c + 1) * gw] + bs_ref[gi])).astype(BF16)
    mix = jnp.dot(gated_scr[...], wout_ref[...], preferred_element_type=F32)
    o_ref[...] = _layer_norm(DEEPNORM_ALPHA * x + mix, g_ref[...], b_ref[...])


def _gmlp(x2, w_in, b_in, ln_g, ln_b, w_s, b_s, w_out, g, b, *, tm):
    T = x2.shape[0]
    row = lambda i: (i, 0)
    full = lambda i: (0, 0)
    full3 = lambda i: (0, 0, 0)
    vec = lambda a: a.reshape(1, -1)
    gw = D_GMLP // GMLP_GROUPS
    bs_b = jnp.broadcast_to(b_s[:, :, None], (GMLP_GROUPS, CHUNK, gw))
    return pl.pallas_call(
        functools.partial(_gmlp_kernel, tm=tm),
        grid=(T // tm,),
        in_specs=[
            pl.BlockSpec((tm, D_MODEL), row),
            pl.BlockSpec((D_MODEL, 2 * D_GMLP), full, pipeline_mode=pl.Buffered(1)),
            pl.BlockSpec((1, 2 * D_GMLP), full),
            pl.BlockSpec((1, D_GMLP), full),
            pl.BlockSpec((1, D_GMLP), full),
            pl.BlockSpec((GMLP_GROUPS, CHUNK, CHUNK), full3),
            pl.BlockSpec((GMLP_GROUPS, CHUNK, gw), full3),
            pl.BlockSpec((D_GMLP, D_MODEL), full, pipeline_mode=pl.Buffered(1)),
            pl.BlockSpec((1, D_MODEL), full),
            pl.BlockSpec((1, D_MODEL), full),
        ],
        out_specs=pl.BlockSpec((tm, D_MODEL), row),
        out_shape=jax.ShapeDtypeStruct((T, D_MODEL), F32),
        scratch_shapes=[pltpu.VMEM((tm, D_GMLP), BF16)],
        compiler_params=_cparams(("parallel",)),
        name="gmlp",
    )(x2, w_in, vec(b_in), vec(ln_g), vec(ln_b), w_s, bs_b, w_out, vec(g), vec(b))


def _ffn_kernel(x_ref, xh_ref, wup_ref, cw_ref, cb_ref, wd_ref, g_ref, b_ref, o_ref,
                *, blocks_per_seq, tf, n_slabs):
    x = x_ref[...]
    tm = x.shape[0]
    xe = jnp.concatenate([x, xh_ref[...]], axis=0).astype(BF16)
    xb = xe[:tm]
    seq_start = pl.program_id(0) % blocks_per_seq == 0

    def up(j):
        cols = slice(j * tf, (j + 1) * tf)
        ge = jnp.dot(xe, wup_ref[:, cols], preferred_element_type=F32)
        val = jnp.dot(xb, wup_ref[:, D_FF + j * tf:D_FF + (j + 1) * tf], preferred_element_type=F32)
        return ge[:tm], val, jnp.where(seq_start, 0.0, ge[tm:])

    acc = None
    nxt = up(0)
    for j in range(n_slabs):
        gate, val, gh = nxt
        if j + 1 < n_slabs:
            nxt = up(j + 1)
        cols = slice(j * tf, (j + 1) * tf)
        cw = cw_ref[:, cols]
        conv = (cb_ref[:, cols] + cw[0:1, :] * _shift_rows(gate, 2, gh)
                + cw[1:2, :] * _shift_rows(gate, 1, gh) + cw[2:3, :] * gate)
        hid = conv * jax.nn.sigmoid(conv) * val
        down = jnp.dot(hid.astype(BF16), wd_ref[j], preferred_element_type=F32)
        acc = down if acc is None else acc + down
    o_ref[...] = _layer_norm(DEEPNORM_ALPHA * x + acc, g_ref[...], b_ref[...])


def _ffn(x2, w_up, conv_w, conv_b, w_down, g, b, *, seq, tm, tf):
    T = x2.shape[0]
    n_slabs = D_FF // tf
    bps = seq // tm
    wup = w_up.astype(BF16)
    cb = conv_b.reshape(1, D_FF)
    wd = w_down.reshape(n_slabs, tf, D_MODEL).astype(BF16)
    row = lambda i: (i, 0)
    full = lambda i: (0, 0)
    full3 = lambda i: (0, 0, 0)
    halo = lambda i: (jnp.maximum(i * (tm // HALO_ROWS) - 1, 0), 0)
    once = pl.Buffered(1)
    return pl.pallas_call(
        functools.partial(_ffn_kernel, blocks_per_seq=bps, tf=tf, n_slabs=n_slabs),
        grid=(T // tm,),
        in_specs=[
            pl.BlockSpec((tm, D_MODEL), row),
            pl.BlockSpec((HALO_ROWS, D_MODEL), halo),
            pl.BlockSpec((D_MODEL, 2 * D_FF), full, pipeline_mode=once),
            pl.BlockSpec((3, D_FF), full, pipeline_mode=once),
            pl.BlockSpec((1, D_FF), full, pipeline_mode=once),
            pl.BlockSpec((n_slabs, tf, D_MODEL), full3, pipeline_mode=once),
            pl.BlockSpec((1, D_MODEL), full),
            pl.BlockSpec((1, D_MODEL), full),
        ],
        out_specs=pl.BlockSpec((tm, D_MODEL), row),
        out_shape=jax.ShapeDtypeStruct((T, D_MODEL), F32),
        compiler_params=_cparams(("parallel",)),
        name="conv_ffn",
    )(x2, x2, wup, conv_w, cb, wd, g.reshape(1, -1), b.reshape(1, -1))


def _rope_tables(seq):
    half = ROPE_DIM // 2
    inv_freq = ROPE_THETA ** (-jnp.arange(0, ROPE_DIM, 2, dtype=F32) / ROPE_DIM)
    ang = jnp.arange(seq, dtype=F32)[:, None] * inv_freq[None, :]
    cos, sin = jnp.cos(ang), jnp.sin(ang)
    pos = jnp.arange(LANES) % HEAD_DIM
    spread = (jnp.arange(half)[:, None] == (pos % half)[None, :]).astype(F32)
    cos_l = jnp.dot(cos, spread, precision=HIGHEST)
    sin_l = jnp.dot(sin, spread, precision=HIGHEST)
    c = jnp.where(pos < ROPE_DIM, cos_l, 1.0)
    s1 = jnp.where((pos >= half) & (pos < ROPE_DIM), sin_l, 0.0)
    s2 = jnp.where(pos < half, -sin_l, 0.0)
    return c, s1, s2


def kernel(x, ab_w_in, ab_shift_mu, ab_w0, ab_w2, ab_a0, ab_a2, ab_g2, ab_k_k, ab_k_a, ab_r_k, ab_lnx_g, ab_lnx_b, ab_lam_q1, ab_lam_k1, ab_lam_q2, ab_lam_k2, ab_subln_g, ab_w_out, c_w_in, c_b_in, c_ln_g, c_ln_b, c_w_s, c_b_s, c_w_out, ln1_g, ln1_b, ffn_w_up, ffn_conv_w, ffn_conv_b, ffn_w_down, ln2_g, ln2_b):
    batch, seq, d_model = x.shape
    assert d_model == D_MODEL and x.dtype == F32
    x2 = x.reshape(batch * seq, D_MODEL)
    rc, rs1, rs2 = _rope_tables(seq)
    tm_in = min(512, seq)
    tm_ffn = min(1024, seq)
    tm_gmlp = min(1024, seq)
    tq = min(512, seq)
    wkv_rows = min(512, seq)
    for rows in (tm_in, tm_ffn, tm_gmlp, tq, wkv_rows):
        assert seq % rows == 0 and rows % (2 * WKV_CHUNK) == 0, (seq, rows)
    for i in range(DEPTH):
        j = i // 2
        if i % 2 == 0:
            (r, k, v, lw, kn, a, g, qd, kd, vd) = _ab_in(
                x2, ab_w_in[j].astype(BF16), ab_shift_mu[j], ab_w0[j], ab_w2[j], ab_a0[j],
                ab_a2[j].astype(BF16), ab_g2[j].astype(BF16), ab_k_k[j], ab_k_a[j], rc, rs1, rs2,
                seq=seq, tm=tm_in)
            y_r = _wkv(r, k, v, lw, kn, a, g, ab_r_k[j], ab_lnx_g[j], ab_lnx_b[j],
                       batch=batch, seq=seq, rows=wkv_rows)
            lam_init = 0.8 - 0.6 * math.exp(-0.3 * i)
            y_d = _dattn(qd, kd, vd, ab_lam_q1[j], ab_lam_k1[j], ab_lam_q2[j], ab_lam_k2[j],
                         ab_subln_g[j], batch=batch, seq=seq, tq=tq, lam_init=lam_init)
            x2 = _proj_ln(y_r, y_d, ab_w_out[j].astype(BF16), x2, ln1_g[i], ln1_b[i], tm=tm_in)
        else:
            x2 = _gmlp(x2, c_w_in[j].astype(BF16), c_b_in[j], c_ln_g[j], c_ln_b[j], c_w_s[j],
                       c_b_s[j], c_w_out[j].astype(BF16), ln1_g[i], ln1_b[i], tm=tm_gmlp)
        x2 = _ffn(x2, ffn_w_up[i], ffn_conv_w[i], ffn_conv_b[i], ffn_w_down[i], ln2_g[i], ln2_b[i],
                  seq=seq, tm=tm_ffn, tf=256)
    return x2.reshape(batch, seq, D_MODEL)
```
